```python
import math
import jax, jax.numpy as jnp
from jax import lax
import numpy as np

D_MODEL = 2048
BATCH = 4
SEQ = 2048
DEPTH = 4
DEC_BATCH = 128
DEC_SEQ = 4
PAST_LEN = 16384
PAGE_SIZE = 128

N_EVEN = (DEPTH + 1) // 2
N_ODD = DEPTH // 2
DK = 128
DV = 128
H_A = (D_MODEL // 2) // DV
D_A = H_A * DV
CONV_A = 4
CHUNK_A = 64
D_B = D_MODEL - D_A
CONV_B = 3
D_C = D_MODEL
CONV_C = 31
N_EXPERTS = 32
TOP_K = 4
D_FF_EXPERT = D_MODEL
SWIGLU_ALPHA = 1.702
SWIGLU_LIMIT = 7.0
MOE_BLOCK = 128
IN_EVEN = 3 * H_A * DK + D_A + 2 * H_A + 3 * D_B
EPS = 1e-6

kernel_name = "hybrid_gdn_shortconv_conformer_moe_adaln_step"


def rms_norm(x, g):
    x32 = x.astype(jnp.float32)
    y = x32 * lax.rsqrt(jnp.mean(x32 * x32, axis=-1, keepdims=True) + EPS)
    return y.astype(x.dtype) * g


def layer_norm(x, g, b):
    x32 = x.astype(jnp.float32)
    mu = jnp.mean(x32, axis=-1, keepdims=True)
    xc = x32 - mu
    y = xc * lax.rsqrt(jnp.mean(xc * xc, axis=-1, keepdims=True) + EPS)
    return y.astype(x.dtype) * g + b


def l2_normalize(x):
    return x * lax.rsqrt(jnp.sum(x * x, axis=-1, keepdims=True) + EPS)


def causal_dwconv(x_ext, w):
    width, ch = w.shape
    return lax.conv_general_dilated(x_ext, w[:, None, :], window_strides=(1,), padding='VALID',
                                    dimension_numbers=('NWC', 'WIO', 'NWC'), feature_group_count=ch)


def gated_delta_rule(q, k, v, g, beta, s0):
    nb, length, nh, dk = q.shape
    dv = v.shape[-1]
    c = math.gcd(length, CHUNK_A)
    n = length // c

    def chunks(t):
        t = t.reshape((nb, n, c, nh) + t.shape[3:])
        return jnp.moveaxis(t, (1, 3), (0, 2))

    qc, kc, vc, gc, bc = map(chunks, (q, k, v, g, beta))
    gcum = jnp.cumsum(gc, axis=-1)
    idx = jnp.arange(c)
    causal = idx[:, None] >= idx[None, :]
    strict = idx[:, None] > idx[None, :]
    diff = gcum[..., :, None] - gcum[..., None, :]
    decay = jnp.where(causal, jnp.exp(jnp.where(causal, diff, 0.0)), 0.0)
    kb = kc * bc[..., None]
    vb = vc * bc[..., None]
    lower = jnp.where(strict, jnp.einsum('nbhid,nbhjd->nbhij', kb, kc) * decay, 0.0)
    tmat = lower + jnp.eye(c, dtype=lower.dtype)
    u = lax.linalg.triangular_solve(tmat, vb, left_side=True, lower=True, unit_diagonal=True)
    w = lax.linalg.triangular_solve(tmat, kb * jnp.exp(gcum)[..., None], left_side=True, lower=True,
                                    unit_diagonal=True)
    intra = jnp.where(causal, jnp.einsum('nbhid,nbhjd->nbhij', qc, kc) * decay, 0.0)
    q_dec = qc * jnp.exp(gcum)[..., None]
    k_dec = kc * jnp.exp(gcum[..., -1:] - gcum)[..., None]
    g_last = jnp.exp(gcum[..., -1])

    def step(s, xs):
        q_i, k_i, u_i, w_i, a_i, gl = xs
        v_new = u_i - jnp.einsum('bhck,bhkv->bhcv', w_i, s)
        o_i = jnp.einsum('bhck,bhkv->bhcv', q_i, s) + jnp.einsum('bhij,bhjv->bhiv', a_i, v_new)
        s = s * gl[..., None, None] + jnp.einsum('bhck,bhcv->bhkv', k_i, v_new)
        return s, o_i

    s_final, o = lax.scan(step, s0, (q_dec, k_dec, u, w, intra, g_last))
    o = jnp.moveaxis(o, (0, 2), (1, 3)).reshape(nb, length, nh, dv)
    return o, s_final


def even_mixer(h, conv_qkv_prev, conv_b_prev, s_prev, w_in, conv_qkv_w, a_log, dt_bias, o_norm_g,
               conv_b_w, w_out):
    nb, length, _ = h.shape
    proj = h @ w_in
    splits = np.cumsum([3 * H_A * DK, D_A, H_A, H_A, D_B, D_B]).tolist()
    qkv_raw, g_out, a_raw, b_raw, xb, c_pre, b_post = jnp.split(proj, splits, axis=-1)
    qkv_ext = jnp.concatenate([conv_qkv_prev, qkv_raw], axis=1)
    qkv = jax.nn.silu(causal_dwconv(qkv_ext, conv_qkv_w)).astype(jnp.float32)
    qkv = qkv.reshape(nb, length, 3, H_A, DK)
    q = l2_normalize(qkv[:, :, 0]) * (DK ** -0.5)
    k = l2_normalize(qkv[:, :, 1])
    v = qkv[:, :, 2]
    g = -jnp.exp(a_log.astype(jnp.float32)) * jax.nn.softplus(a_raw.astype(jnp.float32) + dt_bias.astype(jnp.float32))
    beta = jax.nn.sigmoid(b_raw.astype(jnp.float32))
    o, s_new = gated_delta_rule(q, k, v, g, beta, s_prev.astype(jnp.float32))
    o = rms_norm(o.astype(h.dtype), o_norm_g) * jax.nn.silu(g_out.reshape(nb, length, H_A, DV))
    u_ext = jnp.concatenate([conv_b_prev, c_pre * xb], axis=1)
    y_b = b_post * causal_dwconv(u_ext, conv_b_w)
    y = jnp.concatenate([o.reshape(nb, length, D_A), y_b], axis=-1) @ w_out
    return y, s_new.astype(s_prev.dtype), qkv_ext[:, -(CONV_A - 1):], u_ext[:, -(CONV_B - 1):]


def conformer_conv(h, prev, w_in, b_in, dw_w, dw_b, ln_g, ln_b, w_out, b_out):
    u = h @ w_in + b_in
    u = u[..., :D_C] * jax.nn.sigmoid(u[..., D_C:])
    u_ext = jnp.concatenate([prev, u], axis=1)
    z = causal_dwconv(u_ext, dw_w) + dw_b
    z = jax.nn.silu(layer_norm(z, ln_g, ln_b))
    return z @ w_out + b_out, u_ext[:, -(CONV_C - 1):]


def moe(h, layer, w_router, b_router, w_gate_up, b_gate_up, w_down, b_down):
    nb, length, d = h.shape
    x = h.reshape(nb * length, d)
    t = nb * length
    logits = (x @ w_router[layer] + b_router[layer]).astype(jnp.float32)
    top_logit, top_idx = lax.top_k(logits, TOP_K)
    gates = jax.nn.softmax(top_logit, axis=-1).astype(h.dtype)
    a_tot = t * TOP_K
    e_flat = top_idx.reshape(a_tot).astype(jnp.int32)
    tok_flat = jnp.arange(a_tot, dtype=jnp.int32) // TOP_K
    order = jnp.argsort(e_flat)
    e_sorted = e_flat[order]
    counts = jnp.bincount(e_flat, length=N_EXPERTS).astype(jnp.int32)
    padded = (counts + MOE_BLOCK - 1) // MOE_BLOCK * MOE_BLOCK
    pad_end = jnp.cumsum(padded)
    pad_start = pad_end - padded
    cnt_start = jnp.cumsum(counts) - counts
    dest = pad_start[e_sorted] + (jnp.arange(a_tot, dtype=jnp.int32) - cnt_start[e_sorted])
    slot = jnp.zeros((a_tot,), jnp.int32).at[order].set(dest)
    n_blocks = -(-(a_tot + N_EXPERTS * (MOE_BLOCK - 1)) // MOE_BLOCK)
    rows = n_blocks * MOE_BLOCK
    row_tok = jnp.zeros((rows,), jnp.int32).at[slot].set(tok_flat)
    blk_start = jnp.arange(n_blocks, dtype=jnp.int32) * MOE_BLOCK
    blk_expert = jnp.minimum(jnp.searchsorted(pad_end, blk_start, side='right'), N_EXPERTS - 1)

    def expert_block(args):
        tok, e = args
        xb = x[tok]
        gu = xb @ w_gate_up[layer, e] + b_gate_up[layer, e]
        gate = jnp.minimum(gu[:, :D_FF_EXPERT], SWIGLU_LIMIT)
        up = jnp.clip(gu[:, D_FF_EXPERT:], -SWIGLU_LIMIT, SWIGLU_LIMIT)
        act = (up + 1) * gate * jax.nn.sigmoid(SWIGLU_ALPHA * gate)
        return act @ w_down[layer, e] + b_down[layer, e]

    out_rows = lax.map(expert_block, (row_tok.reshape(n_blocks, MOE_BLOCK), blk_expert)).reshape(rows, d)
    y = jnp.einsum('tkd,tk->td', out_rows[slot].reshape(t, TOP_K, d), gates)
    return y.reshape(nb, length, d)


def modulation(c, w, b):
    m = jax.nn.silu(c) @ w + b
    return m.reshape(c.shape[0], 6, 1, D_MODEL)


def modulate(x, g, shift, scale):
    return rms_norm(x, g) * (1 + scale) + shift


def setup_inputs(seed: int = 0) -> dict:
    key = jax.random.key(seed)
    k = jax.random.split(key, 34)
    d = D_MODEL

    def nrm(i, shape, scale):
        return jax.random.normal(k[i], shape, jnp.float32) * scale

    dt = jnp.exp(jax.random.uniform(k[15], (N_EVEN, H_A), jnp.float32, math.log(1e-3), math.log(1e-1)))
    return {
        "x_prompt": nrm(0, (BATCH, SEQ, d), 1.0),
        "x_sample": nrm(1, (DEC_BATCH, DEC_SEQ, d), 1.0),
        "state_delta": nrm(2, (N_EVEN, DEC_BATCH, H_A, DK, DV), 0.5),
        "state_conv_qkv": nrm(3, (N_EVEN, DEC_BATCH, CONV_A - 1, 3 * H_A * DK), 1.0),
        "state_conv_b": nrm(4, (N_EVEN, DEC_BATCH, CONV_B - 1, D_B), 1.0),
        "state_conv_c": nrm(5, (N_ODD, DEC_BATCH, CONV_C - 1, D_C), 1.0),
        "c_prompt": nrm(6, (BATCH, d), 1.0),
        "c_sample": nrm(7, (DEC_BATCH, d), 1.0),
        "norm_mix": 1.0 + nrm(8, (DEPTH, d), 0.01),
        "norm_ffn": 1.0 + nrm(9, (DEPTH, d), 0.01),
        "w_mod": nrm(10, (DEPTH, d, 6 * d), 0.5 * d ** -0.5),
        "b_mod": nrm(11, (DEPTH, 6 * d), 0.01),
        "w_in_even": nrm(12, (N_EVEN, d, IN_EVEN), d ** -0.5),
        "conv_qkv_w": nrm(13, (N_EVEN, CONV_A, 3 * H_A * DK), CONV_A ** -0.5),
        "a_log": jnp.log(jax.random.uniform(k[14], (N_EVEN, H_A), jnp.float32, 1.0, 16.0)),
        "dt_bias": jnp.log(jnp.expm1(dt)),
        "o_norm_g": 1.0 + nrm(16, (N_EVEN, DV), 0.01),
        "conv_b_w": nrm(17, (N_EVEN, CONV_B, D_B), CONV_B ** -0.5),
        "w_out_even": nrm(18, (N_EVEN, D_A + D_B, d), (D_A + D_B) ** -0.5),
        "w_in_odd": nrm(19, (N_ODD, d, 2 * D_C), d ** -0.5),
        "b_in_odd": nrm(20, (N_ODD, 2 * D_C), 0.01),
        "dw_w": nrm(21, (N_ODD, CONV_C, D_C), CONV_C ** -0.5),
        "dw_b": nrm(22, (N_ODD, D_C), 0.01),
        "ln_g": 1.0 + nrm(23, (N_ODD, D_C), 0.01),
        "ln_b": nrm(24, (N_ODD, D_C), 0.01),
        "w_out_odd": nrm(25, (N_ODD, D_C, d), D_C ** -0.5),
        "b_out_odd": nrm(26, (N_ODD, d), 0.01),
        "w_router": nrm(27, (DEPTH, d, N_EXPERTS), d ** -0.5),
        "b_router": nrm(28, (DEPTH, N_EXPERTS), 0.01),
        "w_gate_up": nrm(29, (DEPTH, N_EXPERTS, d, 2 * D_FF_EXPERT), d ** -0.5),
        "b_gate_up": nrm(30, (DEPTH, N_EXPERTS, 2 * D_FF_EXPERT), 0.01),
        "w_down": nrm(31, (DEPTH, N_EXPERTS, D_FF_EXPERT, d), D_FF_EXPERT ** -0.5),
        "b_down": nrm(32, (DEPTH, N_EXPERTS, d), 0.01),
        "norm_final": 1.0 + nrm(33, (d,), 0.01),
    }


def reference(x_prompt, x_sample, state_delta, state_conv_qkv, state_conv_b, state_conv_c, c_prompt, c_sample,
              norm_mix, norm_ffn, w_mod, b_mod, w_in_even, conv_qkv_w, a_log, dt_bias, o_norm_g, conv_b_w,
              w_out_even, w_in_odd, b_in_odd, dw_w, dw_b, ln_g, ln_b, w_out_odd, b_out_odd, w_router, b_router,
              w_gate_up, b_gate_up, w_down, b_down, norm_final):
    xp, xs = x_prompt, x_sample
    bp = xp.shape[0]
    dt = xp.dtype
    dp_l, qp_l, bp_l, cp_l = [], [], [], []
    ds_l, qs_l, bs_l, cs_l = [], [], [], []
    moe_w = (w_router, b_router, w_gate_up, b_gate_up, w_down, b_down)
    for l in range(DEPTH):
        mp = modulation(c_prompt, w_mod[l], b_mod[l])
        ms = modulation(c_sample, w_mod[l], b_mod[l])
        hp = modulate(xp, norm_mix[l], mp[:, 0], mp[:, 1])
        hs = modulate(xs, norm_mix[l], ms[:, 0], ms[:, 1])
        if l % 2 == 0:
            i = l // 2
            ew = (w_in_even[i], conv_qkv_w[i], a_log[i], dt_bias[i], o_norm_g[i], conv_b_w[i], w_out_even[i])
            yp, sp, qp, cbp = even_mixer(hp, jnp.zeros((bp, CONV_A - 1, 3 * H_A * DK), dt),
                                         jnp.zeros((bp, CONV_B - 1, D_B), dt),
                                         jnp.zeros((bp, H_A, DK, DV), dt), *ew)
            ys, ss, qs, cbs = even_mixer(hs, state_conv_qkv[i], state_conv_b[i], state_delta[i], *ew)
            dp_l.append(sp); qp_l.append(qp); bp_l.append(cbp)
            ds_l.append(ss); qs_l.append(qs); bs_l.append(cbs)
        else:
            i = l // 2
            ow = (w_in_odd[i], b_in_odd[i], dw_w[i], dw_b[i], ln_g[i], ln_b[i], w_out_odd[i], b_out_odd[i])
            yp, ccp = conformer_conv(hp, jnp.zeros((bp, CONV_C - 1, D_C), dt), *ow)
            ys, ccs = conformer_conv(hs, state_conv_c[i], *ow)
            cp_l.append(ccp); cs_l.append(ccs)
        xp = xp + mp[:, 2] * yp
        xs = xs + ms[:, 2] * ys
        hp = modulate(xp, norm_ffn[l], mp[:, 3], mp[:, 4])
        hs = modulate(xs, norm_ffn[l], ms[:, 3], ms[:, 4])
        xp = xp + mp[:, 5] * moe(hp, l, *moe_w)
        xs = xs + ms[:, 5] * moe(hs, l, *moe_w)
    y_prompt = rms_norm(xp, norm_final)
    y_sample = rms_norm(xs, norm_final)
    return (y_prompt, y_sample,
            jnp.stack(dp_l), jnp.stack(qp_l), jnp.stack(bp_l), jnp.stack(cp_l),
            jnp.stack(ds_l), jnp.stack(qs_l), jnp.stack(bs_l), jnp.stack(cs_l))
```

```python
import functools
import math

import jax
import jax.numpy as jnp
from jax import lax
from jax.experimental import pallas as pl
from jax.experimental.pallas import tpu as pltpu

F32 = jnp.float32
BF16 = jnp.bfloat16
I32 = jnp.int32
EPS = 1e-6
HI = lax.Precision.HIGHEST

DK = 128
DV = 128
CONV_A = 4
CONV_B = 3
CONV_C = 31
CHUNK_A = 64
TOP_K = 4
SWIGLU_ALPHA = 1.702
SWIGLU_LIMIT = 7.0

V7X_LANES = 128
V7X_SUBLANES = 8
V7X_VMEM_LIMIT_BYTES = 56 * 1024 * 1024
MOE_ROWS = 256
GATHER_LAG = 16


def _params(*sem):
    return pltpu.CompilerParams(dimension_semantics=sem, vmem_limit_bytes=V7X_VMEM_LIMIT_BYTES)


def _mm(a, b):
    return jnp.dot(a.astype(BF16), b.astype(BF16), preferred_element_type=F32)


def _mm_hi(a, b):
    return jnp.dot(a, b, precision=HI, preferred_element_type=F32)


def _mm_nt_hi(a, b):
    return lax.dot_general(a, b, (((1,), (1,)), ((), ())), precision=HI, preferred_element_type=F32)


def _silu(x):
    return x * jax.nn.sigmoid(x)


def _rms(x):
    return x * lax.rsqrt(jnp.mean(x * x, axis=-1, keepdims=True) + EPS)


def _tile_mod(mp_ref, ms_ref, is_sample, reps):
    s = ms_ref[0]
    s = jnp.concatenate([s] * reps, axis=0)
    return jnp.where(is_sample, s, mp_ref[0, 0])


def _mod_specs(k, tps, bp, bs, d, ngrid):
    if ngrid == 1:
        mp = pl.BlockSpec((1, 1, 1, d), lambda i: (k, jnp.minimum(i // tps, bp - 1), 0, 0))
        ms = pl.BlockSpec((1, bs, d), lambda i: (k, 0, 0))
    else:
        mp = pl.BlockSpec((1, 1, 1, d), lambda i, j: (k, jnp.minimum(i // tps, bp - 1), 0, 0))
        ms = pl.BlockSpec((1, bs, d), lambda i, j: (k, 0, 0))
    return mp, ms


def _mod_kernel(c_ref, w_ref, b_ref, o_ref):
    o_ref[0, 0] = _mm(_silu(c_ref[...]), w_ref[0]) + b_ref[0]


def _modulation(c_all, w_mod, b_mod):
    depth, d, d6 = w_mod.shape
    rc = c_all.shape[0]
    tn = min(512, d)
    nj = d // tn
    return pl.pallas_call(
        _mod_kernel,
        out_shape=jax.ShapeDtypeStruct((depth, 6, rc, d), F32),
        grid=(depth, 6, nj),
        in_specs=[
            pl.BlockSpec((rc, d), lambda l, k, j: (0, 0)),
            pl.BlockSpec((1, d, tn), lambda l, k, j: (l, 0, k * nj + j)),
            pl.BlockSpec((1, 1, tn), lambda l, k, j: (l, 0, k * nj + j)),
        ],
        out_specs=pl.BlockSpec((1, 1, rc, tn), lambda l, k, j: (l, k, 0, j)),
        compiler_params=_params("parallel", "parallel", "parallel"),
        name="modulation",
    )(c_all, w_mod, b_mod.reshape(depth, 1, d6))


def _even_in_kernel(x_ref, g_ref, mpsh, mpsc, mssh, mssc, w_ref, wab_ref, proj_ref, ab_ref, hb_ref, *, ntp, reps):
    i = pl.program_id(0)

    @pl.when(pl.program_id(1) == 0)
    def _():
        is_s = i >= ntp
        h = _rms(x_ref[...]) * g_ref[...] * (1 + _tile_mod(mpsc, mssc, is_s, reps)) + _tile_mod(mpsh, mssh, is_s, reps)
        hb = h.astype(BF16)
        hb_ref[...] = hb
        ab_ref[...] = jnp.dot(hb, wab_ref[...].astype(BF16), preferred_element_type=F32)

    proj_ref[...] = jnp.dot(hb_ref[...], w_ref[...].astype(BF16), preferred_element_type=F32)


def _even_in(x, g, mp, ms, w_main, w_ab, dims):
    t, d = x.shape
    tm, ntp, tps, bp, bs, reps = dims
    nmain = w_main.shape[1]
    tn = nmain // 7
    nj = 7
    sh = _mod_specs(0, tps, bp, bs, d, 2)
    sc = _mod_specs(1, tps, bp, bs, d, 2)
    return pl.pallas_call(
        functools.partial(_even_in_kernel, ntp=ntp, reps=reps),
        out_shape=(jax.ShapeDtypeStruct((t, nmain), F32), jax.ShapeDtypeStruct((t, V7X_LANES), F32)),
        grid=(t // tm, nj),
        in_specs=[
            pl.BlockSpec((tm, d), lambda i, j: (i, 0)),
            pl.BlockSpec((1, d), lambda i, j: (0, 0)),
            sh[0], sc[0], sh[1], sc[1],
            pl.BlockSpec((d, tn), lambda i, j: (0, j)),
            pl.BlockSpec((d, V7X_LANES), lambda i, j: (0, 0)),
        ],
        out_specs=(pl.BlockSpec((tm, tn), lambda i, j: (i, j)), pl.BlockSpec((tm, V7X_LANES), lambda i, j: (i, 0))),
        scratch_shapes=[pltpu.VMEM((tm, d), BF16)],
        compiler_params=_params("parallel", "arbitrary"),
        name="even_in",
    )(x, g, mp, mp, ms, ms, w_main, w_ab)


def _causal_taps(ext_ref, r0, rb, lanes, w, width, stride, hr):
    acc = None
    if stride % V7X_SUBLANES == 0:
        for j in range(width):
            start = pl.multiple_of(r0 + (hr - (width - 1 - j) * stride), V7X_SUBLANES)
            term = ext_ref[pl.ds(start, rb), lanes] * w[j:j + 1, :]
            acc = term if acc is None else acc + term
        return acc
    look = -(-(width - 1) * stride // V7X_SUBLANES) * V7X_SUBLANES
    win = ext_ref[pl.ds(pl.multiple_of(r0 + (hr - look), V7X_SUBLANES), rb + look), lanes]
    for j in range(width):
        off = look - (width - 1 - j) * stride
        term = win[off:off + rb, :] * w[j:j + 1, :]
        acc = term if acc is None else acc + term
    return acc


def _evenprep_kernel(q_ref, k_ref, v_ref, xb_ref, cp_ref, bpost_ref, ab_ref,
                     hq_ref, hk_ref, hv_ref, hxb_ref, hcp_ref,
                     wqkv_ref, wb_ref, adt_ref,
                     qo_ref, ko_ref, vo_ref, gb_ref, yb_ref, ut_ref,
                     extq, extk, extv, extu, *, stride, hr, tps, zero_start, nh):
    tm = q_ref.shape[0]
    da = q_ref.shape[1]
    keep = jnp.logical_not(jnp.logical_and(zero_start, pl.program_id(0) % tps == 0)).astype(F32)
    for ext, halo, cur in ((extq, hq_ref, q_ref), (extk, hk_ref, k_ref), (extv, hv_ref, v_ref)):
        ext[pl.ds(0, hr), :] = halo[...] * keep
        ext[pl.ds(hr, tm), :] = cur[...]
    extu[pl.ds(0, hr), :] = hcp_ref[...] * hxb_ref[...] * keep
    extu[pl.ds(hr, tm), :] = cp_ref[...] * xb_ref[...]
    tr = ut_ref.shape[0]
    ut_ref[...] = extu[pl.ds(hr + tm - tr, tr), :]

    ab = ab_ref[...]
    adt = adt_ref[...]
    z = ab + adt[1:2, :]
    softplus = jnp.maximum(z, 0.0) + jnp.log(1.0 + jnp.exp(-jnp.abs(z)))
    lane = lax.broadcasted_iota(I32, ab.shape, 1)
    gb_ref[...] = jnp.where(lane < nh, -jnp.exp(adt[0:1, :]) * softplus, jax.nn.sigmoid(ab))

    wqkv = wqkv_ref[...]
    wb = wb_ref[...]
    rb = min(64, tm)

    def chunk(ci, carry):
        r0 = pl.multiple_of(ci * rb, rb)
        rows = pl.ds(r0, rb)
        for h in range(nh):
            sl = slice(h * DK, (h + 1) * DK)
            taps = lambda ext, w0: _causal_taps(ext, r0, rb, sl, wqkv[:, w0 + h * DK:w0 + (h + 1) * DK],
                                                CONV_A, stride, hr)
            qh = _silu(taps(extq, 0))
            kh = _silu(taps(extk, da))
            qo_ref[rows, sl] = qh * lax.rsqrt(jnp.sum(qh * qh, axis=-1, keepdims=True) + EPS) * (DK ** -0.5)
            ko_ref[rows, sl] = kh * lax.rsqrt(jnp.sum(kh * kh, axis=-1, keepdims=True) + EPS)
            vo_ref[rows, sl] = _silu(taps(extv, 2 * da))
            yb = bpost_ref[rows, sl] * _causal_taps(extu, r0, rb, sl, wb[:, sl], CONV_B, stride, hr)
            yb_ref[rows, sl] = yb.astype(BF16)
        return carry

    lax.fori_loop(0, tm // rb, chunk, 0)


def _evenprep(proj, ab, halos, wqkv, wb, adt, row0, ntiles, tm, stride, hr, tps, zero_start, nh):
    da = nh * DK
    tr = max(V7X_SUBLANES, (CONV_B - 1) * stride)
    cur = lambda c: pl.BlockSpec((tm, da), lambda i: (row0 + i, c))
    if halos is None:
        hb = tm // hr
        hspec = lambda c: pl.BlockSpec((hr, da), lambda i: (jnp.maximum((row0 + i) * hb - 1, 0), c))
        hargs = [proj] * 5
        hspecs = [hspec(0), hspec(1), hspec(2), hspec(4), hspec(5)]
    else:
        hargs = list(halos)
        hspecs = [pl.BlockSpec((hr, da), lambda i: (0, 0)) for _ in range(5)]
    rows = ntiles * tm
    full = lambda a: pl.BlockSpec(a.shape, lambda i: (0, 0))
    return pl.pallas_call(
        functools.partial(_evenprep_kernel, stride=stride, hr=hr, tps=tps, zero_start=zero_start, nh=nh),
        out_shape=(jax.ShapeDtypeStruct((rows, da), F32), jax.ShapeDtypeStruct((rows, da), F32),
                   jax.ShapeDtypeStruct((rows, da), F32), jax.ShapeDtypeStruct((rows, V7X_LANES), F32),
                   jax.ShapeDtypeStruct((rows, da), BF16), jax.ShapeDtypeStruct((ntiles * tr, da), F32)),
        grid=(ntiles,),
        in_specs=[cur(0), cur(1), cur(2), cur(4), cur(5), cur(6),
                  pl.BlockSpec((tm, V7X_LANES), lambda i: (row0 + i, 0))] + hspecs + [full(wqkv), full(wb), full(adt)],
        out_specs=(pl.BlockSpec((tm, da), lambda i: (i, 0)), pl.BlockSpec((tm, da), lambda i: (i, 0)),
                   pl.BlockSpec((tm, da), lambda i: (i, 0)), pl.BlockSpec((tm, V7X_LANES), lambda i: (i, 0)),
                   pl.BlockSpec((tm, da), lambda i: (i, 0)), pl.BlockSpec((tr, da), lambda i: (i, 0))),
        scratch_shapes=[pltpu.VMEM((hr + tm, da), F32) for _ in range(4)],
        compiler_params=_params("parallel"),
        name="evenprep",
    )(proj, proj, proj, proj, proj, proj, ab, *hargs, wqkv, wb, adt)


def _delta_kernel(*refs, nh, has_s0):
    if has_s0:
        q_ref, k_ref, v_ref, gb_ref, s0_ref, o_ref, sout_ref, s_ref = refs
    else:
        q_ref, k_ref, v_ref, gb_ref, o_ref, sout_ref, s_ref = refs
    n = pl.program_id(1)
    c = q_ref.shape[0]

    @pl.when(n == 0)
    def _():
        s_ref[...] = s0_ref[0, 0] if has_s0 else jnp.zeros(s_ref.shape, F32)

    row = lax.broadcasted_iota(I32, (c, c), 0)
    col = lax.broadcasted_iota(I32, (c, c), 1)
    causal = row >= col
    strict = row > col
    eye = (row == col).astype(F32)
    lrow = lax.broadcasted_iota(I32, (V7X_LANES, V7X_LANES), 0)
    lcol = lax.broadcasted_iota(I32, (V7X_LANES, V7X_LANES), 1)
    eye_l = (lrow == lcol).astype(F32)

    gb = gb_ref[...]
    gcum = _mm_hi(causal.astype(F32), gb)
    gcum_t = _mm_nt_hi(eye_l, gcum)
    levels = int(math.log2(c))

    for h in range(nh):
        sl = slice(h * DK, (h + 1) * DK)
        qh, kh, vh = q_ref[:, sl], k_ref[:, sl], v_ref[:, sl]
        g_col = gcum[:, h:h + 1]
        g_row = gcum_t[h:h + 1, :]
        beta = gb[:, nh + h:nh + h + 1]
        diff = g_col - g_row
        decay = jnp.where(causal, jnp.exp(jnp.where(causal, diff, 0.0)), 0.0)
        kb = kh * beta
        vb = vh * beta
        low = jnp.where(strict, _mm_nt_hi(kb, kh) * decay, 0.0)
        inv = eye - low
        pw = low
        for _ in range(levels - 1):
            pw = _mm_hi(pw, pw)
            inv = inv + _mm_hi(inv, pw)
        eg = jnp.exp(g_col)
        uw = _mm_hi(inv, jnp.concatenate([vb, kb * eg], axis=1))
        u, w = uw[:, :DV], uw[:, DV:]
        intra = jnp.where(causal, _mm_nt_hi(qh, kh) * decay, 0.0)
        g_last = gcum[c - 1:c, h:h + 1]
        k_dec_t = _mm_nt_hi(eye_l, kh * jnp.exp(g_last - g_col))
        s = s_ref[h]
        v_new = u - _mm_hi(w, s)
        o_ref[:, sl] = _mm_hi(qh * eg, s) + _mm_hi(intra, v_new)
        s_ref[h] = s * jnp.exp(g_last) + _mm_hi(k_dec_t, v_new)

    @pl.when(n == pl.num_programs(1) - 1)
    def _():
        sout_ref[0] = s_ref[...]


def _delta(q, k, v, gb, s0, li, nb, nchunks, c, nh):
    da = nh * DK
    rowspec = lambda w: pl.BlockSpec((c, w), lambda b, n: (b * nchunks + n, 0))
    sspec = pl.BlockSpec((1, nh, DK, DV), lambda b, n: (b, 0, 0, 0))
    in_specs = [rowspec(da), rowspec(da), rowspec(da), rowspec(V7X_LANES)]
    args = [q, k, v, gb]
    if s0 is not None:
        in_specs.append(pl.BlockSpec((1, 1, nh, DK, DV), lambda b, n: (li, b, 0, 0, 0)))
        args.append(s0)
    return pl.pallas_call(
        functools.partial(_delta_kernel, nh=nh, has_s0=s0 is not None),
        out_shape=(jax.ShapeDtypeStruct((nb * nchunks * c, da), F32), jax.ShapeDtypeStruct((nb, nh, DK, DV), F32)),
        grid=(nb, nchunks),
        in_specs=in_specs,
        out_specs=(rowspec(da), sspec),
        scratch_shapes=[pltpu.VMEM((nh, DK, DV), F32)],
        compiler_params=_params("parallel", "arbitrary"),
        name="delta_rule",
    )(*args)


def _even_out_kernel(o_ref, gout_ref, yb_ref, og_ref, mpg, msg, w_ref, x_ref, xo_ref, yin_ref, *, ntp, reps, nh):
    i = pl.program_id(0)
    da = nh * DV

    @pl.when(pl.program_id(1) == 0)
    def _():
        for h in range(nh):
            sl = slice(h * DV, (h + 1) * DV)
            yin_ref[:, sl] = (_rms(o_ref[:, sl]) * og_ref[...] * _silu(gout_ref[:, sl])).astype(BF16)
        yin_ref[:, da:] = yb_ref[...]

    y = jnp.dot(yin_ref[...], w_ref[0].astype(BF16), preferred_element_type=F32)
    xo_ref[...] = x_ref[...] + _tile_mod(mpg, msg, i >= ntp, reps) * y


def _even_out(o, proj, yb, og, mp, ms, w_out, li, x, dims, nh):
    t, d = x.shape
    tm, ntp, tps, bp, bs, reps = dims
    da = nh * DV
    tn = min(1024, d)
    nj = d // tn
    mpg = pl.BlockSpec((1, 1, 1, tn), lambda i, j: (2, jnp.minimum(i // tps, bp - 1), 0, j))
    msg = pl.BlockSpec((1, bs, tn), lambda i, j: (2, 0, j))
    return pl.pallas_call(
        functools.partial(_even_out_kernel, ntp=ntp, reps=reps, nh=nh),
        out_shape=jax.ShapeDtypeStruct((t, d), F32),
        grid=(t // tm, nj),
        in_specs=[
            pl.BlockSpec((tm, da), lambda i, j: (i, 0)),
            pl.BlockSpec((tm, da), lambda i, j: (i, 3)),
            pl.BlockSpec((tm, yb.shape[1]), lambda i, j: (i, 0)),
            pl.BlockSpec((1, DV), lambda i, j: (0, 0)),
            mpg, msg,
            pl.BlockSpec((1, w_out.shape[1], tn), lambda i, j: (li, 0, j)),
            pl.BlockSpec((tm, tn), lambda i, j: (i, j)),
        ],
        out_specs=pl.BlockSpec((tm, tn), lambda i, j: (i, j)),
        scratch_shapes=[pltpu.VMEM((tm, w_out.shape[1]), BF16)],
        compiler_params=_params("parallel", "arbitrary"),
        name="even_out",
    )(o, proj, yb, og, mp, ms, w_out, x)


def _odd_in_kernel(x_ref, g_ref, mpsh, mpsc, mssh, mssc, wa_ref, wb_ref, ba_ref, bb_ref, u_ref, hb_ref, *, ntp, reps):
    i = pl.program_id(0)

    @pl.when(pl.program_id(1) == 0)
    def _():
        is_s = i >= ntp
        h = _rms(x_ref[...]) * g_ref[...] * (1 + _tile_mod(mpsc, mssc, is_s, reps)) + _tile_mod(mpsh, mssh, is_s, reps)
        hb_ref[...] = h.astype(BF16)

    hb = hb_ref[...]
    a = jnp.dot(hb, wa_ref[0].astype(BF16), preferred_element_type=F32) + ba_ref[0]
    b = jnp.dot(hb, wb_ref[0].astype(BF16), preferred_element_type=F32) + bb_ref[0]
    u_ref[...] = a * jax.nn.sigmoid(b)


def _odd_in(x, g, mp, ms, w_in, b_in, li, dims):
    t, d = x.shape
    tm, ntp, tps, bp, bs, reps = dims
    dc = w_in.shape[2] // 2
    tn = min(512, dc)
    nj = dc // tn
    sh = _mod_specs(0, tps, bp, bs, d, 2)
    sc = _mod_specs(1, tps, bp, bs, d, 2)
    return pl.pallas_call(
        functools.partial(_odd_in_kernel, ntp=ntp, reps=reps),
        out_shape=jax.ShapeDtypeStruct((t, dc), F32),
        grid=(t // tm, nj),
        in_specs=[
            pl.BlockSpec((tm, d), lambda i, j: (i, 0)),
            pl.BlockSpec((1, d), lambda i, j: (0, 0)),
            sh[0], sc[0], sh[1], sc[1],
            pl.BlockSpec((1, d, tn), lambda i, j: (li, 0, j)),
            pl.BlockSpec((1, d, tn), lambda i, j: (li, 0, nj + j)),
            pl.BlockSpec((1, 1, tn), lambda i, j: (li, 0, j)),
            pl.BlockSpec((1, 1, tn), lambda i, j: (li, 0, nj + j)),
        ],
        out_specs=pl.BlockSpec((tm, tn), lambda i, j: (i, j)),
        scratch_shapes=[pltpu.VMEM((tm, d), BF16)],
        compiler_params=_params("parallel", "arbitrary"),
        name="odd_in",
    )(x, g, mp, mp, ms, ms, w_in, w_in, b_in, b_in)


def _dwconv_kernel(u_ref, halo_ref, w_ref, b_ref, z_ref, ext_ref, *, stride, hr, tps, zero_start):
    tm = u_ref.shape[0]
    keep = jnp.logical_not(jnp.logical_and(zero_start, pl.program_id(0) % tps == 0)).astype(F32)
    ext_ref[pl.ds(0, hr), :] = halo_ref[...] * keep
    ext_ref[pl.ds(hr, tm), :] = u_ref[...]
    w = w_ref[0]
    b = b_ref[0]
    rb = min(32, tm)

    def chunk(ci, carry):
        r0 = pl.multiple_of(ci * rb, rb)
        z_ref[pl.ds(r0, rb), :] = _causal_taps(ext_ref, r0, rb, slice(None), w, CONV_C, stride, hr) + b
        return carry

    lax.fori_loop(0, tm // rb, chunk, 0)


def _dwconv(u, halo, dw_w, dw_b, li, row0, ntiles, tm, stride, hr, tps, zero_start):
    dc = u.shape[1]
    cb = min(256, dc)
    if halo is None:
        hb = tm // hr
        harg = u
        hspec = pl.BlockSpec((hr, cb), lambda i, c: (jnp.maximum((row0 + i) * hb - 1, 0), c))
    else:
        harg = halo
        hspec = pl.BlockSpec((hr, cb), lambda i, c: (0, c))
    return pl.pallas_call(
        functools.partial(_dwconv_kernel, stride=stride, hr=hr, tps=tps, zero_start=zero_start),
        out_shape=jax.ShapeDtypeStruct((ntiles * tm, dc), F32),
        grid=(ntiles, dc // cb),
        in_specs=[
            pl.BlockSpec((tm, cb), lambda i, c: (row0 + i, c)),
            hspec,
            pl.BlockSpec((1, CONV_C, cb), lambda i, c: (li, 0, c)),
            pl.BlockSpec((1, 1, cb), lambda i, c: (li, 0, c)),
        ],
        out_specs=pl.BlockSpec((tm, cb), lambda i, c: (i, c)),
        scratch_shapes=[pltpu.VMEM((hr + tm, cb), F32)],
        compiler_params=_params("parallel", "parallel"),
        name="dwconv",
    )(u, harg, dw_w, dw_b)


def _odd_out_kernel(z_ref, lg_ref, lb_ref, mpg, msg, w_ref, b_ref, x_ref, xo_ref, zs_ref, *, ntp, reps):
    i = pl.program_id(0)

    @pl.when(pl.program_id(1) == 0)
    def _():
        z = z_ref[...]
        zc = z - jnp.mean(z, axis=-1, keepdims=True)
        y = zc * lax.rsqrt(jnp.mean(zc * zc, axis=-1, keepdims=True) + EPS)
        zs_ref[...] = _silu(y * lg_ref[0] + lb_ref[0]).astype(BF16)

    y = jnp.dot(zs_ref[...], w_ref[0].astype(BF16), preferred_element_type=F32) + b_ref[0]
    xo_ref[...] = x_ref[...] + _tile_mod(mpg, msg, i >= ntp, reps) * y


def _odd_out(z, ln_g, ln_b, mp, ms, w_out, b_out, li, x, dims):
    t, d = x.shape
    tm, ntp, tps, bp, bs, reps = dims
    dc = z.shape[1]
    tn = min(1024, d)
    nj = d // tn
    mpg = pl.BlockSpec((1, 1, 1, tn), lambda i, j: (2, jnp.minimum(i // tps, bp - 1), 0, j))
    msg = pl.BlockSpec((1, bs, tn), lambda i, j: (2, 0, j))
    return pl.pallas_call(
        functools.partial(_odd_out_kernel, ntp=ntp, reps=reps),
        out_shape=jax.ShapeDtypeStruct((t, d), F32),
        grid=(t // tm, nj),
        in_specs=[
            pl.BlockSpec((tm, dc), lambda i, j: (i, 0)),
            pl.BlockSpec((1, 1, dc), lambda i, j: (li, 0, 0)),
            pl.BlockSpec((1, 1, dc), lambda i, j: (li, 0, 0)),
            mpg, msg,
            pl.BlockSpec((1, dc, tn), lambda i, j: (li, 0, j)),
            pl.BlockSpec((1, 1, tn), lambda i, j: (li, 0, j)),
            pl.BlockSpec((tm, tn), lambda i, j: (i, j)),
        ],
        out_specs=pl.BlockSpec((tm, tn), lambda i, j: (i, j)),
        scratch_shapes=[pltpu.VMEM((tm, dc), BF16)],
        compiler_params=_params("parallel", "arbitrary"),
        name="odd_out",
    )(z, ln_g, ln_b, mp, ms, w_out, b_out, x)


def _ffnpre_kernel(x_ref, g_ref, mpsh, mpsc, mssh, mssc, wr_ref, br_ref,
                   h_ref, idx_ref, gate_ref, rank_ref, cnt_ref, carry_ref, *, ntp, reps):
    i = pl.program_id(0)
    tm = x_ref.shape[0]
    is_s = i >= ntp
    h = _rms(x_ref[...]) * g_ref[0] * (1 + _tile_mod(mpsc, mssc, is_s, reps)) + _tile_mod(mpsh, mssh, is_s, reps)
    h_ref[...] = h
    logits = _mm_hi(h, wr_ref[0]) + br_ref[0]

    @pl.when(i == 0)
    def _():
        carry_ref[...] = jnp.zeros(carry_ref.shape, F32)

    lane = lax.broadcasted_iota(I32, logits.shape, 1)
    work = logits
    sels, tops, picks = [], [], []
    for _ in range(TOP_K):
        m = jnp.max(work, axis=-1, keepdims=True)
        pick = jnp.min(jnp.where(work == m, lane, V7X_LANES), axis=-1, keepdims=True)
        sel = lane == pick
        work = jnp.where(sel, -jnp.inf, work)
        sels.append(sel)
        tops.append(m)
        picks.append(pick)
    exps = [jnp.exp(m - tops[0]) for m in tops]
    denom = exps[0]
    for e in exps[1:]:
        denom = denom + e

    onehot = sels[0]
    for s in sels[1:]:
        onehot = jnp.logical_or(onehot, s)
    onehot = onehot.astype(F32)
    r = lax.broadcasted_iota(I32, (tm, tm), 0)
    c = lax.broadcasted_iota(I32, (tm, tm), 1)
    before = _mm((r > c).astype(F32), onehot) + carry_ref[...]
    carry_ref[...] = carry_ref[...] + jnp.sum(onehot, axis=0, keepdims=True)

    idx_out = jnp.zeros(logits.shape, I32)
    gate_out = jnp.zeros(logits.shape, F32)
    rank_out = jnp.zeros(logits.shape, I32)
    for kk in range(TOP_K):
        rank = jnp.sum(jnp.where(sels[kk], before, 0.0), axis=-1, keepdims=True).astype(I32)
        idx_out = jnp.where(lane == kk, picks[kk], idx_out)
        gate_out = jnp.where(lane == kk, exps[kk] / denom, gate_out)
        rank_out = jnp.where(lane == kk, rank, rank_out)
    idx_ref[...] = idx_out
    gate_ref[...] = gate_out
    rank_ref[...] = rank_out
    cnt_ref[...] = carry_ref[...].astype(I32)


def _ffnpre(x, g, mp, ms, w_router, b_router, li, dims):
    t, d = x.shape
    tm, ntp, tps, bp, bs, reps = dims
    sh = _mod_specs(3, tps, bp, bs, d, 1)
    sc = _mod_specs(4, tps, bp, bs, d, 1)
    lanes = jax.ShapeDtypeStruct((t, V7X_LANES), I32)
    rowspec = pl.BlockSpec((tm, V7X_LANES), lambda i: (i, 0))
    return pl.pallas_call(
        functools.partial(_ffnpre_kernel, ntp=ntp, reps=reps),
        out_shape=(jax.ShapeDtypeStruct((t, d), F32), lanes, jax.ShapeDtypeStruct((t, V7X_LANES), F32), lanes,
                   jax.ShapeDtypeStruct((1, V7X_LANES), I32)),
        grid=(t // tm,),
        in_specs=[
            pl.BlockSpec((tm, d), lambda i: (i, 0)),
            pl.BlockSpec((1, 1, d), lambda i: (li, 0, 0)),
            sh[0], sc[0], sh[1], sc[1],
            pl.BlockSpec((1, d, V7X_LANES), lambda i: (li, 0, 0)),
            pl.BlockSpec((1, 1, V7X_LANES), lambda i: (li, 0, 0)),
        ],
        out_specs=(pl.BlockSpec((tm, d), lambda i: (i, 0)), rowspec, rowspec, rowspec,
                   pl.BlockSpec((1, V7X_LANES), lambda i: (0, 0))),
        scratch_shapes=[pltpu.VMEM((1, V7X_LANES), F32)],
        compiler_params=_params("arbitrary"),
        name="ffn_pre_router",
    )(x, g, mp, mp, ms, ms, w_router, b_router)


def _dispatch_kernel(slot_ref, zrow_ref, h_ref, xs_ref, zero_ref, sem, *, ntok, nexp, bm):
    zero_ref[...] = jnp.zeros(zero_ref.shape, F32)

    def zcopy(r):
        return pltpu.make_async_copy(zero_ref, xs_ref.at[pl.ds(r, 1)], sem)

    def zfill(e, carry):
        base = zrow_ref[e]

        def issue(j, cc):
            zcopy(base + j).start()
            return cc

        lax.fori_loop(0, bm, issue, 0)

        def drain(j, cc):
            zcopy(base + j).wait()
            return cc

        lax.fori_loop(0, bm, drain, 0)
        return carry

    lax.fori_loop(0, nexp, zfill, 0)

    def rcopy(t, kk):
        return pltpu.make_async_copy(h_ref.at[pl.ds(t, 1)], xs_ref.at[pl.ds(slot_ref[t * TOP_K + kk], 1)], sem)

    def body(t, carry):
        for kk in range(TOP_K):
            rcopy(t, kk).start()

        @pl.when(t >= GATHER_LAG)
        def _():
            for kk in range(TOP_K):
                rcopy(t - GATHER_LAG, kk).wait()

        return carry

    lax.fori_loop(0, ntok, body, 0)

    def tail(t, carry):
        for kk in range(TOP_K):
            rcopy(t, kk).wait()
        return carry

    lax.fori_loop(ntok - GATHER_LAG, ntok, tail, 0)


def _dispatch(slots, zrows, h, rows, bm):
    t, d = h.shape
    nexp = zrows.shape[0]
    return pl.pallas_call(
        functools.partial(_dispatch_kernel, ntok=t, nexp=nexp, bm=bm),
        out_shape=jax.ShapeDtypeStruct((rows, d), F32),
        grid_spec=pltpu.PrefetchScalarGridSpec(
            num_scalar_prefetch=2,
            grid=(1,),
            in_specs=[pl.BlockSpec(memory_space=pl.ANY)],
            out_specs=pl.BlockSpec(memory_space=pl.ANY),
            scratch_shapes=[pltpu.VMEM((1, d), F32), pltpu.SemaphoreType.DMA(())],
        ),
        compiler_params=_params("arbitrary"),
        name="moe_dispatch",
    )(slots, zrows, h)


def _expert_up_kernel(be_ref, nv_ref, x_ref, wg_ref, wu_ref, bg_ref, bu_ref, act_ref, wgb_ref, wub_ref):
    i = pl.program_id(1)
    fresh = jnp.logical_or(i == 0, be_ref[i] != be_ref[jnp.maximum(i - 1, 0)])

    @pl.when(jnp.logical_and(fresh, i < nv_ref[0]))
    def _():
        wgb_ref[...] = wg_ref[0, 0].astype(BF16)
        wub_ref[...] = wu_ref[0, 0].astype(BF16)

    @pl.when(i < nv_ref[0])
    def _():
        xb = x_ref[...].astype(BF16)
        g = jnp.dot(xb, wgb_ref[...], preferred_element_type=F32) + bg_ref[0, 0]
        u = jnp.dot(xb, wub_ref[...], preferred_element_type=F32) + bu_ref[0, 0]
        gate = jnp.minimum(g, SWIGLU_LIMIT)
        up = jnp.clip(u, -SWIGLU_LIMIT, SWIGLU_LIMIT)
        act_ref[...] = ((up + 1) * gate * jax.nn.sigmoid(SWIGLU_ALPHA * gate)).astype(BF16)


def _expert_up(blk_expert, nvalid, xs, w_gate_up, b_gate_up, li, bm):
    rows, d = xs.shape
    nblk = rows // bm
    f = w_gate_up.shape[3] // 2
    tn = min(1024, f)
    nj = f // tn
    clamp = lambda i, nv: jnp.minimum(i, nv[0] - 1)
    return pl.pallas_call(
        _expert_up_kernel,
        out_shape=jax.ShapeDtypeStruct((rows, f), BF16),
        grid_spec=pltpu.PrefetchScalarGridSpec(
            num_scalar_prefetch=2,
            grid=(nj, nblk),
            in_specs=[
                pl.BlockSpec((bm, d), lambda j, i, be, nv: (clamp(i, nv), 0)),
                pl.BlockSpec((1, 1, d, tn), lambda j, i, be, nv: (li, be[clamp(i, nv)], 0, j)),
                pl.BlockSpec((1, 1, d, tn), lambda j, i, be, nv: (li, be[clamp(i, nv)], 0, nj + j)),
                pl.BlockSpec((1, 1, 1, tn), lambda j, i, be, nv: (li, be[clamp(i, nv)], 0, j)),
                pl.BlockSpec((1, 1, 1, tn), lambda j, i, be, nv: (li, be[clamp(i, nv)], 0, nj + j)),
            ],
            out_specs=pl.BlockSpec((bm, tn), lambda j, i, be, nv: (clamp(i, nv), j)),
            scratch_shapes=[pltpu.VMEM((d, tn), BF16), pltpu.VMEM((d, tn), BF16)],
        ),
        compiler_params=_params("arbitrary", "arbitrary"),
        name="expert_gate_up",
    )(blk_expert, nvalid, xs, w_gate_up, w_gate_up, b_gate_up, b_gate_up)


def _expert_down_kernel(be_ref, nv_ref, a_ref, w_ref, b_ref, y_ref, wb_ref):
    i = pl.program_id(1)
    fresh = jnp.logical_or(i == 0, be_ref[i] != be_ref[jnp.maximum(i - 1, 0)])

    @pl.when(jnp.logical_and(fresh, i < nv_ref[0]))
    def _():
        wb_ref[...] = w_ref[0, 0].astype(BF16)

    @pl.when(i < nv_ref[0])
    def _():
        y_ref[...] = jnp.dot(a_ref[...], wb_ref[...], preferred_element_type=F32) + b_ref[0, 0]


def _expert_down(blk_expert, nvalid, act, w_down, b_down, li, bm):
    rows, f = act.shape
    nblk = rows // bm
    d = w_down.shape[3]
    tn = min(1024, d)
    nj = d // tn
    clamp = lambda i, nv: jnp.minimum(i, nv[0] - 1)
    return pl.pallas_call(
        _expert_down_kernel,
        out_shape=jax.ShapeDtypeStruct((rows, d), F32),
        grid_spec=pltpu.PrefetchScalarGridSpec(
            num_scalar_prefetch=2,
            grid=(nj, nblk),
            in_specs=[
                pl.BlockSpec((bm, f), lambda j, i, be, nv: (clamp(i, nv), 0)),
                pl.BlockSpec((1, 1, f, tn), lambda j, i, be, nv: (li, be[clamp(i, nv)], 0, j)),
                pl.BlockSpec((1, 1, 1, tn), lambda j, i, be, nv: (li, be[clamp(i, nv)], 0, j)),
            ],
            out_specs=pl.BlockSpec((bm, tn), lambda j, i, be, nv: (clamp(i, nv), j)),
            scratch_shapes=[pltpu.VMEM((f, tn), BF16)],
        ),
        compiler_params=_params("arbitrary", "arbitrary"),
        name="expert_down",
    )(blk_expert, nvalid, act, w_down, b_down)


def _combine_kernel(slot_ref, gate_ref, mpg, msg, x_ref, ys_ref, xo_ref, buf_ref, sem, *, ntp, reps):
    i = pl.program_id(0)
    tm = x_ref.shape[0]

    def rcopy(t, kk):
        return pltpu.make_async_copy(ys_ref.at[pl.ds(slot_ref[(i * tm + t) * TOP_K + kk], 1)],
                                     buf_ref.at[kk, pl.ds(t, 1)], sem)

    def issue(t, carry):
        for kk in range(TOP_K):
            rcopy(t, kk).start()
        return carry

    lax.fori_loop(0, tm, issue, 0)

    def drain(t, carry):
        for kk in range(TOP_K):
            rcopy(t, kk).wait()
        return carry

    lax.fori_loop(0, tm, drain, 0)

    gates = gate_ref[...]
    y = buf_ref[0] * gates[:, 0:1]
    for kk in range(1, TOP_K):
        y = y + buf_ref[kk] * gates[:, kk:kk + 1]
    xo_ref[...] = x_ref[...] + _tile_mod(mpg, msg, i >= ntp, reps) * y


def _combine(slots, gates, mp, ms, x, ys, dims):
    t, d = x.shape
    tm, ntp, tps, bp, bs, reps = dims
    mpg = pl.BlockSpec((1, 1, 1, d), lambda i, sl: (5, jnp.minimum(i // tps, bp - 1), 0, 0))
    msg = pl.BlockSpec((1, bs, d), lambda i, sl: (5, 0, 0))
    return pl.pallas_call(
        functools.partial(_combine_kernel, ntp=ntp, reps=reps),
        out_shape=jax.ShapeDtypeStruct((t, d), F32),
        grid_spec=pltpu.PrefetchScalarGridSpec(
            num_scalar_prefetch=1,
            grid=(t // tm,),
            in_specs=[
                pl.BlockSpec((tm, V7X_LANES), lambda i, sl: (i, 0)),
                mpg, msg,
                pl.BlockSpec((tm, d), lambda i, sl: (i, 0)),
                pl.BlockSpec(memory_space=pl.ANY),
            ],
            out_specs=pl.BlockSpec((tm, d), lambda i, sl: (i, 0)),
            scratch_shapes=[pltpu.VMEM((TOP_K, tm, d), F32), pltpu.SemaphoreType.DMA(())],
        ),
        compiler_params=_params("arbitrary"),
        name="moe_combine",
    )(slots, gates, mp, ms, x, ys)


def _moe(x, li, g, mp, ms, w_router_p, b_router_p, w_gate_up, b_gate_up4, w_down, b_down4, dims):
    t, d = x.shape
    nexp = w_gate_up.shape[1]
    bm = MOE_ROWS
    h, idx, gates, rank, cnt = _ffnpre(x, g, mp, ms, w_router_p, b_router_p, li, dims)
    counts = cnt[0, :nexp]
    padded = (counts + bm - 1) // bm * bm
    pad_end = jnp.cumsum(padded)
    pad_start = pad_end - padded
    idx4 = idx[:, :TOP_K]
    start4 = jnp.sum(jnp.where(idx4[..., None] == jnp.arange(nexp, dtype=I32), pad_start, 0), axis=-1)
    slots = (start4 + rank[:, :TOP_K]).reshape(t * TOP_K).astype(I32)
    nblk = -(-(t * TOP_K + nexp * (bm - 1)) // bm) + 1
    rows = nblk * bm
    blk_start = jnp.arange(nblk, dtype=I32) * bm
    blk_expert = jnp.minimum(jnp.sum(pad_end[None, :] <= blk_start[:, None], axis=1), nexp - 1).astype(I32)
    nvalid = (pad_end[-1:] // bm).astype(I32)
    zrows = (pad_start + counts).astype(I32)
    xs = _dispatch(slots, zrows, h, rows, bm)
    act = _expert_up(blk_expert, nvalid, xs, w_gate_up, b_gate_up4, li, bm)
    ys = _expert_down(blk_expert, nvalid, act, w_down, b_down4, li, bm)
    return _combine(slots, gates, mp, ms, x, ys, dims)


def _final_kernel(x_ref, g_ref, o_ref):
    o_ref[...] = _rms(x_ref[...]) * g_ref[...]


def _final_norm(x, g, tm):
    t, d = x.shape
    return pl.pallas_call(
        _final_kernel,
        out_shape=jax.ShapeDtypeStruct((t, d), F32),
        grid=(t // tm,),
        in_specs=[pl.BlockSpec((tm, d), lambda i: (i, 0)), pl.BlockSpec((1, d), lambda i: (0, 0))],
        out_specs=pl.BlockSpec((tm, d), lambda i: (i, 0)),
        compiler_params=_params("parallel"),
        name="final_norm",
    )(x, g)


def _to_pos_major(a):
    b, l, c = a.shape
    return jnp.swapaxes(a, 0, 1).reshape(l * b, c)


def _from_pos_major(a, b):
    lb, c = a.shape
    return jnp.swapaxes(a.reshape(lb // b, b, c), 0, 1)


def kernel(x_prompt, x_sample, state_delta, state_conv_qkv, state_conv_b, state_conv_c, c_prompt, c_sample, norm_mix, norm_ffn, w_mod, b_mod, w_in_even, conv_qkv_w, a_log, dt_bias, o_norm_g, conv_b_w, w_out_even, w_in_odd, b_in_odd, dw_w, dw_b, ln_g, ln_b, w_out_odd, b_out_odd, w_router, b_router, w_gate_up, b_gate_up, w_down, b_down, norm_final):
    bp, seq, d = x_prompt.shape
    bs, dseq, _ = x_sample.shape
    depth = w_mod.shape[0]
    nh = a_log.shape[1]
    da = nh * DV
    db = conv_b_w.shape[2]
    dc = dw_w.shape[2]
    nexp = w_router.shape[2]
    tp, ts = bp * seq, bs * dseq
    tm = ts
    assert seq % tm == 0 and bs % V7X_SUBLANES == 0 and da == db and dseq >= CONV_A - 1
    assert nexp <= V7X_LANES and 2 * nh <= V7X_LANES
    tps = seq // tm
    ntp = tp // tm
    dims = (tm, ntp, tps, bp, bs, dseq)
    chunk = math.gcd(seq, CHUNK_A)

    x = jnp.concatenate([x_prompt.reshape(tp, d), _to_pos_major(x_sample)], axis=0)
    rc = -(-(bp + bs) // V7X_SUBLANES) * V7X_SUBLANES
    c_all = jnp.zeros((rc, d), F32).at[:bp].set(c_prompt).at[bp:bp + bs].set(c_sample)
    mod = _modulation(c_all, w_mod, b_mod)

    w_router_p = jnp.zeros((depth, d, V7X_LANES), F32).at[:, :, :nexp].set(w_router)
    b_router_p = jnp.full((depth, 1, V7X_LANES), -jnp.inf, F32).at[:, 0, :nexp].set(b_router)
    b_gate_up4 = b_gate_up.reshape(depth, nexp, 1, b_gate_up.shape[2])
    b_down4 = b_down.reshape(depth, nexp, 1, d)

    qkv_w = 3 * nh * DK
    c_ab = qkv_w + da
    c_b = c_ab + 2 * nh
    hr_qkv = (CONV_A - 1) * bs
    hr_c = (CONV_C - 1) * bs
    outs = {k: [] for k in ("dp", "qp", "bp", "cp", "ds", "qs", "bs", "cs")}

    for l in range(depth):
        mp = mod[l][:, :bp].reshape(6, bp, 1, d)
        ms = mod[l][:, bp:bp + bs]
        li = l // 2
        if l % 2 == 0:
            w = w_in_even[li]
            w_main = jnp.concatenate([w[:, :c_ab], w[:, c_b:]], axis=1)
            w_ab = jnp.zeros((d, V7X_LANES), F32).at[:, :2 * nh].set(w[:, c_ab:c_b])
            proj, ab = _even_in(x, norm_mix[l:l + 1], mp, ms, w_main, w_ab, dims)
            adt = jnp.zeros((V7X_SUBLANES, V7X_LANES), F32).at[0, :nh].set(a_log[li]).at[1, :nh].set(dt_bias[li])
            wqkv, wb = conv_qkv_w[li], conv_b_w[li]
            qp, kp, vp, gbp, ybp, utp = _evenprep(proj, ab, None, wqkv, wb, adt, 0, ntp, tm, 1, V7X_SUBLANES,
                                                  tps, True, nh)
            sq = _to_pos_major(state_conv_qkv[li])
            su = jnp.concatenate([jnp.zeros(((CONV_A - CONV_B) * bs, db), F32), _to_pos_major(state_conv_b[li])], axis=0)
            halos = (sq[:, :da], sq[:, da:2 * da], sq[:, 2 * da:], su, jnp.ones_like(su))
            qs, ks, vs, gbs, ybs, uts = _evenprep(proj, ab, halos, wqkv, wb, adt, ntp, 1, tm, bs, hr_qkv,
                                                  1, False, nh)
            o_p, s_p = _delta(qp, kp, vp, gbp, None, li, bp, seq // chunk, chunk, nh)

            def seq_major(a):
                a = _from_pos_major(a, bs)
                return jnp.pad(a, ((0, 0), (0, V7X_SUBLANES - dseq), (0, 0))).reshape(bs * V7X_SUBLANES, a.shape[2])

            o_s8, s_s = _delta(seq_major(qs), seq_major(ks), seq_major(vs), seq_major(gbs), state_delta, li,
                               bs, 1, V7X_SUBLANES, nh)
            o_s = _to_pos_major(o_s8.reshape(bs, V7X_SUBLANES, da)[:, :dseq])
            o = jnp.concatenate([o_p, o_s], axis=0)
            yb = jnp.concatenate([ybp, ybs], axis=0)
            x = _even_out(o, proj, yb, o_norm_g[li:li + 1], mp, ms, w_out_even, li, x, dims, nh)

            qkv_raw = proj[:, :qkv_w]
            outs["dp"].append(s_p)
            outs["qp"].append(qkv_raw[:tp].reshape(bp, seq, qkv_w)[:, seq - (CONV_A - 1):])
            outs["bp"].append(utp.reshape(bp, tps, V7X_SUBLANES, db)[:, tps - 1, V7X_SUBLANES - (CONV_B - 1):])
            outs["ds"].append(s_s)
            outs["qs"].append(_from_pos_major(qkv_raw[tp:], bs)[:, dseq - (CONV_A - 1):])
            u_ext = jnp.concatenate([state_conv_b[li], _from_pos_major(uts, bs)], axis=1)
            outs["bs"].append(u_ext[:, u_ext.shape[1] - (CONV_B - 1):])
        else:
            u = _odd_in(x, norm_mix[l:l + 1], mp, ms, w_in_odd, b_in_odd.reshape(-1, 1, 2 * dc), li, dims)
            dwb = dw_b.reshape(-1, 1, dc)
            zp = _dwconv(u, None, dw_w, dwb, li, 0, ntp, tm, 1, 32, tps, True)
            zs = _dwconv(u, _to_pos_major(state_conv_c[li]), dw_w, dwb, li, ntp, 1, tm, bs, hr_c, 1, False)
            z = jnp.concatenate([zp, zs], axis=0)
            x = _odd_out(z, ln_g.reshape(-1, 1, dc), ln_b.reshape(-1, 1, dc), mp, ms, w_out_odd,
                         b_out_odd.reshape(-1, 1, d), li, x, dims)
            outs["cp"].append(u[:tp].reshape(bp, seq, dc)[:, seq - (CONV_C - 1):])
            c_ext = jnp.concatenate([state_conv_c[li], _from_pos_major(u[tp:], bs)], axis=1)
            outs["cs"].append(c_ext[:, c_ext.shape[1] - (CONV_C - 1):])
        x = _moe(x, l, norm_ffn.reshape(depth, 1, d), mp, ms, w_router_p, b_router_p, w_gate_up, b_gate_up4,
                 w_down, b_down4, dims)

    y = _final_norm(x, norm_final.reshape(1, d), tm)
    y_prompt = y[:tp].reshape(bp, seq, d)
    y_sample = _from_pos_major(y[tp:], bs)
    return (y_prompt, y_sample,
            jnp.stack(outs["dp"]), jnp.stack(outs["qp"]), jnp.stack(outs["bp"]), jnp.stack(outs["cp"]),
            jnp.stack(outs["ds"]), jnp.stack(outs["qs"]), jnp.stack(outs["bs"]), jnp.stack(outs["cs"]))
```

```python
import functools
import math

import jax
import jax.numpy as jnp
from jax import lax
from jax.experimental import pallas as pl
from jax.experimental.pallas import tpu as pltpu

F32 = jnp.float32
BF16 = jnp.bfloat16
I32 = jnp.int32
EPS = 1e-6
HI = lax.Precision.HIGHEST

DK = 128
DV = 128
CONV_A = 4
CONV_B = 3
CONV_C = 31
CHUNK_A = 64
TOP_K = 4
SWIGLU_ALPHA = 1.702
SWIGLU_LIMIT = 7.0

V7X_LANES = 128
V7X_SUBLANES = 8
V7X_VMEM_LIMIT_BYTES = 56 * 1024 * 1024
MOE_ROWS = 256
EXPERT_COLS = 1024
ROW_UNROLL = 8


def _params(*sem):
    return pltpu.CompilerParams(dimension_semantics=sem, vmem_limit_bytes=V7X_VMEM_LIMIT_BYTES)


def _mm(a, b):
    return jnp.dot(a.astype(BF16), b.astype(BF16), preferred_element_type=F32)


def _mm_hi(a, b):
    return jnp.dot(a, b, precision=HI, preferred_element_type=F32)


_NN = (((1,), (0,)), ((), ()))
_NT = (((1,), (1,)), ((), ()))
_TN = (((0,), (0,)), ((), ()))


def _dot(a, b, dims):
    return lax.dot_general(a, b, dims, preferred_element_type=F32)


def _split2(a):
    hi = a.astype(BF16)
    return hi, (a - hi.astype(F32)).astype(BF16)


def _mm3s(a, b):
    return _dot(a[0], b[0], _NN) + (_dot(a[0], b[1], _NN) + _dot(a[1], b[0], _NN))


def _silu(x):
    return x * jax.nn.sigmoid(x)


def _rms(x):
    return x * lax.rsqrt(jnp.mean(x * x, axis=-1, keepdims=True) + EPS)


def _tile_mod(mp_ref, ms_ref, is_sample, reps):
    s = ms_ref[0]
    s = jnp.concatenate([s] * reps, axis=0)
    return jnp.where(is_sample, s, mp_ref[0, 0])


def _mod_specs(k, tps, bp, bs, d, ngrid):
    if ngrid == 1:
        mp = pl.BlockSpec((1, 1, 1, d), lambda i: (k, jnp.minimum(i // tps, bp - 1), 0, 0))
        ms = pl.BlockSpec((1, bs, d), lambda i: (k, 0, 0))
    else:
        mp = pl.BlockSpec((1, 1, 1, d), lambda i, j: (k, jnp.minimum(i // tps, bp - 1), 0, 0))
        ms = pl.BlockSpec((1, bs, d), lambda i, j: (k, 0, 0))
    return mp, ms


def _mod_kernel(c_ref, w_ref, b_ref, o_ref):
    o_ref[0, 0] = _mm(_silu(c_ref[...]), w_ref[0]) + b_ref[0]


def _modulation(c_all, w_mod, b_mod):
    depth, d, d6 = w_mod.shape
    rc = c_all.shape[0]
    tn = min(512, d)
    nj = d // tn
    return pl.pallas_call(
        _mod_kernel,
        out_shape=jax.ShapeDtypeStruct((depth, 6, rc, d), F32),
        grid=(depth, 6, nj),
        in_specs=[
            pl.BlockSpec((rc, d), lambda l, k, j: (0, 0)),
            pl.BlockSpec((1, d, tn), lambda l, k, j: (l, 0, k * nj + j)),
            pl.BlockSpec((1, 1, tn), lambda l, k, j: (l, 0, k * nj + j)),
        ],
        out_specs=pl.BlockSpec((1, 1, rc, tn), lambda l, k, j: (l, k, 0, j)),
        compiler_params=_params("parallel", "parallel", "parallel"),
        name="modulation",
    )(c_all, w_mod, b_mod.reshape(depth, 1, d6))


def _even_in_kernel(x_ref, g_ref, mpsh, mpsc, mssh, mssc, w_ref, wab_ref, proj_ref, ab_ref, hb_ref, *, ntp, reps):
    i = pl.program_id(0)

    @pl.when(pl.program_id(1) == 0)
    def _():
        is_s = i >= ntp
        h = _rms(x_ref[...]) * g_ref[...] * (1 + _tile_mod(mpsc, mssc, is_s, reps)) + _tile_mod(mpsh, mssh, is_s, reps)
        hb = h.astype(BF16)
        hb_ref[...] = hb
        ab_ref[...] = jnp.dot(hb, wab_ref[...].astype(BF16), preferred_element_type=F32)

    proj_ref[...] = jnp.dot(hb_ref[...], w_ref[...].astype(BF16), preferred_element_type=F32)


def _even_in(x, g, mp, ms, w_main, w_ab, dims):
    t, d = x.shape
    tm, ntp, tps, bp, bs, reps = dims
    nmain = w_main.shape[1]
    tn = nmain // 7
    nj = 7
    sh = _mod_specs(0, tps, bp, bs, d, 2)
    sc = _mod_specs(1, tps, bp, bs, d, 2)
    return pl.pallas_call(
        functools.partial(_even_in_kernel, ntp=ntp, reps=reps),
        out_shape=(jax.ShapeDtypeStruct((t, nmain), F32), jax.ShapeDtypeStruct((t, V7X_LANES), F32)),
        grid=(t // tm, nj),
        in_specs=[
            pl.BlockSpec((tm, d), lambda i, j: (i, 0)),
            pl.BlockSpec((1, d), lambda i, j: (0, 0)),
            sh[0], sc[0], sh[1], sc[1],
            pl.BlockSpec((d, tn), lambda i, j: (0, j)),
            pl.BlockSpec((d, V7X_LANES), lambda i, j: (0, 0)),
        ],
        out_specs=(pl.BlockSpec((tm, tn), lambda i, j: (i, j)), pl.BlockSpec((tm, V7X_LANES), lambda i, j: (i, 0))),
        scratch_shapes=[pltpu.VMEM((tm, d), BF16)],
        compiler_params=_params("parallel", "arbitrary"),
        name="even_in",
    )(x, g, mp, mp, ms, ms, w_main, w_ab)


def _causal_taps(ext_ref, r0, rb, lanes, w, width, stride, hr):
    acc = None
    if stride % V7X_SUBLANES == 0:
        for j in range(width):
            start = pl.multiple_of(r0 + (hr - (width - 1 - j) * stride), V7X_SUBLANES)
            term = ext_ref[pl.ds(start, rb), lanes] * w[j:j + 1, :]
            acc = term if acc is None else acc + term
        return acc
    look = -(-(width - 1) * stride // V7X_SUBLANES) * V7X_SUBLANES
    nrow = rb + look
    win = ext_ref[pl.ds(pl.multiple_of(r0 + (hr - look), V7X_SUBLANES), nrow), lanes]
    for res in range(V7X_SUBLANES):
        taps = [j for j in range(width) if (look - (width - 1 - j) * stride) % V7X_SUBLANES == res]
        if not taps:
            continue
        shifted = win if res == 0 else pltpu.roll(win, nrow - res, axis=0)
        for j in taps:
            off = look - (width - 1 - j) * stride - res
            term = shifted[off:off + rb, :] * w[j:j + 1, :]
            acc = term if acc is None else acc + term
    return acc


def _evenprep_kernel(q_ref, k_ref, v_ref, xb_ref, cp_ref, bpost_ref, ab_ref,
                     hq_ref, hk_ref, hv_ref, hxb_ref, hcp_ref,
                     wqkv_ref, wb_ref, adt_ref,
                     qo_ref, ko_ref, vo_ref, gb_ref, yb_ref, ut_ref,
                     extq, extk, extv, extu, *, stride, hr, tps, zero_start, nh):
    tm = q_ref.shape[0]
    da = q_ref.shape[1]
    keep = jnp.logical_not(jnp.logical_and(zero_start, pl.program_id(0) % tps == 0)).astype(F32)
    for ext, halo, cur in ((extq, hq_ref, q_ref), (extk, hk_ref, k_ref), (extv, hv_ref, v_ref)):
        ext[pl.ds(0, hr), :] = halo[...] * keep
        ext[pl.ds(hr, tm), :] = cur[...]
    extu[pl.ds(0, hr), :] = hcp_ref[...] * hxb_ref[...] * keep
    extu[pl.ds(hr, tm), :] = cp_ref[...] * xb_ref[...]
    tr = ut_ref.shape[0]
    ut_ref[...] = extu[pl.ds(hr + tm - tr, tr), :]

    ab = ab_ref[...]
    adt = adt_ref[...]
    z = ab + adt[1:2, :]
    softplus = jnp.maximum(z, 0.0) + jnp.log(1.0 + jnp.exp(-jnp.abs(z)))
    lane = lax.broadcasted_iota(I32, ab.shape, 1)
    gb_ref[...] = jnp.where(lane < nh, -jnp.exp(adt[0:1, :]) * softplus, jax.nn.sigmoid(ab))

    wqkv = wqkv_ref[...]
    wb = wb_ref[...]
    rb = min(64, tm)

    def chunk(ci, carry):
        r0 = pl.multiple_of(ci * rb, rb)
        rows = pl.ds(r0, rb)
        for h in range(nh):
            sl = slice(h * DK, (h + 1) * DK)
            taps = lambda ext, w0: _causal_taps(ext, r0, rb, sl, wqkv[:, w0 + h * DK:w0 + (h + 1) * DK],
                                                CONV_A, stride, hr)
            qh = _silu(taps(extq, 0))
            kh = _silu(taps(extk, da))
            qo_ref[rows, sl] = qh * lax.rsqrt(jnp.sum(qh * qh, axis=-1, keepdims=True) + EPS) * (DK ** -0.5)
            ko_ref[rows, sl] = kh * lax.rsqrt(jnp.sum(kh * kh, axis=-1, keepdims=True) + EPS)
            vo_ref[rows, sl] = _silu(taps(extv, 2 * da))
            yb = bpost_ref[rows, sl] * _causal_taps(extu, r0, rb, sl, wb[:, sl], CONV_B, stride, hr)
            yb_ref[rows, sl] = yb.astype(BF16)
        return carry

    lax.fori_loop(0, tm // rb, chunk, 0)


def _evenprep(proj, ab, halos, wqkv, wb, adt, row0, ntiles, tm, stride, hr, tps, zero_start, nh):
    da = nh * DK
    tr = max(V7X_SUBLANES, (CONV_B - 1) * stride)
    cur = lambda c: pl.BlockSpec((tm, da), lambda i: (row0 + i, c))
    if halos is None:
        hb = tm // hr
        hspec = lambda c: pl.BlockSpec((hr, da), lambda i: (jnp.maximum((row0 + i) * hb - 1, 0), c))
        hargs = [proj] * 5
        hspecs = [hspec(0), hspec(1), hspec(2), hspec(4), hspec(5)]
    else:
        hargs = list(halos)
        hspecs = [pl.BlockSpec((hr, da), lambda i: (0, 0)) for _ in range(5)]
    rows = ntiles * tm
    full = lambda a: pl.BlockSpec(a.shape, lambda i: (0, 0))
    return pl.pallas_call(
        functools.partial(_evenprep_kernel, stride=stride, hr=hr, tps=tps, zero_start=zero_start, nh=nh),
        out_shape=(jax.ShapeDtypeStruct((rows, da), F32), jax.ShapeDtypeStruct((rows, da), F32),
                   jax.ShapeDtypeStruct((rows, da), F32), jax.ShapeDtypeStruct((rows, V7X_LANES), F32),
                   jax.ShapeDtypeStruct((rows, da), BF16), jax.ShapeDtypeStruct((ntiles * tr, da), F32)),
        grid=(ntiles,),
        in_specs=[cur(0), cur(1), cur(2), cur(4), cur(5), cur(6),
                  pl.BlockSpec((tm, V7X_LANES), lambda i: (row0 + i, 0))] + hspecs + [full(wqkv), full(wb), full(adt)],
        out_specs=(pl.BlockSpec((tm, da), lambda i: (i, 0)), pl.BlockSpec((tm, da), lambda i: (i, 0)),
                   pl.BlockSpec((tm, da), lambda i: (i, 0)), pl.BlockSpec((tm, V7X_LANES), lambda i: (i, 0)),
                   pl.BlockSpec((tm, da), lambda i: (i, 0)), pl.BlockSpec((tr, da), lambda i: (i, 0))),
        scratch_shapes=[pltpu.VMEM((hr + tm, da), F32) for _ in range(4)],
        compiler_params=_params("parallel"),
        name="evenprep",
    )(proj, proj, proj, proj, proj, proj, ab, *hargs, wqkv, wb, adt)


def _delta_kernel(*refs, nh, has_s0):
    if has_s0:
        q_ref, k_ref, v_ref, gb_ref, s0_ref, o_ref, sout_ref, s_ref = refs
    else:
        q_ref, k_ref, v_ref, gb_ref, o_ref, sout_ref, s_ref = refs
    n = pl.program_id(1)
    c = q_ref.shape[0]

    @pl.when(n == 0)
    def _():
        s_ref[...] = s0_ref[0, 0] if has_s0 else jnp.zeros(s_ref.shape, F32)

    row = lax.broadcasted_iota(I32, (c, c), 0)
    col = lax.broadcasted_iota(I32, (c, c), 1)
    causal = row >= col
    strict = row > col
    eye = (row == col).astype(F32)

    gb = gb_ref[...]
    tri = causal.astype(BF16)
    g1 = gb.astype(BF16)
    r1 = gb - g1.astype(F32)
    g2 = r1.astype(BF16)
    g3 = (r1 - g2.astype(F32)).astype(BF16)
    gcum = _dot(tri, g1, _NN) + (_dot(tri, g2, _NN) + _dot(tri, g3, _NN))
    gcum_t = gcum.T
    levels = int(math.log2(c))

    heads = range(nh)
    sls = [slice(h * DK, (h + 1) * DK) for h in heads]
    qs = [q_ref[:, sl] for sl in sls]
    ks = [k_ref[:, sl] for sl in sls]
    g_col = [gcum[:, h:h + 1] for h in heads]
    g_last = [gcum[c - 1:c, h:h + 1] for h in heads]
    beta = [gb[:, nh + h:nh + h + 1] for h in heads]
    decay = [jnp.where(causal, jnp.exp(jnp.where(causal, g_col[h] - gcum_t[h:h + 1, :], 0.0)), 0.0) for h in heads]
    kb = [ks[h] * beta[h] for h in heads]
    khb = [ks[h].astype(BF16) for h in heads]
    low = [jnp.where(strict, _dot(kb[h].astype(BF16), khb[h], _NT) * decay[h], 0.0) for h in heads]
    inv = [eye - low[h] for h in heads]
    pw = [_split2(low[h]) for h in heads]
    for _ in range(levels - 1):
        pw = [_split2(_mm3s(pw[h], pw[h])) for h in heads]
        inv = [inv[h] + _mm3s(_split2(inv[h]), pw[h]) for h in heads]
    eg = [jnp.exp(g_col[h]) for h in heads]
    uw = [_mm3s(_split2(inv[h]), _split2(jnp.concatenate([v_ref[:, sls[h]] * beta[h], kb[h] * eg[h]], axis=1)))
          for h in heads]
    intra = [jnp.where(causal, _dot(qs[h].astype(BF16), khb[h], _NT) * decay[h], 0.0) for h in heads]
    s = [s_ref[h] for h in heads]
    sb = [s[h].astype(BF16) for h in heads]
    vnb = [(uw[h][:, :DV] - _dot(uw[h][:, DV:].astype(BF16), sb[h], _NN)).astype(BF16) for h in heads]
    for h in heads:
        o_ref[:, sls[h]] = (_dot((qs[h] * eg[h]).astype(BF16), sb[h], _NN)
                            + _dot(intra[h].astype(BF16), vnb[h], _NN))
    for h in heads:
        k_dec = ks[h] * jnp.exp(g_last[h] - g_col[h])
        s_ref[h] = s[h] * jnp.exp(g_last[h]) + _dot(k_dec.astype(BF16), vnb[h], _TN)

    @pl.when(n == pl.num_programs(1) - 1)
    def _():
        sout_ref[0] = s_ref[...]


def _delta(q, k, v, gb, s0, li, nb, nchunks, c, nh):
    da = nh * DK
    rowspec = lambda w: pl.BlockSpec((c, w), lambda b, n: (b * nchunks + n, 0))
    sspec = pl.BlockSpec((1, nh, DK, DV), lambda b, n: (b, 0, 0, 0))
    in_specs = [rowspec(da), rowspec(da), rowspec(da), rowspec(V7X_LANES)]
    args = [q, k, v, gb]
    if s0 is not None:
        in_specs.append(pl.BlockSpec((1, 1, nh, DK, DV), lambda b, n: (li, b, 0, 0, 0)))
        args.append(s0)
    return pl.pallas_call(
        functools.partial(_delta_kernel, nh=nh, has_s0=s0 is not None),
        out_shape=(jax.ShapeDtypeStruct((nb * nchunks * c, da), F32), jax.ShapeDtypeStruct((nb, nh, DK, DV), F32)),
        grid=(nb, nchunks),
        in_specs=in_specs,
        out_specs=(rowspec(da), sspec),
        scratch_shapes=[pltpu.VMEM((nh, DK, DV), F32)],
        compiler_params=_params("parallel", "arbitrary"),
        name="delta_rule",
    )(*args)


def _even_out_kernel(o_ref, gout_ref, yb_ref, og_ref, mpg, msg, w_ref, x_ref, xo_ref, yin_ref, *, ntp, reps, nh):
    i = pl.program_id(0)
    da = nh * DV

    @pl.when(pl.program_id(1) == 0)
    def _():
        for h in range(nh):
            sl = slice(h * DV, (h + 1) * DV)
            yin_ref[:, sl] = (_rms(o_ref[:, sl]) * og_ref[...] * _silu(gout_ref[:, sl])).astype(BF16)
        yin_ref[:, da:] = yb_ref[...]

    y = jnp.dot(yin_ref[...], w_ref[0].astype(BF16), preferred_element_type=F32)
    xo_ref[...] = x_ref[...] + _tile_mod(mpg, msg, i >= ntp, reps) * y


def _even_out(o, proj, yb, og, mp, ms, w_out, li, x, dims, nh):
    t, d = x.shape
    tm, ntp, tps, bp, bs, reps = dims
    da = nh * DV
    tn = min(1024, d)
    nj = d // tn
    mpg = pl.BlockSpec((1, 1, 1, tn), lambda i, j: (2, jnp.minimum(i // tps, bp - 1), 0, j))
    msg = pl.BlockSpec((1, bs, tn), lambda i, j: (2, 0, j))
    return pl.pallas_call(
        functools.partial(_even_out_kernel, ntp=ntp, reps=reps, nh=nh),
        out_shape=jax.ShapeDtypeStruct((t, d), F32),
        grid=(t // tm, nj),
        in_specs=[
            pl.BlockSpec((tm, da), lambda i, j: (i, 0)),
            pl.BlockSpec((tm, da), lambda i, j: (i, 3)),
            pl.BlockSpec((tm, yb.shape[1]), lambda i, j: (i, 0)),
            pl.BlockSpec((1, DV), lambda i, j: (0, 0)),
            mpg, msg,
            pl.BlockSpec((1, w_out.shape[1], tn), lambda i, j: (li, 0, j)),
            pl.BlockSpec((tm, tn), lambda i, j: (i, j)),
        ],
        out_specs=pl.BlockSpec((tm, tn), lambda i, j: (i, j)),
        scratch_shapes=[pltpu.VMEM((tm, w_out.shape[1]), BF16)],
        compiler_params=_params("parallel", "arbitrary"),
        name="even_out",
    )(o, proj, yb, og, mp, ms, w_out, x)


def _odd_in_kernel(x_ref, g_ref, mpsh, mpsc, mssh, mssc, wa_ref, wb_ref, ba_ref, bb_ref, u_ref, hb_ref, *, ntp, reps):
    i = pl.program_id(0)

    @pl.when(pl.program_id(1) == 0)
    def _():
        is_s = i >= ntp
        h = _rms(x_ref[...]) * g_ref[...] * (1 + _tile_mod(mpsc, mssc, is_s, reps)) + _tile_mod(mpsh, mssh, is_s, reps)
        hb_ref[...] = h.astype(BF16)

    hb = hb_ref[...]
    a = jnp.dot(hb, wa_ref[0].astype(BF16), preferred_element_type=F32) + ba_ref[0]
    b = jnp.dot(hb, wb_ref[0].astype(BF16), preferred_element_type=F32) + bb_ref[0]
    u_ref[...] = a * jax.nn.sigmoid(b)


def _odd_in(x, g, mp, ms, w_in, b_in, li, dims):
    t, d = x.shape
    tm, ntp, tps, bp, bs, reps = dims
    dc = w_in.shape[2] // 2
    tn = min(512, dc)
    nj = dc // tn
    sh = _mod_specs(0, tps, bp, bs, d, 2)
    sc = _mod_specs(1, tps, bp, bs, d, 2)
    return pl.pallas_call(
        functools.partial(_odd_in_kernel, ntp=ntp, reps=reps),
        out_shape=jax.ShapeDtypeStruct((t, dc), F32),
        grid=(t // tm, nj),
        in_specs=[
            pl.BlockSpec((tm, d), lambda i, j: (i, 0)),
            pl.BlockSpec((1, d), lambda i, j: (0, 0)),
            sh[0], sc[0], sh[1], sc[1],
            pl.BlockSpec((1, d, tn), lambda i, j: (li, 0, j)),
            pl.BlockSpec((1, d, tn), lambda i, j: (li, 0, nj + j)),
            pl.BlockSpec((1, 1, tn), lambda i, j: (li, 0, j)),
            pl.BlockSpec((1, 1, tn), lambda i, j: (li, 0, nj + j)),
        ],
        out_specs=pl.BlockSpec((tm, tn), lambda i, j: (i, j)),
        scratch_shapes=[pltpu.VMEM((tm, d), BF16)],
        compiler_params=_params("parallel", "arbitrary"),
        name="odd_in",
    )(x, g, mp, mp, ms, ms, w_in, w_in, b_in, b_in)


def _dwconv_kernel(u_ref, halo_ref, w_ref, b_ref, z_ref, ext_ref, *, stride, hr, tps, zero_start):
    tm = u_ref.shape[0]
    keep = jnp.logical_not(jnp.logical_and(zero_start, pl.program_id(0) % tps == 0)).astype(F32)
    ext_ref[pl.ds(0, hr), :] = halo_ref[...] * keep
    ext_ref[pl.ds(hr, tm), :] = u_ref[...]
    w = w_ref[0]
    b = b_ref[0]
    rb = min(32, tm)

    def chunk(ci, carry):
        r0 = pl.multiple_of(ci * rb, rb)
        z_ref[pl.ds(r0, rb), :] = _causal_taps(ext_ref, r0, rb, slice(None), w, CONV_C, stride, hr) + b
        return carry

    lax.fori_loop(0, tm // rb, chunk, 0)


def _dwconv(u, halo, dw_w, dw_b, li, row0, ntiles, tm, stride, hr, tps, zero_start):
    dc = u.shape[1]
    cb = min(256, dc)
    if halo is None:
        hb = tm // hr
        harg = u
        hspec = pl.BlockSpec((hr, cb), lambda i, c: (jnp.maximum((row0 + i) * hb - 1, 0), c))
    else:
        harg = halo
        hspec = pl.BlockSpec((hr, cb), lambda i, c: (0, c))
    return pl.pallas_call(
        functools.partial(_dwconv_kernel, stride=stride, hr=hr, tps=tps, zero_start=zero_start),
        out_shape=jax.ShapeDtypeStruct((ntiles * tm, dc), F32),
        grid=(ntiles, dc // cb),
        in_specs=[
            pl.BlockSpec((tm, cb), lambda i, c: (row0 + i, c)),
            hspec,
            pl.BlockSpec((1, CONV_C, cb), lambda i, c: (li, 0, c)),
            pl.BlockSpec((1, 1, cb), lambda i, c: (li, 0, c)),
        ],
        out_specs=pl.BlockSpec((tm, cb), lambda i, c: (i, c)),
        scratch_shapes=[pltpu.VMEM((hr + tm, cb), F32)],
        compiler_params=_params("parallel", "parallel"),
        name="dwconv",
    )(u, harg, dw_w, dw_b)


def _odd_out_kernel(z_ref, lg_ref, lb_ref, mpg, msg, w_ref, b_ref, x_ref, xo_ref, zs_ref, *, ntp, reps):
    i = pl.program_id(0)

    @pl.when(pl.program_id(1) == 0)
    def _():
        z = z_ref[...]
        zc = z - jnp.mean(z, axis=-1, keepdims=True)
        y = zc * lax.rsqrt(jnp.mean(zc * zc, axis=-1, keepdims=True) + EPS)
        zs_ref[...] = _silu(y * lg_ref[0] + lb_ref[0]).astype(BF16)

    y = jnp.dot(zs_ref[...], w_ref[0].astype(BF16), preferred_element_type=F32) + b_ref[0]
    xo_ref[...] = x_ref[...] + _tile_mod(mpg, msg, i >= ntp, reps) * y


def _odd_out(z, ln_g, ln_b, mp, ms, w_out, b_out, li, x, dims):
    t, d = x.shape
    tm, ntp, tps, bp, bs, reps = dims
    dc = z.shape[1]
    tn = min(1024, d)
    nj = d // tn
    mpg = pl.BlockSpec((1, 1, 1, tn), lambda i, j: (2, jnp.minimum(i // tps, bp - 1), 0, j))
    msg = pl.BlockSpec((1, bs, tn), lambda i, j: (2, 0, j))
    return pl.pallas_call(
        functools.partial(_odd_out_kernel, ntp=ntp, reps=reps),
        out_shape=jax.ShapeDtypeStruct((t, d), F32),
        grid=(t // tm, nj),
        in_specs=[
            pl.BlockSpec((tm, dc), lambda i, j: (i, 0)),
            pl.BlockSpec((1, 1, dc), lambda i, j: (li, 0, 0)),
            pl.BlockSpec((1, 1, dc), lambda i, j: (li, 0, 0)),
            mpg, msg,
            pl.BlockSpec((1, dc, tn), lambda i, j: (li, 0, j)),
            pl.BlockSpec((1, 1, tn), lambda i, j: (li, 0, j)),
            pl.BlockSpec((tm, tn), lambda i, j: (i, j)),
        ],
        out_specs=pl.BlockSpec((tm, tn), lambda i, j: (i, j)),
        scratch_shapes=[pltpu.VMEM((tm, dc), BF16)],
        compiler_params=_params("parallel", "arbitrary"),
        name="odd_out",
    )(z, ln_g, ln_b, mp, ms, w_out, b_out, x)


def _ffnpre_kernel(x_ref, g_ref, mpsh, mpsc, mssh, mssc, wr_ref, br_ref,
                   h_ref, idx_ref, gate_ref, rank_ref, cnt_ref, carry_ref, *, ntp, reps):
    i = pl.program_id(0)
    tm = x_ref.shape[0]
    is_s = i >= ntp
    h = _rms(x_ref[...]) * g_ref[0] * (1 + _tile_mod(mpsc, mssc, is_s, reps)) + _tile_mod(mpsh, mssh, is_s, reps)
    h_ref[...] = h
    logits = _mm_hi(h, wr_ref[0]) + br_ref[0]

    @pl.when(i == 0)
    def _():
        carry_ref[...] = jnp.zeros(carry_ref.shape, F32)

    lane = lax.broadcasted_iota(I32, logits.shape, 1)
    work = logits
    sels, tops, picks = [], [], []
    for _ in range(TOP_K):
        m = jnp.max(work, axis=-1, keepdims=True)
        pick = jnp.min(jnp.where(work == m, lane, V7X_LANES), axis=-1, keepdims=True)
        sel = lane == pick
        work = jnp.where(sel, -jnp.inf, work)
        sels.append(sel)
        tops.append(m)
        picks.append(pick)
    exps = [jnp.exp(m - tops[0]) for m in tops]
    denom = exps[0]
    for e in exps[1:]:
        denom = denom + e

    onehot = sels[0]
    for s in sels[1:]:
        onehot = jnp.logical_or(onehot, s)
    onehot = onehot.astype(F32)
    r = lax.broadcasted_iota(I32, (tm, tm), 0)
    c = lax.broadcasted_iota(I32, (tm, tm), 1)
    before = _mm((r > c).astype(F32), onehot) + carry_ref[...]
    carry_ref[...] = carry_ref[...] + jnp.sum(onehot, axis=0, keepdims=True)

    idx_out = jnp.zeros(logits.shape, I32)
    gate_out = jnp.zeros(logits.shape, F32)
    rank_out = jnp.zeros(logits.shape, I32)
    for kk in range(TOP_K):
        rank = jnp.sum(jnp.where(sels[kk], before, 0.0), axis=-1, keepdims=True).astype(I32)
        idx_out = jnp.where(lane == kk, picks[kk], idx_out)
        gate_out = jnp.where(lane == kk, exps[kk] / denom, gate_out)
        rank_out = jnp.where(lane == kk, rank, rank_out)
    idx_ref[...] = idx_out
    gate_ref[...] = gate_out
    rank_ref[...] = rank_out
    cnt_ref[...] = carry_ref[...].astype(I32)


def _ffnpre(x, g, mp, ms, w_router, b_router, li, dims):
    t, d = x.shape
    tm, ntp, tps, bp, bs, reps = dims
    sh = _mod_specs(3, tps, bp, bs, d, 1)
    sc = _mod_specs(4, tps, bp, bs, d, 1)
    lanes = jax.ShapeDtypeStruct((t, V7X_LANES), I32)
    rowspec = pl.BlockSpec((tm, V7X_LANES), lambda i: (i, 0))
    return pl.pallas_call(
        functools.partial(_ffnpre_kernel, ntp=ntp, reps=reps),
        out_shape=(jax.ShapeDtypeStruct((t, d), F32), lanes, jax.ShapeDtypeStruct((t, V7X_LANES), F32), lanes,
                   jax.ShapeDtypeStruct((1, V7X_LANES), I32)),
        grid=(t // tm,),
        in_specs=[
            pl.BlockSpec((tm, d), lambda i: (i, 0)),
            pl.BlockSpec((1, 1, d), lambda i: (li, 0, 0)),
            sh[0], sc[0], sh[1], sc[1],
            pl.BlockSpec((1, d, V7X_LANES), lambda i: (li, 0, 0)),
            pl.BlockSpec((1, 1, V7X_LANES), lambda i: (li, 0, 0)),
        ],
        out_specs=(pl.BlockSpec((tm, d), lambda i: (i, 0)), rowspec, rowspec, rowspec,
                   pl.BlockSpec((1, V7X_LANES), lambda i: (0, 0))),
        scratch_shapes=[pltpu.VMEM((1, V7X_LANES), F32)],
        compiler_params=_params("arbitrary"),
        name="ffn_pre_router",
    )(x, g, mp, mp, ms, ms, w_router, b_router)


def _dispatch_kernel(slot_ref, zrow_ref, h_ref, xs_ref, zero_ref, sem, *, nexp, bm):
    i = pl.program_id(0)
    tm = h_ref.shape[0]

    @pl.when(i == 0)
    def _():
        zero_ref[...] = jnp.zeros(zero_ref.shape, F32)

        def zcopy(r):
            return pltpu.make_async_copy(zero_ref, xs_ref.at[pl.ds(r, 1)], sem)

        def zfill(e, carry):
            base = zrow_ref[e]

            def issue(j, cc):
                for jj in range(ROW_UNROLL):
                    zcopy(base + j * ROW_UNROLL + jj).start()
                return cc

            lax.fori_loop(0, bm // ROW_UNROLL, issue, 0)

            def drain(j, cc):
                for jj in range(ROW_UNROLL):
                    zcopy(base + j * ROW_UNROLL + jj).wait()
                return cc

            lax.fori_loop(0, bm // ROW_UNROLL, drain, 0)
            return carry

        lax.fori_loop(0, nexp, zfill, 0)

    def rcopy(t, kk):
        return pltpu.make_async_copy(h_ref.at[pl.ds(t, 1)],
                                     xs_ref.at[pl.ds(slot_ref[(i * tm + t) * TOP_K + kk], 1)], sem)

    def group(g, fn):
        for tt in range(ROW_UNROLL):
            for kk in range(TOP_K):
                fn(rcopy(g * ROW_UNROLL + tt, kk))

    ngroups = tm // ROW_UNROLL

    def body(g, carry):
        group(g, lambda c: c.start())

        @pl.when(g > 0)
        def _():
            group(g - 1, lambda c: c.wait())

        return carry

    lax.fori_loop(0, ngroups, body, 0)
    group(ngroups - 1, lambda c: c.wait())


def _dispatch(slots, zrows, h, rows, bm, tm):
    t, d = h.shape
    nexp = zrows.shape[0]
    assert tm % ROW_UNROLL == 0 and bm % ROW_UNROLL == 0
    return pl.pallas_call(
        functools.partial(_dispatch_kernel, nexp=nexp, bm=bm),
        out_shape=jax.ShapeDtypeStruct((rows, d), F32),
        grid_spec=pltpu.PrefetchScalarGridSpec(
            num_scalar_prefetch=2,
            grid=(t // tm,),
            in_specs=[pl.BlockSpec((tm, d), lambda i, sl, zr: (i, 0))],
            out_specs=pl.BlockSpec(memory_space=pl.ANY),
            scratch_shapes=[pltpu.VMEM((1, d), F32), pltpu.SemaphoreType.DMA(())],
        ),
        compiler_params=_params("arbitrary"),
        name="moe_dispatch",
    )(slots, zrows, h)


def _weight_stream(be_ref, grp_ref, ge_ref, meta_ref, copies, install):
    j, i = pl.program_id(0), pl.program_id(1)
    nj = pl.num_programs(0)
    nv, ng = meta_ref[0], meta_ref[1]
    valid = i < nv
    first = jnp.logical_and(valid, jnp.logical_or(i == 0, be_ref[i] != be_ref[jnp.maximum(i - 1, 0)]))

    @pl.when(first)
    def _():
        grp = grp_ref[i]

        @pl.when(jnp.logical_and(j == 0, grp == 0))
        def _():
            for c in copies(j, be_ref[i]):
                c.start()

        for c in copies(j, be_ref[i]):
            c.wait()
        install()
        wrap = grp + 1 == ng
        nxt_j = jnp.where(wrap, j + 1, j)
        nxt_e = ge_ref[jnp.where(wrap, 0, grp + 1)]

        @pl.when(nxt_j < nj)
        def _():
            for c in copies(nxt_j, nxt_e):
                c.start()

    return valid


def _expert_up_kernel(be_ref, grp_ref, ge_ref, meta_ref, x_ref, w_ref, bg_ref, bu_ref, act_ref,
                      sg_ref, su_ref, wgb_ref, wub_ref, sem, *, li, f):
    tn = sg_ref.shape[1]

    def copies(j, e):
        col = pl.multiple_of(j * tn, tn)
        return (pltpu.make_async_copy(w_ref.at[li, e, :, pl.ds(col, tn)], sg_ref, sem.at[0]),
                pltpu.make_async_copy(w_ref.at[li, e, :, pl.ds(f + col, tn)], su_ref, sem.at[1]))

    def install():
        wgb_ref[...] = sg_ref[...].astype(BF16)
        wub_ref[...] = su_ref[...].astype(BF16)

    valid = _weight_stream(be_ref, grp_ref, ge_ref, meta_ref, copies, install)

    @pl.when(valid)
    def _():
        xb = x_ref[...].astype(BF16)
        g = jnp.dot(xb, wgb_ref[...], preferred_element_type=F32) + bg_ref[0, 0]
        u = jnp.dot(xb, wub_ref[...], preferred_element_type=F32) + bu_ref[0, 0]
        gate = jnp.minimum(g, SWIGLU_LIMIT)
        up = jnp.clip(u, -SWIGLU_LIMIT, SWIGLU_LIMIT)
        act_ref[...] = ((up + 1) * gate * jax.nn.sigmoid(SWIGLU_ALPHA * gate)).astype(BF16)


def _expert_up(route, xs, w_gate_up, b_gate_up, li, bm):
    rows, d = xs.shape
    nblk = rows // bm
    f = w_gate_up.shape[3] // 2
    tn = min(EXPERT_COLS, f)
    nj = f // tn
    blk = lambda i, meta: jnp.minimum(i, meta[0] - 1)
    return pl.pallas_call(
        functools.partial(_expert_up_kernel, li=li, f=f),
        out_shape=jax.ShapeDtypeStruct((rows, f), BF16),
        grid_spec=pltpu.PrefetchScalarGridSpec(
            num_scalar_prefetch=4,
            grid=(nj, nblk),
            in_specs=[
                pl.BlockSpec((bm, d), lambda j, i, be, grp, ge, meta: (blk(i, meta), 0)),
                pl.BlockSpec(memory_space=pl.ANY),
                pl.BlockSpec((1, 1, 1, tn), lambda j, i, be, grp, ge, meta: (li, be[blk(i, meta)], 0, j)),
                pl.BlockSpec((1, 1, 1, tn), lambda j, i, be, grp, ge, meta: (li, be[blk(i, meta)], 0, nj + j)),
            ],
            out_specs=pl.BlockSpec((bm, tn), lambda j, i, be, grp, ge, meta: (blk(i, meta), j)),
            scratch_shapes=[pltpu.VMEM((d, tn), F32), pltpu.VMEM((d, tn), F32),
                            pltpu.VMEM((d, tn), BF16), pltpu.VMEM((d, tn), BF16),
                            pltpu.SemaphoreType.DMA((2,))],
        ),
        compiler_params=_params("arbitrary", "arbitrary"),
        name="expert_gate_up",
    )(*route, xs, w_gate_up, b_gate_up, b_gate_up)


def _expert_down_kernel(be_ref, grp_ref, ge_ref, meta_ref, a_ref, w_ref, b_ref, y_ref, st_ref, wb_ref, sem, *, li):
    tn = st_ref.shape[1]

    def copies(j, e):
        col = pl.multiple_of(j * tn, tn)
        return (pltpu.make_async_copy(w_ref.at[li, e, :, pl.ds(col, tn)], st_ref, sem.at[0]),)

    def install():
        wb_ref[...] = st_ref[...].astype(BF16)

    valid = _weight_stream(be_ref, grp_ref, ge_ref, meta_ref, copies, install)

    @pl.when(valid)
    def _():
        y_ref[...] = jnp.dot(a_ref[...], wb_ref[...], preferred_element_type=F32) + b_ref[0, 0]


def _expert_down(route, act, w_down, b_down, li, bm):
    rows, f = act.shape
    nblk = rows // bm
    d = w_down.shape[3]
    tn = min(EXPERT_COLS, d)
    nj = d // tn
    blk = lambda i, meta: jnp.minimum(i, meta[0] - 1)
    return pl.pallas_call(
        functools.partial(_expert_down_kernel, li=li),
        out_shape=jax.ShapeDtypeStruct((rows, d), F32),
        grid_spec=pltpu.PrefetchScalarGridSpec(
            num_scalar_prefetch=4,
            grid=(nj, nblk),
            in_specs=[
                pl.BlockSpec((bm, f), lambda j, i, be, grp, ge, meta: (blk(i, meta), 0)),
                pl.BlockSpec(memory_space=pl.ANY),
                pl.BlockSpec((1, 1, 1, tn), lambda j, i, be, grp, ge, meta: (li, be[blk(i, meta)], 0, j)),
            ],
            out_specs=pl.BlockSpec((bm, tn), lambda j, i, be, grp, ge, meta: (blk(i, meta), j)),
            scratch_shapes=[pltpu.VMEM((f, tn), F32), pltpu.VMEM((f, tn), BF16), pltpu.SemaphoreType.DMA((1,))],
        ),
        compiler_params=_params("arbitrary", "arbitrary"),
        name="expert_down",
    )(*route, act, w_down, b_down)


def _combine_kernel(slot_ref, gate_ref, mpg, msg, x_ref, ys_ref, xo_ref, buf_ref, sem, *, ntp, reps):
    i = pl.program_id(0)
    tm = x_ref.shape[0]

    def rcopy(t, kk):
        return pltpu.make_async_copy(ys_ref.at[pl.ds(slot_ref[(i * tm + t) * TOP_K + kk], 1)],
                                     buf_ref.at[kk, pl.ds(t, 1)], sem)

    def issue(g, carry):
        for tt in range(ROW_UNROLL):
            for kk in range(TOP_K):
                rcopy(g * ROW_UNROLL + tt, kk).start()
        return carry

    lax.fori_loop(0, tm // ROW_UNROLL, issue, 0)

    def drain(g, carry):
        for tt in range(ROW_UNROLL):
            for kk in range(TOP_K):
                rcopy(g * ROW_UNROLL + tt, kk).wait()
        return carry

    lax.fori_loop(0, tm // ROW_UNROLL, drain, 0)

    gates = gate_ref[...]
    y = buf_ref[0] * gates[:, 0:1]
    for kk in range(1, TOP_K):
        y = y + buf_ref[kk] * gates[:, kk:kk + 1]
    xo_ref[...] = x_ref[...] + _tile_mod(mpg, msg, i >= ntp, reps) * y


def _combine(slots, gates, mp, ms, x, ys, dims):
    t, d = x.shape
    tm, ntp, tps, bp, bs, reps = dims
    mpg = pl.BlockSpec((1, 1, 1, d), lambda i, sl: (5, jnp.minimum(i // tps, bp - 1), 0, 0))
    msg = pl.BlockSpec((1, bs, d), lambda i, sl: (5, 0, 0))
    return pl.pallas_call(
        functools.partial(_combine_kernel, ntp=ntp, reps=reps),
        out_shape=jax.ShapeDtypeStruct((t, d), F32),
        grid_spec=pltpu.PrefetchScalarGridSpec(
            num_scalar_prefetch=1,
            grid=(t // tm,),
            in_specs=[
                pl.BlockSpec((tm, V7X_LANES), lambda i, sl: (i, 0)),
                mpg, msg,
                pl.BlockSpec((tm, d), lambda i, sl: (i, 0)),
                pl.BlockSpec(memory_space=pl.ANY),
            ],
            out_specs=pl.BlockSpec((tm, d), lambda i, sl: (i, 0)),
            scratch_shapes=[pltpu.VMEM((TOP_K, tm, d), F32), pltpu.SemaphoreType.DMA(())],
        ),
        compiler_params=_params("arbitrary"),
        name="moe_combine",
    )(slots, gates, mp, ms, x, ys)


def _moe(x, li, g, mp, ms, w_router_p, b_router_p, w_gate_up, b_gate_up4, w_down, b_down4, dims):
    t, d = x.shape
    nexp = w_gate_up.shape[1]
    bm = MOE_ROWS
    h, idx, gates, rank, cnt = _ffnpre(x, g, mp, ms, w_router_p, b_router_p, li, dims)
    counts = cnt[0, :nexp]
    padded = (counts + bm - 1) // bm * bm
    pad_end = jnp.cumsum(padded)
    pad_start = pad_end - padded
    idx4 = idx[:, :TOP_K]
    start4 = jnp.sum(jnp.where(idx4[..., None] == jnp.arange(nexp, dtype=I32), pad_start, 0), axis=-1)
    slots = (start4 + rank[:, :TOP_K]).reshape(t * TOP_K).astype(I32)
    nblk = -(-(t * TOP_K + nexp * (bm - 1)) // bm) + 1
    rows = nblk * bm
    blk_start = jnp.arange(nblk, dtype=I32) * bm
    blk_expert = jnp.minimum(jnp.sum(pad_end[None, :] <= blk_start[:, None], axis=1), nexp - 1).astype(I32)
    has = counts > 0
    run_of_expert = jnp.cumsum(has.astype(I32)) - 1
    blk_run = jnp.sum(jnp.where(blk_expert[:, None] == jnp.arange(nexp, dtype=I32), run_of_expert, 0), axis=1)
    run_expert = jnp.argsort(jnp.logical_not(has), stable=True).astype(I32)
    meta = jnp.stack([pad_end[-1] // bm, jnp.sum(has.astype(I32))]).astype(I32)
    route = (blk_expert, blk_run.astype(I32), run_expert, meta)
    zrows = (pad_start + counts).astype(I32)
    xs = _dispatch(slots, zrows, h, rows, bm, dims[0])
    act = _expert_up(route, xs, w_gate_up, b_gate_up4, li, bm)
    ys = _expert_down(route, act, w_down, b_down4, li, bm)
    return _combine(slots, gates, mp, ms, x, ys, dims)


def _final_kernel(x_ref, g_ref, o_ref):
    o_ref[...] = _rms(x_ref[...]) * g_ref[...]


def _final_norm(x, g, tm):
    t, d = x.shape
    return pl.pallas_call(
        _final_kernel,
        out_shape=jax.ShapeDtypeStruct((t, d), F32),
        grid=(t // tm,),
        in_specs=[pl.BlockSpec((tm, d), lambda i: (i, 0)), pl.BlockSpec((1, d), lambda i: (0, 0))],
        out_specs=pl.BlockSpec((tm, d), lambda i: (i, 0)),
        compiler_params=_params("parallel"),
        name="final_norm",
    )(x, g)


def _to_pos_major(a):
    b, l, c = a.shape
    return jnp.swapaxes(a, 0, 1).reshape(l * b, c)


def _from_pos_major(a, b):
    lb, c = a.shape
    return jnp.swapaxes(a.reshape(lb // b, b, c), 0, 1)


def kernel(x_prompt, x_sample, state_delta, state_conv_qkv, state_conv_b, state_conv_c, c_prompt, c_sample, norm_mix, norm_ffn, w_mod, b_mod, w_in_even, conv_qkv_w, a_log, dt_bias, o_norm_g, conv_b_w, w_out_even, w_in_odd, b_in_odd, dw_w, dw_b, ln_g, ln_b, w_out_odd, b_out_odd, w_router, b_router, w_gate_up, b_gate_up, w_down, b_down, norm_final):
    bp, seq, d = x_prompt.shape
    bs, dseq, _ = x_sample.shape
    depth = w_mod.shape[0]
    nh = a_log.shape[1]
    da = nh * DV
    db = conv_b_w.shape[2]
    dc = dw_w.shape[2]
    nexp = w_router.shape[2]
    tp, ts = bp * seq, bs * dseq
    tm = ts
    assert seq % tm == 0 and bs % V7X_SUBLANES == 0 and da == db and dseq >= CONV_A - 1
    assert nexp <= V7X_LANES and 2 * nh <= V7X_LANES
    tps = seq // tm
    ntp = tp // tm
    dims = (tm, ntp, tps, bp, bs, dseq)
    chunk = math.gcd(seq, CHUNK_A)

    x = jnp.concatenate([x_prompt.reshape(tp, d), _to_pos_major(x_sample)], axis=0)
    rc = -(-(bp + bs) // V7X_SUBLANES) * V7X_SUBLANES
    c_all = jnp.zeros((rc, d), F32).at[:bp].set(c_prompt).at[bp:bp + bs].set(c_sample)
    mod = _modulation(c_all, w_mod, b_mod)

    w_router_p = jnp.zeros((depth, d, V7X_LANES), F32).at[:, :, :nexp].set(w_router)
    b_router_p = jnp.full((depth, 1, V7X_LANES), -jnp.inf, F32).at[:, 0, :nexp].set(b_router)
    b_gate_up4 = b_gate_up.reshape(depth, nexp, 1, b_gate_up.shape[2])
    b_down4 = b_down.reshape(depth, nexp, 1, d)

    qkv_w = 3 * nh * DK
    c_ab = qkv_w + da
    c_b = c_ab + 2 * nh
    hr_qkv = (CONV_A - 1) * bs
    hr_c = (CONV_C - 1) * bs
    outs = {k: [] for k in ("dp", "qp", "bp", "cp", "ds", "qs", "bs", "cs")}

    for l in range(depth):
        mp = mod[l][:, :bp].reshape(6, bp, 1, d)
        ms = mod[l][:, bp:bp + bs]
        li = l // 2
        if l % 2 == 0:
            w = w_in_even[li]
            w_main = jnp.concatenate([w[:, :c_ab], w[:, c_b:]], axis=1)
            w_ab = jnp.zeros((d, V7X_LANES), F32).at[:, :2 * nh].set(w[:, c_ab:c_b])
            proj, ab = _even_in(x, norm_mix[l:l + 1], mp, ms, w_main, w_ab, dims)
            adt = jnp.zeros((V7X_SUBLANES, V7X_LANES), F32).at[0, :nh].set(a_log[li]).at[1, :nh].set(dt_bias[li])
            wqkv, wb = conv_qkv_w[li], conv_b_w[li]
            qp, kp, vp, gbp, ybp, utp = _evenprep(proj, ab, None, wqkv, wb, adt, 0, ntp, tm, 1, V7X_SUBLANES,
                                                  tps, True, nh)
            sq = _to_pos_major(state_conv_qkv[li])
            su = jnp.concatenate([jnp.zeros(((CONV_A - CONV_B) * bs, db), F32), _to_pos_major(state_conv_b[li])], axis=0)
            halos = (sq[:, :da], sq[:, da:2 * da], sq[:, 2 * da:], su, jnp.ones_like(su))
            qs, ks, vs, gbs, ybs, uts = _evenprep(proj, ab, halos, wqkv, wb, adt, ntp, 1, tm, bs, hr_qkv,
                                                  1, False, nh)
            o_p, s_p = _delta(qp, kp, vp, gbp, None, li, bp, seq // chunk, chunk, nh)

            def seq_major(a):
                a = _from_pos_major(a, bs)
                return jnp.pad(a, ((0, 0), (0, V7X_SUBLANES - dseq), (0, 0))).reshape(bs * V7X_SUBLANES, a.shape[2])

            o_s8, s_s = _delta(seq_major(qs), seq_major(ks), seq_major(vs), seq_major(gbs), state_delta, li,
                               bs, 1, V7X_SUBLANES, nh)
            o_s = _to_pos_major(o_s8.reshape(bs, V7X_SUBLANES, da)[:, :dseq])
            o = jnp.concatenate([o_p, o_s], axis=0)
            yb = jnp.concatenate([ybp, ybs], axis=0)
            x = _even_out(o, proj, yb, o_norm_g[li:li + 1], mp, ms, w_out_even, li, x, dims, nh)

            qkv_raw = proj[:, :qkv_w]
            outs["dp"].append(s_p)
            outs["qp"].append(qkv_raw[:tp].reshape(bp, seq, qkv_w)[:, seq - (CONV_A - 1):])
            outs["bp"].append(utp.reshape(bp, tps, V7X_SUBLANES, db)[:, tps - 1, V7X_SUBLANES - (CONV_B - 1):])
            outs["ds"].append(s_s)
            outs["qs"].append(_from_pos_major(qkv_raw[tp:], bs)[:, dseq - (CONV_A - 1):])
            u_ext = jnp.concatenate([state_conv_b[li], _from_pos_major(uts, bs)], axis=1)
            outs["bs"].append(u_ext[:, u_ext.shape[1] - (CONV_B - 1):])
        else:
            u = _odd_in(x, norm_mix[l:l + 1], mp, ms, w_in_odd, b_in_odd.reshape(-1, 1, 2 * dc), li, dims)
            dwb = dw_b.reshape(-1, 1, dc)
            zp = _dwconv(u, None, dw_w, dwb, li, 0, ntp, tm, 1, 32, tps, True)
            zs = _dwconv(u, _to_pos_major(state_conv_c[li]), dw_w, dwb, li, ntp, 1, tm, bs, hr_c, 1, False)
            z = jnp.concatenate([zp, zs], axis=0)
            x = _odd_out(z, ln_g.reshape(-1, 1, dc), ln_b.reshape(-1, 1, dc), mp, ms, w_out_odd,
                         b_out_odd.reshape(-1, 1, d), li, x, dims)
            outs["cp"].append(u[:tp].reshape(bp, seq, dc)[:, seq - (CONV_C - 1):])
            c_ext = jnp.concatenate([state_conv_c[li], _from_pos_major(u[tp:], bs)], axis=1)
            outs["cs"].append(c_ext[:, c_ext.shape[1] - (CONV_C - 1):])
        x = _moe(x, l, norm_ffn.reshape(depth, 1, d), mp, ms, w_router_p, b_router_p, w_gate_up, b_gate_up4,
                 w_down, b_down4, dims)

    y = _final_norm(x, norm_final.reshape(1, d), tm)
    y_prompt = y[:tp].reshape(bp, seq, d)
    y_sample = _from_pos_major(y[tp:], bs)
    return (y_prompt, y_sample,
            jnp.stack(outs["dp"]), jnp.stack(outs["qp"]), jnp.stack(outs["bp"]), jnp.stack(outs["cp"]),
            jnp.stack(outs["ds"]), jnp.stack(outs["qs"]), jnp.stack(outs["bs"]), jnp.stack(outs["cs"]))
```

```python
import functools
import math

import jax
import jax.numpy as jnp
from jax import lax
from jax.experimental import pallas as pl
from jax.experimental.pallas import tpu as pltpu

F32 = jnp.float32
BF16 = jnp.bfloat16
I32 = jnp.int32
U32 = jnp.uint32
EPS = 1e-6
HI = lax.Precision.HIGHEST

DK = 128
DV = 128
CONV_A = 4
CONV_B = 3
CONV_C = 31
CHUNK_A = 64
TOP_K = 4
SWIGLU_ALPHA = 1.702
SWIGLU_LIMIT = 7.0

V7X_LANES = 128
V7X_SUBLANES = 8
V7X_VMEM_LIMIT_BYTES = 56 * 1024 * 1024
MOE_ROWS = 256
EXPERT_COLS = 1024
DELTA_SEQS_PROMPT = 2
DELTA_SEQS_SAMPLE = 4
ROW_UNROLL = 8


def _params(*sem):
    return pltpu.CompilerParams(dimension_semantics=sem, vmem_limit_bytes=V7X_VMEM_LIMIT_BYTES)


def _mm(a, b):
    return jnp.dot(a.astype(BF16), b.astype(BF16), preferred_element_type=F32)


def _mm_hi(a, b):
    return jnp.dot(a, b, precision=HI, preferred_element_type=F32)


_NN = (((1,), (0,)), ((), ()))
_NT = (((1,), (1,)), ((), ()))
_TN = (((0,), (0,)), ((), ()))


def _dot(a, b, dims):
    return lax.dot_general(a, b, dims, preferred_element_type=F32)


def _split2(a):
    hi = a.astype(BF16)
    return hi, (a - hi.astype(F32)).astype(BF16)


def _mm3s(a, b):
    return _dot(a[0], b[0], _NN) + (_dot(a[0], b[1], _NN) + _dot(a[1], b[0], _NN))


def _pack_bf16_pairs(x):
    half = x.shape[1] // 2
    bits = lax.bitcast_convert_type(x.astype(BF16).astype(F32), U32)
    return (bits[:, half:] & jnp.uint32(0xFFFF0000)) | (bits[:, :half] >> 16)


def _unpack_bf16_pairs(w):
    lo = lax.bitcast_convert_type(w << 16, F32)
    hi = lax.bitcast_convert_type(w & jnp.uint32(0xFFFF0000), F32)
    return jnp.concatenate([lo, hi], axis=1).astype(BF16)


def _silu(x):
    return x * jax.nn.sigmoid(x)


def _rms(x):
    return x * lax.rsqrt(jnp.mean(x * x, axis=-1, keepdims=True) + EPS)


def _tile_mod(mp_ref, ms_ref, is_sample, reps):
    s = ms_ref[0]
    s = jnp.concatenate([s] * reps, axis=0)
    return jnp.where(is_sample, s, mp_ref[0, 0])


def _mod_specs(k, tps, bp, bs, d, ngrid):
    if ngrid == 1:
        mp = pl.BlockSpec((1, 1, 1, d), lambda i: (k, jnp.minimum(i // tps, bp - 1), 0, 0))
        ms = pl.BlockSpec((1, bs, d), lambda i: (k, 0, 0))
    else:
        mp = pl.BlockSpec((1, 1, 1, d), lambda i, j: (k, jnp.minimum(i // tps, bp - 1), 0, 0))
        ms = pl.BlockSpec((1, bs, d), lambda i, j: (k, 0, 0))
    return mp, ms


def _mod_kernel(c_ref, w_ref, b_ref, o_ref):
    o_ref[0, 0] = _mm(_silu(c_ref[...]), w_ref[0]) + b_ref[0]


def _modulation(c_all, w_mod, b_mod):
    depth, d, d6 = w_mod.shape
    rc = c_all.shape[0]
    tn = min(512, d)
    nj = d // tn
    return pl.pallas_call(
        _mod_kernel,
        out_shape=jax.ShapeDtypeStruct((depth, 6, rc, d), F32),
        grid=(depth, 6, nj),
        in_specs=[
            pl.BlockSpec((rc, d), lambda l, k, j: (0, 0)),
            pl.BlockSpec((1, d, tn), lambda l, k, j: (l, 0, k * nj + j)),
            pl.BlockSpec((1, 1, tn), lambda l, k, j: (l, 0, k * nj + j)),
        ],
        out_specs=pl.BlockSpec((1, 1, rc, tn), lambda l, k, j: (l, k, 0, j)),
        compiler_params=_params("parallel", "parallel", "parallel"),
        name="modulation",
    )(c_all, w_mod, b_mod.reshape(depth, 1, d6))


def _prenorm_kernel(*refs, ntp, reps, has_ab):
    if has_ab:
        x_ref, g_ref, mpsh, mpsc, mssh, mssc, wab_ref, hb_ref, ab_ref = refs
    else:
        x_ref, g_ref, mpsh, mpsc, mssh, mssc, hb_ref = refs
    is_s = pl.program_id(0) >= ntp
    h = _rms(x_ref[...]) * g_ref[...] * (1 + _tile_mod(mpsc, mssc, is_s, reps)) + _tile_mod(mpsh, mssh, is_s, reps)
    hb = h.astype(BF16)
    hb_ref[...] = hb
    if has_ab:
        ab_ref[...] = jnp.dot(hb, wab_ref[...].astype(BF16), preferred_element_type=F32)


def _prenorm(x, g, mp, ms, w_ab, dims):
    t, d = x.shape
    tm, ntp, tps, bp, bs, reps = dims
    sh = _mod_specs(0, tps, bp, bs, d, 1)
    sc = _mod_specs(1, tps, bp, bs, d, 1)
    in_specs = [pl.BlockSpec((tm, d), lambda i: (i, 0)), pl.BlockSpec((1, d), lambda i: (0, 0)),
                sh[0], sc[0], sh[1], sc[1]]
    args = [x, g, mp, mp, ms, ms]
    out_shape = [jax.ShapeDtypeStruct((t, d), BF16)]
    out_specs = [pl.BlockSpec((tm, d), lambda i: (i, 0))]
    if w_ab is not None:
        in_specs.append(pl.BlockSpec((d, V7X_LANES), lambda i: (0, 0)))
        args.append(w_ab)
        out_shape.append(jax.ShapeDtypeStruct((t, V7X_LANES), F32))
        out_specs.append(pl.BlockSpec((tm, V7X_LANES), lambda i: (i, 0)))
    return pl.pallas_call(
        functools.partial(_prenorm_kernel, ntp=ntp, reps=reps, has_ab=w_ab is not None),
        out_shape=tuple(out_shape),
        grid=(t // tm,),
        in_specs=in_specs,
        out_specs=tuple(out_specs),
        compiler_params=_params("parallel"),
        name="prenorm",
    )(*args)


def _even_in_kernel(hb_ref, w_ref, proj_ref, wb_ref):
    @pl.when(pl.program_id(1) == 0)
    def _():
        wb_ref[...] = w_ref[...].astype(BF16)

    proj_ref[...] = jnp.dot(hb_ref[...], wb_ref[...], preferred_element_type=F32)


def _even_in(hb, w_main, tm):
    t, d = hb.shape
    nmain = w_main.shape[1]
    tn = nmain // 7
    return pl.pallas_call(
        _even_in_kernel,
        out_shape=jax.ShapeDtypeStruct((t, nmain), F32),
        grid=(7, t // tm),
        in_specs=[pl.BlockSpec((tm, d), lambda j, i: (i, 0)), pl.BlockSpec((d, tn), lambda j, i: (0, j))],
        out_specs=pl.BlockSpec((tm, tn), lambda j, i: (i, j)),
        scratch_shapes=[pltpu.VMEM((d, tn), BF16)],
        compiler_params=_params("arbitrary", "arbitrary"),
        name="even_in",
    )(hb, w_main)


def _causal_taps(ext_ref, r0, rb, lanes, w, width, stride, hr):
    acc = None
    if stride % V7X_SUBLANES == 0:
        for j in range(width):
            start = pl.multiple_of(r0 + (hr - (width - 1 - j) * stride), V7X_SUBLANES)
            term = ext_ref[pl.ds(start, rb), lanes] * w[j:j + 1, :]
            acc = term if acc is None else acc + term
        return acc
    look = -(-(width - 1) * stride // V7X_SUBLANES) * V7X_SUBLANES
    nrow = rb + look
    win = ext_ref[pl.ds(pl.multiple_of(r0 + (hr - look), V7X_SUBLANES), nrow), lanes]
    for res in range(V7X_SUBLANES):
        taps = [j for j in range(width) if (look - (width - 1 - j) * stride) % V7X_SUBLANES == res]
        if not taps:
            continue
        shifted = win if res == 0 else pltpu.roll(win, nrow - res, axis=0)
        for j in taps:
            off = look - (width - 1 - j) * stride - res
            term = shifted[off:off + rb, :] * w[j:j + 1, :]
            acc = term if acc is None else acc + term
    return acc


def _evenprep_kernel(q_ref, k_ref, v_ref, xb_ref, cp_ref, bpost_ref, ab_ref,
                     hq_ref, hk_ref, hv_ref, hxb_ref, hcp_ref,
                     wqkv_ref, wb_ref, adt_ref,
                     qo_ref, ko_ref, vo_ref, gb_ref, yb_ref, ut_ref,
                     extq, extk, extv, extu, *, stride, hr, tps, zero_start, nh):
    tm = q_ref.shape[0]
    da = q_ref.shape[1]
    keep = jnp.logical_not(jnp.logical_and(zero_start, pl.program_id(0) % tps == 0)).astype(F32)
    for ext, halo, cur in ((extq, hq_ref, q_ref), (extk, hk_ref, k_ref), (extv, hv_ref, v_ref)):
        ext[pl.ds(0, hr), :] = halo[...] * keep
        ext[pl.ds(hr, tm), :] = cur[...]
    extu[pl.ds(0, hr), :] = hcp_ref[...] * hxb_ref[...] * keep
    extu[pl.ds(hr, tm), :] = cp_ref[...] * xb_ref[...]
    tr = ut_ref.shape[0]
    ut_ref[...] = extu[pl.ds(hr + tm - tr, tr), :]

    ab = ab_ref[...]
    adt = adt_ref[...]
    z = ab + adt[1:2, :]
    softplus = jnp.maximum(z, 0.0) + jnp.log(1.0 + jnp.exp(-jnp.abs(z)))
    lane = lax.broadcasted_iota(I32, ab.shape, 1)
    gb_ref[...] = jnp.where(lane < nh, -jnp.exp(adt[0:1, :]) * softplus, jax.nn.sigmoid(ab))

    wqkv = wqkv_ref[...]
    wb = wb_ref[...]
    rb = min(64, tm)

    def chunk(ci, carry):
        r0 = pl.multiple_of(ci * rb, rb)
        rows = pl.ds(r0, rb)
        for h in range(nh):
            sl = slice(h * DK, (h + 1) * DK)
            taps = lambda ext, w0: _causal_taps(ext, r0, rb, sl, wqkv[:, w0 + h * DK:w0 + (h + 1) * DK],
                                                CONV_A, stride, hr)
            qh = _silu(taps(extq, 0))
            kh = _silu(taps(extk, da))
            qo_ref[rows, sl] = qh * lax.rsqrt(jnp.sum(qh * qh, axis=-1, keepdims=True) + EPS) * (DK ** -0.5)
            ko_ref[rows, sl] = kh * lax.rsqrt(jnp.sum(kh * kh, axis=-1, keepdims=True) + EPS)
            vo_ref[rows, sl] = _silu(taps(extv, 2 * da))
            yb = bpost_ref[rows, sl] * _causal_taps(extu, r0, rb, sl, wb[:, sl], CONV_B, stride, hr)
            yb_ref[rows, sl] = yb.astype(BF16)
        return carry

    lax.fori_loop(0, tm // rb, chunk, 0)


def _evenprep(proj, ab, halos, wqkv, wb, adt, row0, ntiles, tm, stride, hr, tps, zero_start, nh):
    da = nh * DK
    tr = max(V7X_SUBLANES, (CONV_B - 1) * stride)
    cur = lambda c: pl.BlockSpec((tm, da), lambda i: (row0 + i, c))
    if halos is None:
        hb = tm // hr
        hspec = lambda c: pl.BlockSpec((hr, da), lambda i: (jnp.maximum((row0 + i) * hb - 1, 0), c))
        hargs = [proj] * 5
        hspecs = [hspec(0), hspec(1), hspec(2), hspec(4), hspec(5)]
    else:
        hargs = list(halos)
        hspecs = [pl.BlockSpec((hr, da), lambda i: (0, 0)) for _ in range(5)]
    rows = ntiles * tm
    full = lambda a: pl.BlockSpec(a.shape, lambda i: (0, 0))
    return pl.pallas_call(
        functools.partial(_evenprep_kernel, stride=stride, hr=hr, tps=tps, zero_start=zero_start, nh=nh),
        out_shape=(jax.ShapeDtypeStruct((rows, da), F32), jax.ShapeDtypeStruct((rows, da), F32),
                   jax.ShapeDtypeStruct((rows, da), F32), jax.ShapeDtypeStruct((rows, V7X_LANES), F32),
                   jax.ShapeDtypeStruct((rows, da), BF16), jax.ShapeDtypeStruct((ntiles * tr, da), F32)),
        grid=(ntiles,),
        in_specs=[cur(0), cur(1), cur(2), cur(4), cur(5), cur(6),
                  pl.BlockSpec((tm, V7X_LANES), lambda i: (row0 + i, 0))] + hspecs + [full(wqkv), full(wb), full(adt)],
        out_specs=(pl.BlockSpec((tm, da), lambda i: (i, 0)), pl.BlockSpec((tm, da), lambda i: (i, 0)),
                   pl.BlockSpec((tm, da), lambda i: (i, 0)), pl.BlockSpec((tm, V7X_LANES), lambda i: (i, 0)),
                   pl.BlockSpec((tm, da), lambda i: (i, 0)), pl.BlockSpec((tr, da), lambda i: (i, 0))),
        scratch_shapes=[pltpu.VMEM((hr + tm, da), F32) for _ in range(4)],
        compiler_params=_params("parallel"),
        name="evenprep",
    )(proj, proj, proj, proj, proj, proj, ab, *hargs, wqkv, wb, adt)


def _delta_kernel(*refs, nh, has_s0):
    if has_s0:
        q_ref, k_ref, v_ref, gb_ref, s0_ref, o_ref, sout_ref, s_ref = refs
    else:
        q_ref, k_ref, v_ref, gb_ref, o_ref, sout_ref, s_ref = refs
    n = pl.program_id(1)
    nseq, c = q_ref.shape[0], q_ref.shape[1]

    @pl.when(n == 0)
    def _():
        s_ref[...] = s0_ref[0] if has_s0 else jnp.zeros(s_ref.shape, F32)

    row = lax.broadcasted_iota(I32, (c, c), 0)
    col = lax.broadcasted_iota(I32, (c, c), 1)
    causal = row >= col
    strict = row > col
    eye = (row == col).astype(F32)

    tri = causal.astype(BF16)
    gbs, gcums, gcum_ts = [], [], []
    for sq in range(nseq):
        gb = gb_ref[sq]
        g1 = gb.astype(BF16)
        r1 = gb - g1.astype(F32)
        g2 = r1.astype(BF16)
        g3 = (r1 - g2.astype(F32)).astype(BF16)
        gcum = _dot(tri, g1, _NN) + (_dot(tri, g2, _NN) + _dot(tri, g3, _NN))
        gbs.append(gb)
        gcums.append(gcum)
        gcum_ts.append(gcum.T)
    levels = int(math.log2(c))

    chains = [(sq, hh) for sq in range(nseq) for hh in range(nh)]
    heads = range(len(chains))
    lanes = [slice(hh * DK, (hh + 1) * DK) for _, hh in chains]
    qs = [q_ref[sq, :, lanes[h]] for h, (sq, _) in enumerate(chains)]
    ks = [k_ref[sq, :, lanes[h]] for h, (sq, _) in enumerate(chains)]
    g_col = [gcums[sq][:, hh:hh + 1] for sq, hh in chains]
    g_last = [gcums[sq][c - 1:c, hh:hh + 1] for sq, hh in chains]
    beta = [gbs[sq][:, nh + hh:nh + hh + 1] for sq, hh in chains]
    decay = [jnp.where(causal, jnp.exp(jnp.where(causal, g_col[h] - gcum_ts[sq][hh:hh + 1, :], 0.0)), 0.0)
             for h, (sq, hh) in enumerate(chains)]
    kb = [ks[h] * beta[h] for h in heads]
    khb = [ks[h].astype(BF16) for h in heads]
    low = [jnp.where(strict, _dot(kb[h].astype(BF16), khb[h], _NT) * decay[h], 0.0) for h in heads]
    inv = [eye - low[h] for h in heads]
    pw = [_split2(low[h]) for h in heads]
    for _ in range(levels - 1):
        pw = [_split2(_mm3s(pw[h], pw[h])) for h in heads]
        inv = [inv[h] + _mm3s(_split2(inv[h]), pw[h]) for h in heads]
    eg = [jnp.exp(g_col[h]) for h in heads]
    uw = [_mm3s(_split2(inv[h]),
                _split2(jnp.concatenate([v_ref[chains[h][0], :, lanes[h]] * beta[h], kb[h] * eg[h]], axis=1)))
          for h in heads]
    intra = [jnp.where(causal, _dot(qs[h].astype(BF16), khb[h], _NT) * decay[h], 0.0) for h in heads]
    s = [s_ref[sq, hh] for sq, hh in chains]
    sb = [s[h].astype(BF16) for h in heads]
    vnb = [(uw[h][:, :DV] - _dot(uw[h][:, DV:].astype(BF16), sb[h], _NN)).astype(BF16) for h in heads]
    for h, (sq, hh) in enumerate(chains):
        o_ref[sq, :, lanes[h]] = (_dot((qs[h] * eg[h]).astype(BF16), sb[h], _NN)
                                  + _dot(intra[h].astype(BF16), vnb[h], _NN))
    for h, (sq, hh) in enumerate(chains):
        k_dec = ks[h] * jnp.exp(g_last[h] - g_col[h])
        s_ref[sq, hh] = s[h] * jnp.exp(g_last[h]) + _dot(k_dec.astype(BF16), vnb[h], _TN)

    @pl.when(n == pl.num_programs(1) - 1)
    def _():
        sout_ref[...] = s_ref[...]


def _delta(q, k, v, gb, s0, li, nb, nchunks, c, nh, nseq):
    da = nh * DK
    assert nb % nseq == 0
    seqlen = nchunks * c
    view = lambda a: a.reshape(nb, seqlen, a.shape[1])
    rowspec = lambda w, col=0: pl.BlockSpec((nseq, c, w), lambda b, n: (b, n, col))
    sspec = pl.BlockSpec((nseq, nh, DK, DV), lambda b, n: (b, 0, 0, 0))
    if k is None:
        in_specs = [rowspec(da, 0), rowspec(da, 1), rowspec(da, 2), rowspec(V7X_LANES, 3 * da // V7X_LANES)]
        args = [view(q)] * 4
    else:
        in_specs = [rowspec(da), rowspec(da), rowspec(da), rowspec(V7X_LANES)]
        args = [view(q), view(k), view(v), view(gb)]
    if s0 is not None:
        in_specs.append(pl.BlockSpec((1, nseq, nh, DK, DV), lambda b, n: (li, b, 0, 0, 0)))
        args.append(s0)
    o, s_out = pl.pallas_call(
        functools.partial(_delta_kernel, nh=nh, has_s0=s0 is not None),
        out_shape=(jax.ShapeDtypeStruct((nb, seqlen, da), F32), jax.ShapeDtypeStruct((nb, nh, DK, DV), F32)),
        grid=(nb // nseq, nchunks),
        in_specs=in_specs,
        out_specs=(rowspec(da), sspec),
        scratch_shapes=[pltpu.VMEM((nseq, nh, DK, DV), F32)],
        compiler_params=_params("parallel", "arbitrary"),
        name="delta_rule",
    )(*args)
    return o.reshape(nb * seqlen, da), s_out


def _even_out_kernel(o_ref, gout_ref, yb_ref, og_ref, mpg, msg, w_ref, x_ref, xo_ref, yin_ref, *, ntp, reps, nh):
    i = pl.program_id(0)
    da = nh * DV

    @pl.when(pl.program_id(1) == 0)
    def _():
        for h in range(nh):
            sl = slice(h * DV, (h + 1) * DV)
            yin_ref[:, sl] = (_rms(o_ref[:, sl]) * og_ref[...] * _silu(gout_ref[:, sl])).astype(BF16)
        yin_ref[:, da:] = yb_ref[...]

    y = jnp.dot(yin_ref[...], w_ref[0].astype(BF16), preferred_element_type=F32)
    xo_ref[...] = x_ref[...] + _tile_mod(mpg, msg, i >= ntp, reps) * y


def _even_out(o, proj, yb, og, mp, ms, w_out, li, x, dims, nh):
    t, d = x.shape
    tm, ntp, tps, bp, bs, reps = dims
    da = nh * DV
    tn = min(1024, d)
    nj = d // tn
    mpg = pl.BlockSpec((1, 1, 1, tn), lambda i, j: (2, jnp.minimum(i // tps, bp - 1), 0, j))
    msg = pl.BlockSpec((1, bs, tn), lambda i, j: (2, 0, j))
    return pl.pallas_call(
        functools.partial(_even_out_kernel, ntp=ntp, reps=reps, nh=nh),
        out_shape=jax.ShapeDtypeStruct((t, d), F32),
        grid=(t // tm, nj),
        in_specs=[
            pl.BlockSpec((tm, da), lambda i, j: (i, 0)),
            pl.BlockSpec((tm, da), lambda i, j: (i, 3)),
            pl.BlockSpec((tm, yb.shape[1]), lambda i, j: (i, 0)),
            pl.BlockSpec((1, DV), lambda i, j: (0, 0)),
            mpg, msg,
            pl.BlockSpec((1, w_out.shape[1], tn), lambda i, j: (li, 0, j)),
            pl.BlockSpec((tm, tn), lambda i, j: (i, j)),
        ],
        out_specs=pl.BlockSpec((tm, tn), lambda i, j: (i, j)),
        scratch_shapes=[pltpu.VMEM((tm, w_out.shape[1]), BF16)],
        compiler_params=_params("parallel", "arbitrary"),
        name="even_out",
    )(o, proj, yb, og, mp, ms, w_out, x)


def _odd_in_kernel(hb_ref, wa_ref, wb_ref, ba_ref, bb_ref, u_ref, wab_ref, wbb_ref):
    @pl.when(pl.program_id(1) == 0)
    def _():
        wab_ref[...] = wa_ref[0].astype(BF16)
        wbb_ref[...] = wb_ref[0].astype(BF16)

    hb = hb_ref[...]
    a = jnp.dot(hb, wab_ref[...], preferred_element_type=F32) + ba_ref[0]
    b = jnp.dot(hb, wbb_ref[...], preferred_element_type=F32) + bb_ref[0]
    u_ref[...] = a * jax.nn.sigmoid(b)


def _odd_in(hb, w_in, b_in, li, tm):
    t, d = hb.shape
    dc = w_in.shape[2] // 2
    tn = min(512, dc)
    nj = dc // tn
    return pl.pallas_call(
        _odd_in_kernel,
        out_shape=jax.ShapeDtypeStruct((t, dc), F32),
        grid=(nj, t // tm),
        in_specs=[
            pl.BlockSpec((tm, d), lambda j, i: (i, 0)),
            pl.BlockSpec((1, d, tn), lambda j, i: (li, 0, j)),
            pl.BlockSpec((1, d, tn), lambda j, i: (li, 0, nj + j)),
            pl.BlockSpec((1, 1, tn), lambda j, i: (li, 0, j)),
            pl.BlockSpec((1, 1, tn), lambda j, i: (li, 0, nj + j)),
        ],
        out_specs=pl.BlockSpec((tm, tn), lambda j, i: (i, j)),
        scratch_shapes=[pltpu.VMEM((d, tn), BF16), pltpu.VMEM((d, tn), BF16)],
        compiler_params=_params("arbitrary", "arbitrary"),
        name="odd_in",
    )(hb, w_in, w_in, b_in, b_in)


def _dwconv_kernel(u_ref, halo_ref, w_ref, b_ref, z_ref, ext_ref, *, stride, hr, tps, zero_start):
    tm = u_ref.shape[0]
    keep = jnp.logical_not(jnp.logical_and(zero_start, pl.program_id(0) % tps == 0)).astype(F32)
    ext_ref[pl.ds(0, hr), :] = halo_ref[...] * keep
    ext_ref[pl.ds(hr, tm), :] = u_ref[...]
    w = w_ref[0]
    b = b_ref[0]
    rb = min(32, tm)

    def chunk(ci, carry):
        r0 = pl.multiple_of(ci * rb, rb)
        z_ref[pl.ds(r0, rb), :] = _causal_taps(ext_ref, r0, rb, slice(None), w, CONV_C, stride, hr) + b
        return carry

    lax.fori_loop(0, tm // rb, chunk, 0)


def _dwconv(u, halo, dw_w, dw_b, li, row0, ntiles, tm, stride, hr, tps, zero_start):
    dc = u.shape[1]
    cb = min(256, dc)
    if halo is None:
        hb = tm // hr
        harg = u
        hspec = pl.BlockSpec((hr, cb), lambda i, c: (jnp.maximum((row0 + i) * hb - 1, 0), c))
    else:
        harg = halo
        hspec = pl.BlockSpec((hr, cb), lambda i, c: (0, c))
    return pl.pallas_call(
        functools.partial(_dwconv_kernel, stride=stride, hr=hr, tps=tps, zero_start=zero_start),
        out_shape=jax.ShapeDtypeStruct((ntiles * tm, dc), F32),
        grid=(ntiles, dc // cb),
        in_specs=[
            pl.BlockSpec((tm, cb), lambda i, c: (row0 + i, c)),
            hspec,
            pl.BlockSpec((1, CONV_C, cb), lambda i, c: (li, 0, c)),
            pl.BlockSpec((1, 1, cb), lambda i, c: (li, 0, c)),
        ],
        out_specs=pl.BlockSpec((tm, cb), lambda i, c: (i, c)),
        scratch_shapes=[pltpu.VMEM((hr + tm, cb), F32)],
        compiler_params=_params("parallel", "parallel"),
        name="dwconv",
    )(u, harg, dw_w, dw_b)


def _odd_out_kernel(z_ref, lg_ref, lb_ref, mpg, msg, w_ref, b_ref, x_ref, xo_ref, zs_ref, *, ntp, reps):
    i = pl.program_id(0)

    @pl.when(pl.program_id(1) == 0)
    def _():
        z = z_ref[...]
        zc = z - jnp.mean(z, axis=-1, keepdims=True)
        y = zc * lax.rsqrt(jnp.mean(zc * zc, axis=-1, keepdims=True) + EPS)
        zs_ref[...] = _silu(y * lg_ref[0] + lb_ref[0]).astype(BF16)

    y = jnp.dot(zs_ref[...], w_ref[0].astype(BF16), preferred_element_type=F32) + b_ref[0]
    xo_ref[...] = x_ref[...] + _tile_mod(mpg, msg, i >= ntp, reps) * y


def _odd_out(z, ln_g, ln_b, mp, ms, w_out, b_out, li, x, dims):
    t, d = x.shape
    tm, ntp, tps, bp, bs, reps = dims
    dc = z.shape[1]
    tn = min(1024, d)
    nj = d // tn
    mpg = pl.BlockSpec((1, 1, 1, tn), lambda i, j: (2, jnp.minimum(i // tps, bp - 1), 0, j))
    msg = pl.BlockSpec((1, bs, tn), lambda i, j: (2, 0, j))
    return pl.pallas_call(
        functools.partial(_odd_out_kernel, ntp=ntp, reps=reps),
        out_shape=jax.ShapeDtypeStruct((t, d), F32),
        grid=(t // tm, nj),
        in_specs=[
            pl.BlockSpec((tm, dc), lambda i, j: (i, 0)),
            pl.BlockSpec((1, 1, dc), lambda i, j: (li, 0, 0)),
            pl.BlockSpec((1, 1, dc), lambda i, j: (li, 0, 0)),
            mpg, msg,
            pl.BlockSpec((1, dc, tn), lambda i, j: (li, 0, j)),
            pl.BlockSpec((1, 1, tn), lambda i, j: (li, 0, j)),
            pl.BlockSpec((tm, tn), lambda i, j: (i, j)),
        ],
        out_specs=pl.BlockSpec((tm, tn), lambda i, j: (i, j)),
        scratch_shapes=[pltpu.VMEM((tm, dc), BF16)],
        compiler_params=_params("parallel", "arbitrary"),
        name="odd_out",
    )(z, ln_g, ln_b, mp, ms, w_out, b_out, x)


def _ffnpre_kernel(x_ref, g_ref, mpsh, mpsc, mssh, mssc, wr_ref, br_ref,
                   h_ref, idx_ref, gate_ref, rank_ref, cnt_ref, carry_ref, *, ntp, reps):
    i = pl.program_id(0)
    tm = x_ref.shape[0]
    is_s = i >= ntp
    h = _rms(x_ref[...]) * g_ref[0] * (1 + _tile_mod(mpsc, mssc, is_s, reps)) + _tile_mod(mpsh, mssh, is_s, reps)
    h_ref[...] = _pack_bf16_pairs(h)
    logits = _mm_hi(h, wr_ref[0]) + br_ref[0]

    @pl.when(i == 0)
    def _():
        carry_ref[...] = jnp.zeros(carry_ref.shape, F32)

    lane = lax.broadcasted_iota(I32, logits.shape, 1)
    work = logits
    sels, tops, picks = [], [], []
    for _ in range(TOP_K):
        m = jnp.max(work, axis=-1, keepdims=True)
        pick = jnp.min(jnp.where(work == m, lane, V7X_LANES), axis=-1, keepdims=True)
        sel = lane == pick
        work = jnp.where(sel, -jnp.inf, work)
        sels.append(sel)
        tops.append(m)
        picks.append(pick)
    exps = [jnp.exp(m - tops[0]) for m in tops]
    denom = exps[0]
    for e in exps[1:]:
        denom = denom + e

    onehot = sels[0]
    for s in sels[1:]:
        onehot = jnp.logical_or(onehot, s)
    onehot = onehot.astype(F32)
    r = lax.broadcasted_iota(I32, (tm, tm), 0)
    c = lax.broadcasted_iota(I32, (tm, tm), 1)
    before = _mm((r > c).astype(F32), onehot) + carry_ref[...]
    carry_ref[...] = carry_ref[...] + jnp.sum(onehot, axis=0, keepdims=True)

    idx_out = jnp.zeros(logits.shape, I32)
    gate_out = jnp.zeros(logits.shape, F32)
    rank_out = jnp.zeros(logits.shape, I32)
    for kk in range(TOP_K):
        rank = jnp.sum(jnp.where(sels[kk], before, 0.0), axis=-1, keepdims=True).astype(I32)
        idx_out = jnp.where(lane == kk, picks[kk], idx_out)
        gate_out = jnp.where(lane == kk, exps[kk] / denom, gate_out)
        rank_out = jnp.where(lane == kk, rank, rank_out)
    idx_ref[...] = idx_out
    gate_ref[...] = gate_out
    rank_ref[...] = rank_out
    cnt_ref[...] = carry_ref[...].astype(I32)


def _ffnpre(x, g, mp, ms, w_router, b_router, li, dims):
    t, d = x.shape
    tm, ntp, tps, bp, bs, reps = dims
    sh = _mod_specs(3, tps, bp, bs, d, 1)
    sc = _mod_specs(4, tps, bp, bs, d, 1)
    lanes = jax.ShapeDtypeStruct((t, V7X_LANES), I32)
    rowspec = pl.BlockSpec((tm, V7X_LANES), lambda i: (i, 0))
    return pl.pallas_call(
        functools.partial(_ffnpre_kernel, ntp=ntp, reps=reps),
        out_shape=(jax.ShapeDtypeStruct((t, d // 2), U32), lanes, jax.ShapeDtypeStruct((t, V7X_LANES), F32), lanes,
                   jax.ShapeDtypeStruct((1, V7X_LANES), I32)),
        grid=(t // tm,),
        in_specs=[
            pl.BlockSpec((tm, d), lambda i: (i, 0)),
            pl.BlockSpec((1, 1, d), lambda i: (li, 0, 0)),
            sh[0], sc[0], sh[1], sc[1],
            pl.BlockSpec((1, d, V7X_LANES), lambda i: (li, 0, 0)),
            pl.BlockSpec((1, 1, V7X_LANES), lambda i: (li, 0, 0)),
        ],
        out_specs=(pl.BlockSpec((tm, d // 2), lambda i: (i, 0)), rowspec, rowspec, rowspec,
                   pl.BlockSpec((1, V7X_LANES), lambda i: (0, 0))),
        scratch_shapes=[pltpu.VMEM((1, V7X_LANES), F32)],
        compiler_params=_params("arbitrary"),
        name="ffn_pre_router",
    )(x, g, mp, mp, ms, ms, w_router, b_router)


def _dispatch_kernel(slot_ref, zrow_ref, h_ref, xs_ref, zero_ref, sem, *, nexp, bm):
    i = pl.program_id(0)
    tm = h_ref.shape[0]

    @pl.when(i == 0)
    def _():
        zero_ref[...] = jnp.zeros(zero_ref.shape, zero_ref.dtype)

        def zcopy(r):
            return pltpu.make_async_copy(zero_ref, xs_ref.at[pl.ds(r, 1)], sem)

        def zfill(e, carry):
            base = zrow_ref[e]

            def issue(j, cc):
                for jj in range(ROW_UNROLL):
                    zcopy(base + j * ROW_UNROLL + jj).start()
                return cc

            lax.fori_loop(0, bm // ROW_UNROLL, issue, 0)

            def drain(j, cc):
                for jj in range(ROW_UNROLL):
                    zcopy(base + j * ROW_UNROLL + jj).wait()
                return cc

            lax.fori_loop(0, bm // ROW_UNROLL, drain, 0)
            return carry

        lax.fori_loop(0, nexp, zfill, 0)

    def rcopy(t, kk):
        return pltpu.make_async_copy(h_ref.at[pl.ds(t, 1)],
                                     xs_ref.at[pl.ds(slot_ref[(i * tm + t) * TOP_K + kk], 1)], sem)

    def group(g, fn):
        for tt in range(ROW_UNROLL):
            for kk in range(TOP_K):
                fn(rcopy(g * ROW_UNROLL + tt, kk))

    ngroups = tm // ROW_UNROLL

    def body(g, carry):
        group(g, lambda c: c.start())

        @pl.when(g > 0)
        def _():
            group(g - 1, lambda c: c.wait())

        return carry

    lax.fori_loop(0, ngroups, body, 0)
    group(ngroups - 1, lambda c: c.wait())


def _dispatch(slots, zrows, h, rows, bm, tm):
    t, d = h.shape
    nexp = zrows.shape[0]
    assert tm % ROW_UNROLL == 0 and bm % ROW_UNROLL == 0
    return pl.pallas_call(
        functools.partial(_dispatch_kernel, nexp=nexp, bm=bm),
        out_shape=jax.ShapeDtypeStruct((rows, d), h.dtype),
        grid_spec=pltpu.PrefetchScalarGridSpec(
            num_scalar_prefetch=2,
            grid=(t // tm,),
            in_specs=[pl.BlockSpec((tm, d), lambda i, sl, zr: (i, 0))],
            out_specs=pl.BlockSpec(memory_space=pl.ANY),
            scratch_shapes=[pltpu.VMEM((1, d), h.dtype), pltpu.SemaphoreType.DMA(())],
        ),
        compiler_params=_params("arbitrary"),
        name="moe_dispatch",
    )(slots, zrows, h)


def _weight_stream(be_ref, grp_ref, ge_ref, meta_ref, copies, install):
    j, i = pl.program_id(0), pl.program_id(1)
    nj = pl.num_programs(0)
    nv, ng = meta_ref[0], meta_ref[1]
    valid = i < nv
    first = jnp.logical_and(valid, jnp.logical_or(i == 0, be_ref[i] != be_ref[jnp.maximum(i - 1, 0)]))

    @pl.when(first)
    def _():
        grp = grp_ref[i]

        @pl.when(jnp.logical_and(j == 0, grp == 0))
        def _():
            for c in copies(j, be_ref[i]):
                c.start()

        for c in copies(j, be_ref[i]):
            c.wait()
        install()
        wrap = grp + 1 == ng
        nxt_j = jnp.where(wrap, j + 1, j)
        nxt_e = ge_ref[jnp.where(wrap, 0, grp + 1)]

        @pl.when(nxt_j < nj)
        def _():
            for c in copies(nxt_j, nxt_e):
                c.start()

    return valid


def _expert_up_kernel(be_ref, grp_ref, ge_ref, meta_ref, x_ref, w_ref, bg_ref, bu_ref, act_ref,
                      sg_ref, su_ref, wgb_ref, wub_ref, sem, *, li, f):
    tn = sg_ref.shape[1]

    def copies(j, e):
        col = pl.multiple_of(j * tn, tn)
        return (pltpu.make_async_copy(w_ref.at[li, e, :, pl.ds(col, tn)], sg_ref, sem.at[0]),
                pltpu.make_async_copy(w_ref.at[li, e, :, pl.ds(f + col, tn)], su_ref, sem.at[1]))

    def install():
        wgb_ref[...] = sg_ref[...].astype(BF16)
        wub_ref[...] = su_ref[...].astype(BF16)

    valid = _weight_stream(be_ref, grp_ref, ge_ref, meta_ref, copies, install)

    @pl.when(valid)
    def _():
        xb = _unpack_bf16_pairs(x_ref[...])
        g = jnp.dot(xb, wgb_ref[...], preferred_element_type=F32) + bg_ref[0, 0]
        u = jnp.dot(xb, wub_ref[...], preferred_element_type=F32) + bu_ref[0, 0]
        gate = jnp.minimum(g, SWIGLU_LIMIT)
        up = jnp.clip(u, -SWIGLU_LIMIT, SWIGLU_LIMIT)
        act_ref[...] = ((up + 1) * gate * jax.nn.sigmoid(SWIGLU_ALPHA * gate)).astype(BF16)


def _expert_up(route, xs, w_gate_up, b_gate_up, li, bm):
    rows, dh = xs.shape
    d = 2 * dh
    nblk = rows // bm
    f = w_gate_up.shape[3] // 2
    tn = min(EXPERT_COLS, f)
    nj = f // tn
    blk = lambda i, meta: jnp.minimum(i, meta[0] - 1)
    return pl.pallas_call(
        functools.partial(_expert_up_kernel, li=li, f=f),
        out_shape=jax.ShapeDtypeStruct((rows, f), BF16),
        grid_spec=pltpu.PrefetchScalarGridSpec(
            num_scalar_prefetch=4,
            grid=(nj, nblk),
            in_specs=[
                pl.BlockSpec((bm, dh), lambda j, i, be, grp, ge, meta: (blk(i, meta), 0)),
                pl.BlockSpec(memory_space=pl.ANY),
                pl.BlockSpec((1, 1, 1, tn), lambda j, i, be, grp, ge, meta: (li, be[blk(i, meta)], 0, j)),
                pl.BlockSpec((1, 1, 1, tn), lambda j, i, be, grp, ge, meta: (li, be[blk(i, meta)], 0, nj + j)),
            ],
            out_specs=pl.BlockSpec((bm, tn), lambda j, i, be, grp, ge, meta: (blk(i, meta), j)),
            scratch_shapes=[pltpu.VMEM((d, tn), F32), pltpu.VMEM((d, tn), F32),
                            pltpu.VMEM((d, tn), BF16), pltpu.VMEM((d, tn), BF16),
                            pltpu.SemaphoreType.DMA((2,))],
        ),
        compiler_params=_params("arbitrary", "arbitrary"),
        name="expert_gate_up",
    )(*route, xs, w_gate_up, b_gate_up, b_gate_up)


def _expert_down_kernel(be_ref, grp_ref, ge_ref, meta_ref, a_ref, w_ref, b_ref, y_ref, st_ref, wb_ref, sem, *, li):
    tn = st_ref.shape[1]

    def copies(j, e):
        col = pl.multiple_of(j * tn, tn)
        return (pltpu.make_async_copy(w_ref.at[li, e, :, pl.ds(col, tn)], st_ref, sem.at[0]),)

    def install():
        wb_ref[...] = st_ref[...].astype(BF16)

    valid = _weight_stream(be_ref, grp_ref, ge_ref, meta_ref, copies, install)

    @pl.when(valid)
    def _():
        y_ref[...] = jnp.dot(a_ref[...], wb_ref[...], preferred_element_type=F32) + b_ref[0, 0]


def _expert_down(route, act, w_down, b_down, li, bm):
    rows, f = act.shape
    nblk = rows // bm
    d = w_down.shape[3]
    tn = min(EXPERT_COLS, d)
    nj = d // tn
    blk = lambda i, meta: jnp.minimum(i, meta[0] - 1)
    return pl.pallas_call(
        functools.partial(_expert_down_kernel, li=li),
        out_shape=jax.ShapeDtypeStruct((rows, d), F32),
        grid_spec=pltpu.PrefetchScalarGridSpec(
            num_scalar_prefetch=4,
            grid=(nj, nblk),
            in_specs=[
                pl.BlockSpec((bm, f), lambda j, i, be, grp, ge, meta: (blk(i, meta), 0)),
                pl.BlockSpec(memory_space=pl.ANY),
                pl.BlockSpec((1, 1, 1, tn), lambda j, i, be, grp, ge, meta: (li, be[blk(i, meta)], 0, j)),
            ],
            out_specs=pl.BlockSpec((bm, tn), lambda j, i, be, grp, ge, meta: (blk(i, meta), j)),
            scratch_shapes=[pltpu.VMEM((f, tn), F32), pltpu.VMEM((f, tn), BF16), pltpu.SemaphoreType.DMA((1,))],
        ),
        compiler_params=_params("arbitrary", "arbitrary"),
        name="expert_down",
    )(*route, act, w_down, b_down)


def _combine_kernel(slot_ref, gate_ref, mpg, msg, x_ref, ys_ref, xo_ref, buf_ref, sem, *, ntp, reps):
    i = pl.program_id(0)
    tm = x_ref.shape[0]

    def rcopy(t, kk):
        return pltpu.make_async_copy(ys_ref.at[pl.ds(slot_ref[(i * tm + t) * TOP_K + kk], 1)],
                                     buf_ref.at[kk, pl.ds(t, 1)], sem)

    def issue(g, carry):
        for tt in range(ROW_UNROLL):
            for kk in range(TOP_K):
                rcopy(g * ROW_UNROLL + tt, kk).start()
        return carry

    lax.fori_loop(0, tm // ROW_UNROLL, issue, 0)

    def drain(g, carry):
        for tt in range(ROW_UNROLL):
            for kk in range(TOP_K):
                rcopy(g * ROW_UNROLL + tt, kk).wait()
        return carry

    lax.fori_loop(0, tm // ROW_UNROLL, drain, 0)

    gates = gate_ref[...]
    y = buf_ref[0] * gates[:, 0:1]
    for kk in range(1, TOP_K):
        y = y + buf_ref[kk] * gates[:, kk:kk + 1]
    xo_ref[...] = x_ref[...] + _tile_mod(mpg, msg, i >= ntp, reps) * y


def _combine(slots, gates, mp, ms, x, ys, dims):
    t, d = x.shape
    tm, ntp, tps, bp, bs, reps = dims
    mpg = pl.BlockSpec((1, 1, 1, d), lambda i, sl: (5, jnp.minimum(i // tps, bp - 1), 0, 0))
    msg = pl.BlockSpec((1, bs, d), lambda i, sl: (5, 0, 0))
    return pl.pallas_call(
        functools.partial(_combine_kernel, ntp=ntp, reps=reps),
        out_shape=jax.ShapeDtypeStruct((t, d), F32),
        grid_spec=pltpu.PrefetchScalarGridSpec(
            num_scalar_prefetch=1,
            grid=(t // tm,),
            in_specs=[
                pl.BlockSpec((tm, V7X_LANES), lambda i, sl: (i, 0)),
                mpg, msg,
                pl.BlockSpec((tm, d), lambda i, sl: (i, 0)),
                pl.BlockSpec(memory_space=pl.ANY),
            ],
            out_specs=pl.BlockSpec((tm, d), lambda i, sl: (i, 0)),
            scratch_shapes=[pltpu.VMEM((TOP_K, tm, d), F32), pltpu.SemaphoreType.DMA(())],
        ),
        compiler_params=_params("arbitrary"),
        name="moe_combine",
    )(slots, gates, mp, ms, x, ys)


def _moe(x, li, g, mp, ms, w_router_p, b_router_p, w_gate_up, b_gate_up4, w_down, b_down4, dims):
    t, d = x.shape
    nexp = w_gate_up.shape[1]
    bm = MOE_ROWS
    h, idx, gates, rank, cnt = _ffnpre(x, g, mp, ms, w_router_p, b_router_p, li, dims)
    counts = cnt[0, :nexp]
    padded = (counts + bm - 1) // bm * bm
    pad_end = jnp.cumsum(padded)
    pad_start = pad_end - padded
    idx4 = idx[:, :TOP_K]
    start4 = jnp.sum(jnp.where(idx4[..., None] == jnp.arange(nexp, dtype=I32), pad_start, 0), axis=-1)
    slots = (start4 + rank[:, :TOP_K]).reshape(t * TOP_K).astype(I32)
    nblk = -(-(t * TOP_K + nexp * (bm - 1)) // bm) + 1
    rows = nblk * bm
    blk_start = jnp.arange(nblk, dtype=I32) * bm
    blk_expert = jnp.minimum(jnp.sum(pad_end[None, :] <= blk_start[:, None], axis=1), nexp - 1).astype(I32)
    has = counts > 0
    run_of_expert = jnp.cumsum(has.astype(I32)) - 1
    blk_run = jnp.sum(jnp.where(blk_expert[:, None] == jnp.arange(nexp, dtype=I32), run_of_expert, 0), axis=1)
    run_expert = jnp.argsort(jnp.logical_not(has), stable=True).astype(I32)
    meta = jnp.stack([pad_end[-1] // bm, jnp.sum(has.astype(I32))]).astype(I32)
    route = (blk_expert, blk_run.astype(I32), run_expert, meta)
    zrows = (pad_start + counts).astype(I32)
    xs = _dispatch(slots, zrows, h, rows, bm, dims[0])
    act = _expert_up(route, xs, w_gate_up, b_gate_up4, li, bm)
    ys = _expert_down(route, act, w_down, b_down4, li, bm)
    return _combine(slots, gates, mp, ms, x, ys, dims)


def _final_kernel(x_ref, g_ref, o_ref):
    o_ref[...] = _rms(x_ref[...]) * g_ref[...]


def _final_norm(x, g, tm):
    t, d = x.shape
    return pl.pallas_call(
        _final_kernel,
        out_shape=jax.ShapeDtypeStruct((t, d), F32),
        grid=(t // tm,),
        in_specs=[pl.BlockSpec((tm, d), lambda i: (i, 0)), pl.BlockSpec((1, d), lambda i: (0, 0))],
        out_specs=pl.BlockSpec((tm, d), lambda i: (i, 0)),
        compiler_params=_params("parallel"),
        name="final_norm",
    )(x, g)


def _to_pos_major(a):
    b, l, c = a.shape
    return jnp.swapaxes(a, 0, 1).reshape(l * b, c)


def _from_pos_major(a, b):
    lb, c = a.shape
    return jnp.swapaxes(a.reshape(lb // b, b, c), 0, 1)


def kernel(x_prompt, x_sample, state_delta, state_conv_qkv, state_conv_b, state_conv_c, c_prompt, c_sample, norm_mix, norm_ffn, w_mod, b_mod, w_in_even, conv_qkv_w, a_log, dt_bias, o_norm_g, conv_b_w, w_out_even, w_in_odd, b_in_odd, dw_w, dw_b, ln_g, ln_b, w_out_odd, b_out_odd, w_router, b_router, w_gate_up, b_gate_up, w_down, b_down, norm_final):
    bp, seq, d = x_prompt.shape
    bs, dseq, _ = x_sample.shape
    depth = w_mod.shape[0]
    nh = a_log.shape[1]
    da = nh * DV
    db = conv_b_w.shape[2]
    dc = dw_w.shape[2]
    nexp = w_router.shape[2]
    tp, ts = bp * seq, bs * dseq
    tm = ts
    assert seq % tm == 0 and bs % V7X_SUBLANES == 0 and da == db and dseq >= CONV_A - 1
    assert nexp <= V7X_LANES and 2 * nh <= V7X_LANES
    tps = seq // tm
    ntp = tp // tm
    dims = (tm, ntp, tps, bp, bs, dseq)
    chunk = math.gcd(seq, CHUNK_A)

    x = jnp.concatenate([x_prompt.reshape(tp, d), _to_pos_major(x_sample)], axis=0)
    rc = -(-(bp + bs) // V7X_SUBLANES) * V7X_SUBLANES
    c_all = jnp.zeros((rc, d), F32).at[:bp].set(c_prompt).at[bp:bp + bs].set(c_sample)
    mod = _modulation(c_all, w_mod, b_mod)

    w_router_p = jnp.zeros((depth, d, V7X_LANES), F32).at[:, :, :nexp].set(w_router)
    b_router_p = jnp.full((depth, 1, V7X_LANES), -jnp.inf, F32).at[:, 0, :nexp].set(b_router)
    b_gate_up4 = b_gate_up.reshape(depth, nexp, 1, b_gate_up.shape[2])
    b_down4 = b_down.reshape(depth, nexp, 1, d)

    qkv_w = 3 * nh * DK
    c_ab = qkv_w + da
    c_b = c_ab + 2 * nh
    hr_qkv = (CONV_A - 1) * bs
    hr_c = (CONV_C - 1) * bs
    outs = {k: [] for k in ("dp", "qp", "bp", "cp", "ds", "qs", "bs", "cs")}

    for l in range(depth):
        mp = mod[l][:, :bp].reshape(6, bp, 1, d)
        ms = mod[l][:, bp:bp + bs]
        li = l // 2
        if l % 2 == 0:
            w = w_in_even[li]
            w_main = jnp.concatenate([w[:, :c_ab], w[:, c_b:]], axis=1)
            w_ab = jnp.zeros((d, V7X_LANES), F32).at[:, :2 * nh].set(w[:, c_ab:c_b])
            hb, ab = _prenorm(x, norm_mix[l:l + 1], mp, ms, w_ab, dims)
            proj = _even_in(hb, w_main, tm)
            adt = jnp.zeros((V7X_SUBLANES, V7X_LANES), F32).at[0, :nh].set(a_log[li]).at[1, :nh].set(dt_bias[li])
            wqkv, wb = conv_qkv_w[li], conv_b_w[li]
            qp, kp, vp, gbp, ybp, utp = _evenprep(proj, ab, None, wqkv, wb, adt, 0, ntp, tm, 1, V7X_SUBLANES,
                                                  tps, True, nh)
            sq = _to_pos_major(state_conv_qkv[li])
            su = jnp.concatenate([jnp.zeros(((CONV_A - CONV_B) * bs, db), F32), _to_pos_major(state_conv_b[li])], axis=0)
            halos = (sq[:, :da], sq[:, da:2 * da], sq[:, 2 * da:], su, jnp.ones_like(su))
            qs, ks, vs, gbs, ybs, uts = _evenprep(proj, ab, halos, wqkv, wb, adt, ntp, 1, tm, bs, hr_qkv,
                                                  1, False, nh)
            o_p, s_p = _delta(qp, kp, vp, gbp, None, li, bp, seq // chunk, chunk, nh, math.gcd(bp, DELTA_SEQS_PROMPT))

            qkvg = _from_pos_major(jnp.concatenate([qs, ks, vs, gbs], axis=1), bs)
            qkvg = jnp.pad(qkvg, ((0, 0), (0, V7X_SUBLANES - dseq), (0, 0))).reshape(bs * V7X_SUBLANES, -1)
            o_s8, s_s = _delta(qkvg, None, None, None, state_delta, li,
                               bs, 1, V7X_SUBLANES, nh, math.gcd(bs, DELTA_SEQS_SAMPLE))
            o_s = _to_pos_major(o_s8.reshape(bs, V7X_SUBLANES, da)[:, :dseq])
            o = jnp.concatenate([o_p, o_s], axis=0)
            yb = jnp.concatenate([ybp, ybs], axis=0)
            x = _even_out(o, proj, yb, o_norm_g[li:li + 1], mp, ms, w_out_even, li, x, dims, nh)

            outs["dp"].append(s_p)
            outs["qp"].append(proj[:tp].reshape(bp, seq, -1)[:, seq - (CONV_A - 1):, :qkv_w])
            outs["bp"].append(utp.reshape(bp, tps, V7X_SUBLANES, db)[:, tps - 1, V7X_SUBLANES - (CONV_B - 1):])
            outs["ds"].append(s_s)
            outs["qs"].append(_from_pos_major(proj[tp + (dseq - (CONV_A - 1)) * bs:, :qkv_w], bs))
            u_ext = jnp.concatenate([state_conv_b[li], _from_pos_major(uts, bs)], axis=1)
            outs["bs"].append(u_ext[:, u_ext.shape[1] - (CONV_B - 1):])
        else:
            (hb,) = _prenorm(x, norm_mix[l:l + 1], mp, ms, None, dims)
            u = _odd_in(hb, w_in_odd, b_in_odd.reshape(-1, 1, 2 * dc), li, tm)
            dwb = dw_b.reshape(-1, 1, dc)
            zp = _dwconv(u, None, dw_w, dwb, li, 0, ntp, tm, 1, 32, tps, True)
            zs = _dwconv(u, _to_pos_major(state_conv_c[li]), dw_w, dwb, li, ntp, 1, tm, bs, hr_c, 1, False)
            z = jnp.concatenate([zp, zs], axis=0)
            x = _odd_out(z, ln_g.reshape(-1, 1, dc), ln_b.reshape(-1, 1, dc), mp, ms, w_out_odd,
                         b_out_odd.reshape(-1, 1, d), li, x, dims)
            outs["cp"].append(u[:tp].reshape(bp, seq, dc)[:, seq - (CONV_C - 1):])
            c_ext = jnp.concatenate([state_conv_c[li], _from_pos_major(u[tp:], bs)], axis=1)
            outs["cs"].append(c_ext[:, c_ext.shape[1] - (CONV_C - 1):])
        x = _moe(x, l, norm_ffn.reshape(depth, 1, d), mp, ms, w_router_p, b_router_p, w_gate_up, b_gate_up4,
                 w_down, b_down4, dims)

    y = _final_norm(x, norm_final.reshape(1, d), tm)
    y_prompt = y[:tp].reshape(bp, seq, d)
    y_sample = _from_pos_major(y[tp:], bs)
    return (y_prompt, y_sample,
            jnp.stack(outs["dp"]), jnp.stack(outs["qp"]), jnp.stack(outs["bp"]), jnp.stack(outs["cp"]),
            jnp.stack(outs["ds"]), jnp.stack(outs["qs"]), jnp.stack(outs["bs"]), jnp.stack(outs["cs"]))
```

```python
import functools
import math

import jax
import jax.numpy as jnp
from jax import lax
from jax.experimental import pallas as pl
from jax.experimental.pallas import tpu as pltpu

F32 = jnp.float32
BF16 = jnp.bfloat16
I32 = jnp.int32
U32 = jnp.uint32
EPS = 1e-6
HI = lax.Precision.HIGHEST

DK = 128
DV = 128
CONV_A = 4
CONV_B = 3
CONV_C = 31
CHUNK_A = 64
TOP_K = 4
SWIGLU_ALPHA = 1.702
SWIGLU_LIMIT = 7.0

V7X_LANES = 128
V7X_SUBLANES = 8
V7X_VMEM_LIMIT_BYTES = 56 * 1024 * 1024
MOE_ROWS = 256
EXPERT_COLS = 1024
EXPERT_DOWN_COLS = 2048
DELTA_SEQS_PROMPT = 2
DELTA_SEQS_SAMPLE = 4
ROW_UNROLL = 8


def _params(*sem):
    return pltpu.CompilerParams(dimension_semantics=sem, vmem_limit_bytes=V7X_VMEM_LIMIT_BYTES)


def _mm(a, b):
    return jnp.dot(a.astype(BF16), b.astype(BF16), preferred_element_type=F32)


def _mm_hi(a, b):
    return jnp.dot(a, b, precision=HI, preferred_element_type=F32)


_NN = (((1,), (0,)), ((), ()))
_NT = (((1,), (1,)), ((), ()))
_TN = (((0,), (0,)), ((), ()))


def _dot(a, b, dims):
    return lax.dot_general(a, b, dims, preferred_element_type=F32)


def _split2(a):
    hi = a.astype(BF16)
    return hi, (a - hi.astype(F32)).astype(BF16)


def _mm3s(a, b):
    return _dot(a[0], b[0], _NN) + (_dot(a[0], b[1], _NN) + _dot(a[1], b[0], _NN))


def _pack_bf16_pairs(x):
    half = x.shape[1] // 2
    bits = lax.bitcast_convert_type(x.astype(BF16).astype(F32), U32)
    return (bits[:, half:] & jnp.uint32(0xFFFF0000)) | (bits[:, :half] >> 16)


def _unpack_bf16_pairs(w):
    lo = lax.bitcast_convert_type(w << 16, F32)
    hi = lax.bitcast_convert_type(w & jnp.uint32(0xFFFF0000), F32)
    return jnp.concatenate([lo, hi], axis=1).astype(BF16)


def _silu(x):
    return x * jax.nn.sigmoid(x)


def _rms(x):
    return x * lax.rsqrt(jnp.mean(x * x, axis=-1, keepdims=True) + EPS)


def _tile_mod(mp_ref, ms_ref, is_sample, reps):
    s = ms_ref[0]
    s = jnp.concatenate([s] * reps, axis=0)
    return jnp.where(is_sample, s, mp_ref[0, 0])


def _mod_specs(k, tps, bp, bs, d, ngrid):
    if ngrid == 1:
        mp = pl.BlockSpec((1, 1, 1, d), lambda i: (k, jnp.minimum(i // tps, bp - 1), 0, 0))
        ms = pl.BlockSpec((1, bs, d), lambda i: (k, 0, 0))
    else:
        mp = pl.BlockSpec((1, 1, 1, d), lambda i, j: (k, jnp.minimum(i // tps, bp - 1), 0, 0))
        ms = pl.BlockSpec((1, bs, d), lambda i, j: (k, 0, 0))
    return mp, ms


def _mod_kernel(c_ref, w_ref, b_ref, o_ref):
    o_ref[0, 0] = _mm(_silu(c_ref[...]), w_ref[0]) + b_ref[0]


def _modulation(c_all, w_mod, b_mod):
    depth, d, d6 = w_mod.shape
    rc = c_all.shape[0]
    tn = min(512, d)
    nj = d // tn
    return pl.pallas_call(
        _mod_kernel,
        out_shape=jax.ShapeDtypeStruct((depth, 6, rc, d), F32),
        grid=(depth, 6, nj),
        in_specs=[
            pl.BlockSpec((rc, d), lambda l, k, j: (0, 0)),
            pl.BlockSpec((1, d, tn), lambda l, k, j: (l, 0, k * nj + j)),
            pl.BlockSpec((1, 1, tn), lambda l, k, j: (l, 0, k * nj + j)),
        ],
        out_specs=pl.BlockSpec((1, 1, rc, tn), lambda l, k, j: (l, k, 0, j)),
        compiler_params=_params("parallel", "parallel", "parallel"),
        name="modulation",
    )(c_all, w_mod, b_mod.reshape(depth, 1, d6))


def _prenorm_kernel(*refs, ntp, reps, has_ab):
    if has_ab:
        x_ref, g_ref, mpsh, mpsc, mssh, mssc, wab_ref, hb_ref, ab_ref = refs
    else:
        x_ref, g_ref, mpsh, mpsc, mssh, mssc, hb_ref = refs
    is_s = pl.program_id(0) >= ntp
    h = _rms(x_ref[...]) * g_ref[...] * (1 + _tile_mod(mpsc, mssc, is_s, reps)) + _tile_mod(mpsh, mssh, is_s, reps)
    hb = h.astype(BF16)
    hb_ref[...] = hb
    if has_ab:
        ab_ref[...] = jnp.dot(hb, wab_ref[...].astype(BF16), preferred_element_type=F32)


def _prenorm(x, g, mp, ms, w_ab, dims):
    t, d = x.shape
    tm, ntp, tps, bp, bs, reps = dims
    sh = _mod_specs(0, tps, bp, bs, d, 1)
    sc = _mod_specs(1, tps, bp, bs, d, 1)
    in_specs = [pl.BlockSpec((tm, d), lambda i: (i, 0)), pl.BlockSpec((1, d), lambda i: (0, 0)),
                sh[0], sc[0], sh[1], sc[1]]
    args = [x, g, mp, mp, ms, ms]
    out_shape = [jax.ShapeDtypeStruct((t, d), BF16)]
    out_specs = [pl.BlockSpec((tm, d), lambda i: (i, 0))]
    if w_ab is not None:
        in_specs.append(pl.BlockSpec((d, V7X_LANES), lambda i: (0, 0)))
        args.append(w_ab)
        out_shape.append(jax.ShapeDtypeStruct((t, V7X_LANES), F32))
        out_specs.append(pl.BlockSpec((tm, V7X_LANES), lambda i: (i, 0)))
    return pl.pallas_call(
        functools.partial(_prenorm_kernel, ntp=ntp, reps=reps, has_ab=w_ab is not None),
        out_shape=tuple(out_shape),
        grid=(t // tm,),
        in_specs=in_specs,
        out_specs=tuple(out_specs),
        compiler_params=_params("parallel"),
        name="prenorm",
    )(*args)


def _even_in_kernel(hb_ref, w_ref, proj_ref, wb_ref):
    @pl.when(pl.program_id(1) == 0)
    def _():
        wb_ref[...] = w_ref[...].astype(BF16)

    proj_ref[...] = jnp.dot(hb_ref[...], wb_ref[...], preferred_element_type=F32)


def _even_in(hb, w_main, tm):
    t, d = hb.shape
    nmain = w_main.shape[1]
    tn = nmain // 7
    return pl.pallas_call(
        _even_in_kernel,
        out_shape=jax.ShapeDtypeStruct((t, nmain), F32),
        grid=(7, t // tm),
        in_specs=[pl.BlockSpec((tm, d), lambda j, i: (i, 0)), pl.BlockSpec((d, tn), lambda j, i: (0, j))],
        out_specs=pl.BlockSpec((tm, tn), lambda j, i: (i, j)),
        scratch_shapes=[pltpu.VMEM((d, tn), BF16)],
        compiler_params=_params("arbitrary", "arbitrary"),
        name="even_in",
    )(hb, w_main)


def _causal_taps(ext_ref, r0, rb, lanes, w, width, stride, hr):
    acc = None
    if stride % V7X_SUBLANES == 0:
        for j in range(width):
            start = pl.multiple_of(r0 + (hr - (width - 1 - j) * stride), V7X_SUBLANES)
            term = ext_ref[pl.ds(start, rb), lanes] * w[j:j + 1, :]
            acc = term if acc is None else acc + term
        return acc
    look = -(-(width - 1) * stride // V7X_SUBLANES) * V7X_SUBLANES
    nrow = rb + look
    win = ext_ref[pl.ds(pl.multiple_of(r0 + (hr - look), V7X_SUBLANES), nrow), lanes]
    for res in range(V7X_SUBLANES):
        taps = [j for j in range(width) if (look - (width - 1 - j) * stride) % V7X_SUBLANES == res]
        if not taps:
            continue
        shifted = win if res == 0 else pltpu.roll(win, nrow - res, axis=0)
        for j in taps:
            off = look - (width - 1 - j) * stride - res
            term = shifted[off:off + rb, :] * w[j:j + 1, :]
            acc = term if acc is None else acc + term
    return acc


def _evenprep_kernel(q_ref, k_ref, v_ref, xb_ref, cp_ref, bpost_ref, ab_ref,
                     hq_ref, hk_ref, hv_ref, hxb_ref, hcp_ref,
                     wqkv_ref, wb_ref, adt_ref,
                     qo_ref, ko_ref, vo_ref, gb_ref, yb_ref, ut_ref,
                     extq, extk, extv, extu, *, stride, hr, tps, zero_start, nh):
    tm = q_ref.shape[0]
    da = q_ref.shape[1]
    keep = jnp.logical_not(jnp.logical_and(zero_start, pl.program_id(0) % tps == 0)).astype(F32)
    for ext, halo, cur in ((extq, hq_ref, q_ref), (extk, hk_ref, k_ref), (extv, hv_ref, v_ref)):
        ext[pl.ds(0, hr), :] = halo[...] * keep
        ext[pl.ds(hr, tm), :] = cur[...]
    extu[pl.ds(0, hr), :] = hcp_ref[...] * hxb_ref[...] * keep
    extu[pl.ds(hr, tm), :] = cp_ref[...] * xb_ref[...]
    tr = ut_ref.shape[0]
    ut_ref[...] = extu[pl.ds(hr + tm - tr, tr), :]

    ab = ab_ref[...]
    adt = adt_ref[...]
    z = ab + adt[1:2, :]
    softplus = jnp.maximum(z, 0.0) + jnp.log(1.0 + jnp.exp(-jnp.abs(z)))
    lane = lax.broadcasted_iota(I32, ab.shape, 1)
    gb_ref[...] = jnp.where(lane < nh, -jnp.exp(adt[0:1, :]) * softplus, jax.nn.sigmoid(ab))

    wqkv = wqkv_ref[...]
    wb = wb_ref[...]
    rb = min(64, tm)

    def chunk(ci, carry):
        r0 = pl.multiple_of(ci * rb, rb)
        rows = pl.ds(r0, rb)
        for h in range(nh):
            sl = slice(h * DK, (h + 1) * DK)
            taps = lambda ext, w0: _causal_taps(ext, r0, rb, sl, wqkv[:, w0 + h * DK:w0 + (h + 1) * DK],
                                                CONV_A, stride, hr)
            qh = _silu(taps(extq, 0))
            kh = _silu(taps(extk, da))
            qo_ref[rows, sl] = qh * lax.rsqrt(jnp.sum(qh * qh, axis=-1, keepdims=True) + EPS) * (DK ** -0.5)
            ko_ref[rows, sl] = kh * lax.rsqrt(jnp.sum(kh * kh, axis=-1, keepdims=True) + EPS)
            vo_ref[rows, sl] = _silu(taps(extv, 2 * da))
            yb = bpost_ref[rows, sl] * _causal_taps(extu, r0, rb, sl, wb[:, sl], CONV_B, stride, hr)
            yb_ref[rows, sl] = yb.astype(BF16)
        return carry

    lax.fori_loop(0, tm // rb, chunk, 0)


def _evenprep(proj, ab, halos, wqkv, wb, adt, row0, ntiles, tm, stride, hr, tps, zero_start, nh):
    da = nh * DK
    tr = max(V7X_SUBLANES, (CONV_B - 1) * stride)
    cur = lambda c: pl.BlockSpec((tm, da), lambda i: (row0 + i, c))
    if halos is None:
        hb = tm // hr
        hspec = lambda c: pl.BlockSpec((hr, da), lambda i: (jnp.maximum((row0 + i) * hb - 1, 0), c))
        hargs = [proj] * 5
        hspecs = [hspec(0), hspec(1), hspec(2), hspec(4), hspec(5)]
    else:
        hargs = list(halos)
        hspecs = [pl.BlockSpec((hr, da), lambda i: (0, 0)) for _ in range(5)]
    rows = ntiles * tm
    full = lambda a: pl.BlockSpec(a.shape, lambda i: (0, 0))
    return pl.pallas_call(
        functools.partial(_evenprep_kernel, stride=stride, hr=hr, tps=tps, zero_start=zero_start, nh=nh),
        out_shape=(jax.ShapeDtypeStruct((rows, da), F32), jax.ShapeDtypeStruct((rows, da), F32),
                   jax.ShapeDtypeStruct((rows, da), F32), jax.ShapeDtypeStruct((rows, V7X_LANES), F32),
                   jax.ShapeDtypeStruct((rows, da), BF16), jax.ShapeDtypeStruct((ntiles * tr, da), F32)),
        grid=(ntiles,),
        in_specs=[cur(0), cur(1), cur(2), cur(4), cur(5), cur(6),
                  pl.BlockSpec((tm, V7X_LANES), lambda i: (row0 + i, 0))] + hspecs + [full(wqkv), full(wb), full(adt)],
        out_specs=(pl.BlockSpec((tm, da), lambda i: (i, 0)), pl.BlockSpec((tm, da), lambda i: (i, 0)),
                   pl.BlockSpec((tm, da), lambda i: (i, 0)), pl.BlockSpec((tm, V7X_LANES), lambda i: (i, 0)),
                   pl.BlockSpec((tm, da), lambda i: (i, 0)), pl.BlockSpec((tr, da), lambda i: (i, 0))),
        scratch_shapes=[pltpu.VMEM((hr + tm, da), F32) for _ in range(4)],
        compiler_params=_params("parallel"),
        name="evenprep",
    )(proj, proj, proj, proj, proj, proj, ab, *hargs, wqkv, wb, adt)


def _delta_kernel(*refs, nh, has_s0):
    if has_s0:
        q_ref, k_ref, v_ref, gb_ref, s0_ref, o_ref, sout_ref, s_ref = refs
    else:
        q_ref, k_ref, v_ref, gb_ref, o_ref, sout_ref, s_ref = refs
    n = pl.program_id(1)
    nseq, c = q_ref.shape[0], q_ref.shape[1]

    @pl.when(n == 0)
    def _():
        s_ref[...] = s0_ref[0] if has_s0 else jnp.zeros(s_ref.shape, F32)

    row = lax.broadcasted_iota(I32, (c, c), 0)
    col = lax.broadcasted_iota(I32, (c, c), 1)
    causal = row >= col
    strict = row > col
    eye = (row == col).astype(F32)

    tri = causal.astype(BF16)
    gbs, gcums, gcum_ts = [], [], []
    for sq in range(nseq):
        gb = gb_ref[sq]
        g1 = gb.astype(BF16)
        r1 = gb - g1.astype(F32)
        g2 = r1.astype(BF16)
        g3 = (r1 - g2.astype(F32)).astype(BF16)
        gcum = _dot(tri, g1, _NN) + (_dot(tri, g2, _NN) + _dot(tri, g3, _NN))
        gbs.append(gb)
        gcums.append(gcum)
        gcum_ts.append(gcum.T)
    levels = int(math.log2(c))

    chains = [(sq, hh) for sq in range(nseq) for hh in range(nh)]
    heads = range(len(chains))
    lanes = [slice(hh * DK, (hh + 1) * DK) for _, hh in chains]
    qs = [q_ref[sq, :, lanes[h]] for h, (sq, _) in enumerate(chains)]
    ks = [k_ref[sq, :, lanes[h]] for h, (sq, _) in enumerate(chains)]
    g_col = [gcums[sq][:, hh:hh + 1] for sq, hh in chains]
    g_last = [gcums[sq][c - 1:c, hh:hh + 1] for sq, hh in chains]
    beta = [gbs[sq][:, nh + hh:nh + hh + 1] for sq, hh in chains]
    decay = [jnp.where(causal, jnp.exp(jnp.where(causal, g_col[h] - gcum_ts[sq][hh:hh + 1, :], 0.0)), 0.0)
             for h, (sq, hh) in enumerate(chains)]
    kb = [ks[h] * beta[h] for h in heads]
    khb = [ks[h].astype(BF16) for h in heads]
    low = [jnp.where(strict, _dot(kb[h].astype(BF16), khb[h], _NT) * decay[h], 0.0) for h in heads]
    inv = [eye - low[h] for h in heads]
    pw = [_split2(low[h]) for h in heads]
    for _ in range(levels - 1):
        pw = [_split2(_mm3s(pw[h], pw[h])) for h in heads]
        inv = [inv[h] + _mm3s(_split2(inv[h]), pw[h]) for h in heads]
    eg = [jnp.exp(g_col[h]) for h in heads]
    uw = [_mm3s(_split2(inv[h]),
                _split2(jnp.concatenate([v_ref[chains[h][0], :, lanes[h]] * beta[h], kb[h] * eg[h]], axis=1)))
          for h in heads]
    intra = [jnp.where(causal, _dot(qs[h].astype(BF16), khb[h], _NT) * decay[h], 0.0) for h in heads]
    s = [s_ref[sq, hh] for sq, hh in chains]
    sb = [s[h].astype(BF16) for h in heads]
    vnb = [(uw[h][:, :DV] - _dot(uw[h][:, DV:].astype(BF16), sb[h], _NN)).astype(BF16) for h in heads]
    for h, (sq, hh) in enumerate(chains):
        o_ref[sq, :, lanes[h]] = (_dot((qs[h] * eg[h]).astype(BF16), sb[h], _NN)
                                  + _dot(intra[h].astype(BF16), vnb[h], _NN))
    for h, (sq, hh) in enumerate(chains):
        k_dec = ks[h] * jnp.exp(g_last[h] - g_col[h])
        s_ref[sq, hh] = s[h] * jnp.exp(g_last[h]) + _dot(k_dec.astype(BF16), vnb[h], _TN)

    @pl.when(n == pl.num_programs(1) - 1)
    def _():
        sout_ref[...] = s_ref[...]


def _delta(q, k, v, gb, s0, li, nb, nchunks, c, nh, nseq):
    da = nh * DK
    assert nb % nseq == 0
    seqlen = nchunks * c
    view = lambda a: a.reshape(nb, seqlen, a.shape[1])
    rowspec = lambda w, col=0: pl.BlockSpec((nseq, c, w), lambda b, n: (b, n, col))
    sspec = pl.BlockSpec((nseq, nh, DK, DV), lambda b, n: (b, 0, 0, 0))
    if k is None:
        in_specs = [rowspec(da, 0), rowspec(da, 1), rowspec(da, 2), rowspec(V7X_LANES, 3 * da // V7X_LANES)]
        args = [view(q)] * 4
    else:
        in_specs = [rowspec(da), rowspec(da), rowspec(da), rowspec(V7X_LANES)]
        args = [view(q), view(k), view(v), view(gb)]
    if s0 is not None:
        in_specs.append(pl.BlockSpec((1, nseq, nh, DK, DV), lambda b, n: (li, b, 0, 0, 0)))
        args.append(s0)
    o, s_out = pl.pallas_call(
        functools.partial(_delta_kernel, nh=nh, has_s0=s0 is not None),
        out_shape=(jax.ShapeDtypeStruct((nb, seqlen, da), F32), jax.ShapeDtypeStruct((nb, nh, DK, DV), F32)),
        grid=(nb // nseq, nchunks),
        in_specs=in_specs,
        out_specs=(rowspec(da), sspec),
        scratch_shapes=[pltpu.VMEM((nseq, nh, DK, DV), F32)],
        compiler_params=_params("parallel", "arbitrary"),
        name="delta_rule",
    )(*args)
    return o.reshape(nb * seqlen, da), s_out


def _even_out_kernel(op_ref, os_ref, gout_ref, ybp_ref, ybs_ref, og_ref, mpg, msg, w_ref, x_ref, xo_ref, yin_ref,
                     *, ntp, reps, nh):
    i = pl.program_id(0)
    da = nh * DV

    @pl.when(pl.program_id(1) == 0)
    def _():
        is_s = i >= ntp
        for h in range(nh):
            sl = slice(h * DV, (h + 1) * DV)
            o = jnp.where(is_s, os_ref[:, sl], op_ref[:, sl])
            yin_ref[:, sl] = (_rms(o) * og_ref[...] * _silu(gout_ref[:, sl])).astype(BF16)
        yin_ref[:, da:] = jnp.where(is_s, ybs_ref[...], ybp_ref[...])

    y = jnp.dot(yin_ref[...], w_ref[0].astype(BF16), preferred_element_type=F32)
    xo_ref[...] = x_ref[...] + _tile_mod(mpg, msg, i >= ntp, reps) * y


def _even_out(o_p, o_s, proj, yb_p, yb_s, og, mp, ms, w_out, li, x, dims, nh):
    t, d = x.shape
    tm, ntp, tps, bp, bs, reps = dims
    da = nh * DV
    tn = min(1024, d)
    nj = d // tn
    mpg = pl.BlockSpec((1, 1, 1, tn), lambda i, j: (2, jnp.minimum(i // tps, bp - 1), 0, j))
    msg = pl.BlockSpec((1, bs, tn), lambda i, j: (2, 0, j))
    prompt_rows = lambda w: pl.BlockSpec((tm, w), lambda i, j: (jnp.minimum(i, ntp - 1), 0))
    sample_rows = lambda w: pl.BlockSpec((tm, w), lambda i, j: (0, 0))
    return pl.pallas_call(
        functools.partial(_even_out_kernel, ntp=ntp, reps=reps, nh=nh),
        out_shape=jax.ShapeDtypeStruct((t, d), F32),
        grid=(t // tm, nj),
        in_specs=[
            prompt_rows(da), sample_rows(da),
            pl.BlockSpec((tm, da), lambda i, j: (i, 3)),
            prompt_rows(yb_p.shape[1]), sample_rows(yb_s.shape[1]),
            pl.BlockSpec((1, DV), lambda i, j: (0, 0)),
            mpg, msg,
            pl.BlockSpec((1, w_out.shape[1], tn), lambda i, j: (li, 0, j)),
            pl.BlockSpec((tm, tn), lambda i, j: (i, j)),
        ],
        out_specs=pl.BlockSpec((tm, tn), lambda i, j: (i, j)),
        scratch_shapes=[pltpu.VMEM((tm, w_out.shape[1]), BF16)],
        compiler_params=_params("parallel", "arbitrary"),
        name="even_out",
    )(o_p, o_s, proj, yb_p, yb_s, og, mp, ms, w_out, x)


def _odd_in_kernel(hb_ref, wa_ref, wb_ref, ba_ref, bb_ref, u_ref, wab_ref, wbb_ref):
    @pl.when(pl.program_id(1) == 0)
    def _():
        wab_ref[...] = wa_ref[0].astype(BF16)
        wbb_ref[...] = wb_ref[0].astype(BF16)

    hb = hb_ref[...]
    a = jnp.dot(hb, wab_ref[...], preferred_element_type=F32) + ba_ref[0]
    b = jnp.dot(hb, wbb_ref[...], preferred_element_type=F32) + bb_ref[0]
    u_ref[...] = a * jax.nn.sigmoid(b)


def _odd_in(hb, w_in, b_in, li, tm):
    t, d = hb.shape
    dc = w_in.shape[2] // 2
    tn = min(512, dc)
    nj = dc // tn
    return pl.pallas_call(
        _odd_in_kernel,
        out_shape=jax.ShapeDtypeStruct((t, dc), F32),
        grid=(nj, t // tm),
        in_specs=[
            pl.BlockSpec((tm, d), lambda j, i: (i, 0)),
            pl.BlockSpec((1, d, tn), lambda j, i: (li, 0, j)),
            pl.BlockSpec((1, d, tn), lambda j, i: (li, 0, nj + j)),
            pl.BlockSpec((1, 1, tn), lambda j, i: (li, 0, j)),
            pl.BlockSpec((1, 1, tn), lambda j, i: (li, 0, nj + j)),
        ],
        out_specs=pl.BlockSpec((tm, tn), lambda j, i: (i, j)),
        scratch_shapes=[pltpu.VMEM((d, tn), BF16), pltpu.VMEM((d, tn), BF16)],
        compiler_params=_params("arbitrary", "arbitrary"),
        name="odd_in",
    )(hb, w_in, w_in, b_in, b_in)


def _dwconv_kernel(u_ref, halo_ref, w_ref, b_ref, z_ref, ext_ref, *, stride, hr, tps, zero_start):
    tm = u_ref.shape[0]
    keep = jnp.logical_not(jnp.logical_and(zero_start, pl.program_id(0) % tps == 0)).astype(F32)
    ext_ref[pl.ds(0, hr), :] = halo_ref[...] * keep
    ext_ref[pl.ds(hr, tm), :] = u_ref[...]
    w = w_ref[0]
    b = b_ref[0]
    rb = min(32, tm)

    def chunk(ci, carry):
        r0 = pl.multiple_of(ci * rb, rb)
        z_ref[pl.ds(r0, rb), :] = _causal_taps(ext_ref, r0, rb, slice(None), w, CONV_C, stride, hr) + b
        return carry

    lax.fori_loop(0, tm // rb, chunk, 0)


def _dwconv(u, halo, dw_w, dw_b, li, row0, ntiles, tm, stride, hr, tps, zero_start):
    dc = u.shape[1]
    cb = min(256, dc)
    if halo is None:
        hb = tm // hr
        harg = u
        hspec = pl.BlockSpec((hr, cb), lambda i, c: (jnp.maximum((row0 + i) * hb - 1, 0), c))
    else:
        harg = halo
        hspec = pl.BlockSpec((hr, cb), lambda i, c: (0, c))
    return pl.pallas_call(
        functools.partial(_dwconv_kernel, stride=stride, hr=hr, tps=tps, zero_start=zero_start),
        out_shape=jax.ShapeDtypeStruct((ntiles * tm, dc), F32),
        grid=(ntiles, dc // cb),
        in_specs=[
            pl.BlockSpec((tm, cb), lambda i, c: (row0 + i, c)),
            hspec,
            pl.BlockSpec((1, CONV_C, cb), lambda i, c: (li, 0, c)),
            pl.BlockSpec((1, 1, cb), lambda i, c: (li, 0, c)),
        ],
        out_specs=pl.BlockSpec((tm, cb), lambda i, c: (i, c)),
        scratch_shapes=[pltpu.VMEM((hr + tm, cb), F32)],
        compiler_params=_params("parallel", "parallel"),
        name="dwconv",
    )(u, harg, dw_w, dw_b)


def _odd_out_kernel(zp_ref, zsm_ref, lg_ref, lb_ref, mpg, msg, w_ref, b_ref, x_ref, xo_ref, zs_ref, *, ntp, reps):
    i = pl.program_id(0)

    @pl.when(pl.program_id(1) == 0)
    def _():
        z = jnp.where(i >= ntp, zsm_ref[...], zp_ref[...])
        zc = z - jnp.mean(z, axis=-1, keepdims=True)
        y = zc * lax.rsqrt(jnp.mean(zc * zc, axis=-1, keepdims=True) + EPS)
        zs_ref[...] = _silu(y * lg_ref[0] + lb_ref[0]).astype(BF16)

    y = jnp.dot(zs_ref[...], w_ref[0].astype(BF16), preferred_element_type=F32) + b_ref[0]
    xo_ref[...] = x_ref[...] + _tile_mod(mpg, msg, i >= ntp, reps) * y


def _odd_out(z_p, z_s, ln_g, ln_b, mp, ms, w_out, b_out, li, x, dims):
    t, d = x.shape
    tm, ntp, tps, bp, bs, reps = dims
    dc = z_p.shape[1]
    tn = min(1024, d)
    nj = d // tn
    mpg = pl.BlockSpec((1, 1, 1, tn), lambda i, j: (2, jnp.minimum(i // tps, bp - 1), 0, j))
    msg = pl.BlockSpec((1, bs, tn), lambda i, j: (2, 0, j))
    return pl.pallas_call(
        functools.partial(_odd_out_kernel, ntp=ntp, reps=reps),
        out_shape=jax.ShapeDtypeStruct((t, d), F32),
        grid=(t // tm, nj),
        in_specs=[
            pl.BlockSpec((tm, dc), lambda i, j: (jnp.minimum(i, ntp - 1), 0)),
            pl.BlockSpec((tm, dc), lambda i, j: (0, 0)),
            pl.BlockSpec((1, 1, dc), lambda i, j: (li, 0, 0)),
            pl.BlockSpec((1, 1, dc), lambda i, j: (li, 0, 0)),
            mpg, msg,
            pl.BlockSpec((1, dc, tn), lambda i, j: (li, 0, j)),
            pl.BlockSpec((1, 1, tn), lambda i, j: (li, 0, j)),
            pl.BlockSpec((tm, tn), lambda i, j: (i, j)),
        ],
        out_specs=pl.BlockSpec((tm, tn), lambda i, j: (i, j)),
        scratch_shapes=[pltpu.VMEM((tm, dc), BF16)],
        compiler_params=_params("parallel", "arbitrary"),
        name="odd_out",
    )(z_p, z_s, ln_g, ln_b, mp, ms, w_out, b_out, x)


def _ffnpre_kernel(x_ref, g_ref, mpsh, mpsc, mssh, mssc, wr_ref, br_ref,
                   h_ref, idx_ref, gate_ref, rank_ref, cnt_ref, carry_ref, *, ntp, reps):
    i = pl.program_id(0)
    tm = x_ref.shape[0]
    is_s = i >= ntp
    h = _rms(x_ref[...]) * g_ref[0] * (1 + _tile_mod(mpsc, mssc, is_s, reps)) + _tile_mod(mpsh, mssh, is_s, reps)
    h_ref[...] = _pack_bf16_pairs(h)
    logits = _mm(h, wr_ref[0]) + br_ref[0]

    @pl.when(i == 0)
    def _():
        carry_ref[...] = jnp.zeros(carry_ref.shape, F32)

    lane = lax.broadcasted_iota(I32, logits.shape, 1)
    work = logits
    sels, tops, picks = [], [], []
    for _ in range(TOP_K):
        m = jnp.max(work, axis=-1, keepdims=True)
        pick = jnp.min(jnp.where(work == m, lane, V7X_LANES), axis=-1, keepdims=True)
        sel = lane == pick
        work = jnp.where(sel, -jnp.inf, work)
        sels.append(sel)
        tops.append(m)
        picks.append(pick)
    exps = [jnp.exp(m - tops[0]) for m in tops]
    denom = exps[0]
    for e in exps[1:]:
        denom = denom + e

    onehot = sels[0]
    for s in sels[1:]:
        onehot = jnp.logical_or(onehot, s)
    onehot = onehot.astype(F32)
    r = lax.broadcasted_iota(I32, (tm, tm), 0)
    c = lax.broadcasted_iota(I32, (tm, tm), 1)
    before = _mm((r > c).astype(F32), onehot) + carry_ref[...]
    carry_ref[...] = carry_ref[...] + jnp.sum(onehot, axis=0, keepdims=True)

    idx_out = jnp.zeros(logits.shape, I32)
    gate_out = jnp.zeros(logits.shape, F32)
    rank_out = jnp.zeros(logits.shape, I32)
    for kk in range(TOP_K):
        rank = jnp.sum(jnp.where(sels[kk], before, 0.0), axis=-1, keepdims=True).astype(I32)
        idx_out = jnp.where(lane == kk, picks[kk], idx_out)
        gate_out = jnp.where(lane == kk, exps[kk] / denom, gate_out)
        rank_out = jnp.where(lane == kk, rank, rank_out)
    idx_ref[...] = idx_out
    gate_ref[...] = gate_out
    rank_ref[...] = rank_out
    cnt_ref[...] = carry_ref[...].astype(I32)


def _ffnpre(x, g, mp, ms, w_router, b_router, li, dims):
    t, d = x.shape
    tm, ntp, tps, bp, bs, reps = dims
    sh = _mod_specs(3, tps, bp, bs, d, 1)
    sc = _mod_specs(4, tps, bp, bs, d, 1)
    lanes = jax.ShapeDtypeStruct((t, V7X_LANES), I32)
    rowspec = pl.BlockSpec((tm, V7X_LANES), lambda i: (i, 0))
    return pl.pallas_call(
        functools.partial(_ffnpre_kernel, ntp=ntp, reps=reps),
        out_shape=(jax.ShapeDtypeStruct((t, d // 2), U32), lanes, jax.ShapeDtypeStruct((t, V7X_LANES), F32), lanes,
                   jax.ShapeDtypeStruct((1, V7X_LANES), I32)),
        grid=(t // tm,),
        in_specs=[
            pl.BlockSpec((tm, d), lambda i: (i, 0)),
            pl.BlockSpec((1, 1, d), lambda i: (li, 0, 0)),
            sh[0], sc[0], sh[1], sc[1],
            pl.BlockSpec((1, d, V7X_LANES), lambda i: (li, 0, 0)),
            pl.BlockSpec((1, 1, V7X_LANES), lambda i: (li, 0, 0)),
        ],
        out_specs=(pl.BlockSpec((tm, d // 2), lambda i: (i, 0)), rowspec, rowspec, rowspec,
                   pl.BlockSpec((1, V7X_LANES), lambda i: (0, 0))),
        scratch_shapes=[pltpu.VMEM((1, V7X_LANES), F32)],
        compiler_params=_params("arbitrary"),
        name="ffn_pre_router",
    )(x, g, mp, mp, ms, ms, w_router, b_router)


def _dispatch_kernel(slot_ref, zrow_ref, h_ref, xs_ref, zero_ref, sem, *, nexp, bm):
    i = pl.program_id(0)
    tm = h_ref.shape[0]

    @pl.when(i == 0)
    def _():
        zero_ref[...] = jnp.zeros(zero_ref.shape, zero_ref.dtype)

        def zcopy(r):
            return pltpu.make_async_copy(zero_ref, xs_ref.at[pl.ds(r, 1)], sem)

        def zfill(e, carry):
            base = zrow_ref[e]

            def issue(j, cc):
                for jj in range(ROW_UNROLL):
                    zcopy(base + j * ROW_UNROLL + jj).start()
                return cc

            lax.fori_loop(0, bm // ROW_UNROLL, issue, 0)

            def drain(j, cc):
                for jj in range(ROW_UNROLL):
                    zcopy(base + j * ROW_UNROLL + jj).wait()
                return cc

            lax.fori_loop(0, bm // ROW_UNROLL, drain, 0)
            return carry

        lax.fori_loop(0, nexp, zfill, 0)

    def rcopy(t, kk):
        return pltpu.make_async_copy(h_ref.at[pl.ds(t, 1)],
                                     xs_ref.at[pl.ds(slot_ref[(i * tm + t) * TOP_K + kk], 1)], sem)

    def group(g, fn):
        for tt in range(ROW_UNROLL):
            for kk in range(TOP_K):
                fn(rcopy(g * ROW_UNROLL + tt, kk), kk)

    ngroups = tm // ROW_UNROLL

    def body(g, carry):
        group(g, lambda c, kk: c.start(priority=kk % 2))

        @pl.when(g > 0)
        def _():
            group(g - 1, lambda c, kk: c.wait())

        return carry

    lax.fori_loop(0, ngroups, body, 0)
    group(ngroups - 1, lambda c, kk: c.wait())


def _dispatch(slots, zrows, h, rows, bm, tm):
    t, d = h.shape
    nexp = zrows.shape[0]
    assert tm % ROW_UNROLL == 0 and bm % ROW_UNROLL == 0
    return pl.pallas_call(
        functools.partial(_dispatch_kernel, nexp=nexp, bm=bm),
        out_shape=jax.ShapeDtypeStruct((rows, d), h.dtype),
        grid_spec=pltpu.PrefetchScalarGridSpec(
            num_scalar_prefetch=2,
            grid=(t // tm,),
            in_specs=[pl.BlockSpec((tm, d), lambda i, sl, zr: (i, 0))],
            out_specs=pl.BlockSpec(memory_space=pl.ANY),
            scratch_shapes=[pltpu.VMEM((1, d), h.dtype), pltpu.SemaphoreType.DMA(())],
        ),
        compiler_params=_params("arbitrary"),
        name="moe_dispatch",
    )(slots, zrows, h)


def _weight_stream(be_ref, grp_ref, ge_ref, meta_ref, copies, install):
    j, i = pl.program_id(0), pl.program_id(1)
    nj = pl.num_programs(0)
    nv, ng = meta_ref[0], meta_ref[1]
    valid = i < nv
    first = jnp.logical_and(valid, jnp.logical_or(i == 0, be_ref[i] != be_ref[jnp.maximum(i - 1, 0)]))

    @pl.when(first)
    def _():
        grp = grp_ref[i]

        @pl.when(jnp.logical_and(j == 0, grp == 0))
        def _():
            for c in copies(j, be_ref[i]):
                c.start()

        for c in copies(j, be_ref[i]):
            c.wait()
        install()
        wrap = grp + 1 == ng
        nxt_j = jnp.where(wrap, j + 1, j)
        nxt_e = ge_ref[jnp.where(wrap, 0, grp + 1)]

        @pl.when(nxt_j < nj)
        def _():
            for c in copies(nxt_j, nxt_e):
                c.start()

    return valid


def _expert_up_kernel(be_ref, grp_ref, ge_ref, meta_ref, x_ref, w_ref, bg_ref, bu_ref, act_ref,
                      sg_ref, su_ref, wgb_ref, wub_ref, sem, *, li, f):
    tn = sg_ref.shape[1]

    def copies(j, e):
        col = pl.multiple_of(j * tn, tn)
        return (pltpu.make_async_copy(w_ref.at[li, e, :, pl.ds(col, tn)], sg_ref, sem.at[0]),
                pltpu.make_async_copy(w_ref.at[li, e, :, pl.ds(f + col, tn)], su_ref, sem.at[1]))

    def install():
        wgb_ref[...] = sg_ref[...].astype(BF16)
        wub_ref[...] = su_ref[...].astype(BF16)

    valid = _weight_stream(be_ref, grp_ref, ge_ref, meta_ref, copies, install)

    @pl.when(valid)
    def _():
        xb = _unpack_bf16_pairs(x_ref[...])
        g = jnp.dot(xb, wgb_ref[...], preferred_element_type=F32) + bg_ref[0, 0]
        u = jnp.dot(xb, wub_ref[...], preferred_element_type=F32) + bu_ref[0, 0]
        gate = jnp.minimum(g, SWIGLU_LIMIT)
        up = jnp.clip(u, -SWIGLU_LIMIT, SWIGLU_LIMIT)
        act_ref[...] = ((up + 1) * gate * jax.nn.sigmoid(SWIGLU_ALPHA * gate)).astype(BF16)


def _expert_up(route, xs, w_gate_up, b_gate_up, li, bm):
    rows, dh = xs.shape
    d = 2 * dh
    nblk = rows // bm
    f = w_gate_up.shape[3] // 2
    tn = min(EXPERT_COLS, f)
    nj = f // tn
    blk = lambda i, meta: jnp.minimum(i, meta[0] - 1)
    return pl.pallas_call(
        functools.partial(_expert_up_kernel, li=li, f=f),
        out_shape=jax.ShapeDtypeStruct((rows, f), BF16),
        grid_spec=pltpu.PrefetchScalarGridSpec(
            num_scalar_prefetch=4,
            grid=(nj, nblk),
            in_specs=[
                pl.BlockSpec((bm, dh), lambda j, i, be, grp, ge, meta: (blk(i, meta), 0)),
                pl.BlockSpec(memory_space=pl.ANY),
                pl.BlockSpec((1, 1, 1, tn), lambda j, i, be, grp, ge, meta: (li, be[blk(i, meta)], 0, j)),
                pl.BlockSpec((1, 1, 1, tn), lambda j, i, be, grp, ge, meta: (li, be[blk(i, meta)], 0, nj + j)),
            ],
            out_specs=pl.BlockSpec((bm, tn), lambda j, i, be, grp, ge, meta: (blk(i, meta), j)),
            scratch_shapes=[pltpu.VMEM((d, tn), F32), pltpu.VMEM((d, tn), F32),
                            pltpu.VMEM((d, tn), BF16), pltpu.VMEM((d, tn), BF16),
                            pltpu.SemaphoreType.DMA((2,))],
        ),
        compiler_params=_params("arbitrary", "arbitrary"),
        name="expert_gate_up",
    )(*route, xs, w_gate_up, b_gate_up, b_gate_up)


def _expert_down_kernel(be_ref, grp_ref, ge_ref, meta_ref, a_ref, w_ref, b_ref, y_ref, st_ref, wb_ref, sem, *, li):
    tn = st_ref.shape[1]

    def copies(j, e):
        col = pl.multiple_of(j * tn, tn)
        return (pltpu.make_async_copy(w_ref.at[li, e, :, pl.ds(col, tn)], st_ref, sem.at[0]),)

    def install():
        wb_ref[...] = st_ref[...].astype(BF16)

    valid = _weight_stream(be_ref, grp_ref, ge_ref, meta_ref, copies, install)

    @pl.when(valid)
    def _():
        y_ref[...] = jnp.dot(a_ref[...], wb_ref[...], preferred_element_type=F32) + b_ref[0, 0]


def _expert_down(route, act, w_down, b_down, li, bm):
    rows, f = act.shape
    nblk = rows // bm
    d = w_down.shape[3]
    tn = min(EXPERT_DOWN_COLS, d)
    nj = d // tn
    blk = lambda i, meta: jnp.minimum(i, meta[0] - 1)
    return pl.pallas_call(
        functools.partial(_expert_down_kernel, li=li),
        out_shape=jax.ShapeDtypeStruct((rows, d), F32),
        grid_spec=pltpu.PrefetchScalarGridSpec(
            num_scalar_prefetch=4,
            grid=(nj, nblk),
            in_specs=[
                pl.BlockSpec((bm, f), lambda j, i, be, grp, ge, meta: (blk(i, meta), 0)),
                pl.BlockSpec(memory_space=pl.ANY),
                pl.BlockSpec((1, 1, 1, tn), lambda j, i, be, grp, ge, meta: (li, be[blk(i, meta)], 0, j)),
            ],
            out_specs=pl.BlockSpec((bm, tn), lambda j, i, be, grp, ge, meta: (blk(i, meta), j)),
            scratch_shapes=[pltpu.VMEM((f, tn), F32), pltpu.VMEM((f, tn), BF16), pltpu.SemaphoreType.DMA((1,))],
        ),
        compiler_params=_params("arbitrary", "arbitrary"),
        name="expert_down",
    )(*route, act, w_down, b_down)


def _combine_kernel(slot_ref, gate_ref, mpg, msg, x_ref, ys_ref, xo_ref, buf_ref, sem, *, ntp, reps):
    i = pl.program_id(0)
    tm = x_ref.shape[0]

    def rcopy(t, kk):
        return pltpu.make_async_copy(ys_ref.at[pl.ds(slot_ref[(i * tm + t) * TOP_K + kk], 1)],
                                     buf_ref.at[kk, pl.ds(t, 1)], sem)

    def issue(g, carry):
        for tt in range(ROW_UNROLL):
            for kk in range(TOP_K):
                rcopy(g * ROW_UNROLL + tt, kk).start(priority=kk % 2)
        return carry

    lax.fori_loop(0, tm // ROW_UNROLL, issue, 0)

    def drain(g, carry):
        for tt in range(ROW_UNROLL):
            for kk in range(TOP_K):
                rcopy(g * ROW_UNROLL + tt, kk).wait()
        return carry

    lax.fori_loop(0, tm // ROW_UNROLL, drain, 0)

    gates = gate_ref[...]
    y = buf_ref[0] * gates[:, 0:1]
    for kk in range(1, TOP_K):
        y = y + buf_ref[kk] * gates[:, kk:kk + 1]
    xo_ref[...] = x_ref[...] + _tile_mod(mpg, msg, i >= ntp, reps) * y


def _combine(slots, gates, mp, ms, x, ys, dims):
    t, d = x.shape
    tm, ntp, tps, bp, bs, reps = dims
    mpg = pl.BlockSpec((1, 1, 1, d), lambda i, sl: (5, jnp.minimum(i // tps, bp - 1), 0, 0))
    msg = pl.BlockSpec((1, bs, d), lambda i, sl: (5, 0, 0))
    return pl.pallas_call(
        functools.partial(_combine_kernel, ntp=ntp, reps=reps),
        out_shape=jax.ShapeDtypeStruct((t, d), F32),
        grid_spec=pltpu.PrefetchScalarGridSpec(
            num_scalar_prefetch=1,
            grid=(t // tm,),
            in_specs=[
                pl.BlockSpec((tm, V7X_LANES), lambda i, sl: (i, 0)),
                mpg, msg,
                pl.BlockSpec((tm, d), lambda i, sl: (i, 0)),
                pl.BlockSpec(memory_space=pl.ANY),
            ],
            out_specs=pl.BlockSpec((tm, d), lambda i, sl: (i, 0)),
            scratch_shapes=[pltpu.VMEM((TOP_K, tm, d), F32), pltpu.SemaphoreType.DMA(())],
        ),
        compiler_params=_params("arbitrary"),
        name="moe_combine",
    )(slots, gates, mp, ms, x, ys)


def _moe(x, li, g, mp, ms, w_router_p, b_router_p, w_gate_up, b_gate_up4, w_down, b_down4, dims):
    t, d = x.shape
    nexp = w_gate_up.shape[1]
    bm = MOE_ROWS
    h, idx, gates, rank, cnt = _ffnpre(x, g, mp, ms, w_router_p, b_router_p, li, dims)
    counts = cnt[0, :nexp]
    padded = (counts + bm - 1) // bm * bm
    pad_end = jnp.cumsum(padded)
    pad_start = pad_end - padded
    idx4 = idx[:, :TOP_K]
    start4 = jnp.sum(jnp.where(idx4[..., None] == jnp.arange(nexp, dtype=I32), pad_start, 0), axis=-1)
    slots = (start4 + rank[:, :TOP_K]).reshape(t * TOP_K).astype(I32)
    nblk = -(-(t * TOP_K + nexp * (bm - 1)) // bm) + 1
    rows = nblk * bm
    blk_start = jnp.arange(nblk, dtype=I32) * bm
    blk_expert = jnp.minimum(jnp.sum(pad_end[None, :] <= blk_start[:, None], axis=1), nexp - 1).astype(I32)
    has = counts > 0
    run_of_expert = jnp.cumsum(has.astype(I32)) - 1
    blk_run = jnp.sum(jnp.where(blk_expert[:, None] == jnp.arange(nexp, dtype=I32), run_of_expert, 0), axis=1)
    run_expert = jnp.argsort(jnp.logical_not(has), stable=True).astype(I32)
    meta = jnp.stack([pad_end[-1] // bm, jnp.sum(has.astype(I32))]).astype(I32)
    route = (blk_expert, blk_run.astype(I32), run_expert, meta)
    zrows = (pad_start + counts).astype(I32)
    xs = _dispatch(slots, zrows, h, rows, bm, dims[0])
    act = _expert_up(route, xs, w_gate_up, b_gate_up4, li, bm)
    ys = _expert_down(route, act, w_down, b_down4, li, bm)
    return _combine(slots, gates, mp, ms, x, ys, dims)


def _final_kernel(x_ref, g_ref, yp_ref, ys_ref, *, ntp):
    y = _rms(x_ref[...]) * g_ref[...]
    i = pl.program_id(0)

    @pl.when(i < ntp)
    def _():
        yp_ref[...] = y

    @pl.when(i >= ntp)
    def _():
        ys_ref[...] = y


def _final_norm(x, g, tm, ntp):
    t, d = x.shape
    return pl.pallas_call(
        functools.partial(_final_kernel, ntp=ntp),
        out_shape=(jax.ShapeDtypeStruct((ntp * tm, d), F32), jax.ShapeDtypeStruct((t - ntp * tm, d), F32)),
        grid=(t // tm,),
        in_specs=[pl.BlockSpec((tm, d), lambda i: (i, 0)), pl.BlockSpec((1, d), lambda i: (0, 0))],
        out_specs=(pl.BlockSpec((tm, d), lambda i: (jnp.minimum(i, ntp - 1), 0)),
                   pl.BlockSpec((tm, d), lambda i: (0, 0))),
        compiler_params=_params("arbitrary"),
        name="final_norm",
    )(x, g)


def _to_pos_major(a):
    b, l, c = a.shape
    return jnp.swapaxes(a, 0, 1).reshape(l * b, c)


def _seq_tails(a, nseq, seqlen, n, width):
    return jnp.stack([a[(b + 1) * seqlen - n:(b + 1) * seqlen, :width] for b in range(nseq)])


def _from_pos_major(a, b):
    lb, c = a.shape
    return jnp.swapaxes(a.reshape(lb // b, b, c), 0, 1)


def kernel(x_prompt, x_sample, state_delta, state_conv_qkv, state_conv_b, state_conv_c, c_prompt, c_sample, norm_mix, norm_ffn, w_mod, b_mod, w_in_even, conv_qkv_w, a_log, dt_bias, o_norm_g, conv_b_w, w_out_even, w_in_odd, b_in_odd, dw_w, dw_b, ln_g, ln_b, w_out_odd, b_out_odd, w_router, b_router, w_gate_up, b_gate_up, w_down, b_down, norm_final):
    bp, seq, d = x_prompt.shape
    bs, dseq, _ = x_sample.shape
    depth = w_mod.shape[0]
    nh = a_log.shape[1]
    da = nh * DV
    db = conv_b_w.shape[2]
    dc = dw_w.shape[2]
    nexp = w_router.shape[2]
    tp, ts = bp * seq, bs * dseq
    tm = ts
    assert seq % tm == 0 and bs % V7X_SUBLANES == 0 and da == db and dseq >= CONV_A - 1
    assert nexp <= V7X_LANES and 2 * nh <= V7X_LANES
    tps = seq // tm
    ntp = tp // tm
    dims = (tm, ntp, tps, bp, bs, dseq)
    chunk = math.gcd(seq, CHUNK_A)

    x = jnp.concatenate([x_prompt.reshape(tp, d), _to_pos_major(x_sample)], axis=0)
    rc = -(-(bp + bs) // V7X_SUBLANES) * V7X_SUBLANES
    c_all = jnp.zeros((rc, d), F32).at[:bp].set(c_prompt).at[bp:bp + bs].set(c_sample)
    mod = _modulation(c_all, w_mod, b_mod)

    w_router_p = jnp.zeros((depth, d, V7X_LANES), F32).at[:, :, :nexp].set(w_router)
    b_router_p = jnp.full((depth, 1, V7X_LANES), -jnp.inf, F32).at[:, 0, :nexp].set(b_router)
    b_gate_up4 = b_gate_up.reshape(depth, nexp, 1, b_gate_up.shape[2])
    b_down4 = b_down.reshape(depth, nexp, 1, d)

    qkv_w = 3 * nh * DK
    c_ab = qkv_w + da
    c_b = c_ab + 2 * nh
    hr_qkv = (CONV_A - 1) * bs
    hr_c = (CONV_C - 1) * bs
    outs = {k: [] for k in ("dp", "qp", "bp", "cp", "ds", "qs", "bs", "cs")}

    for l in range(depth):
        mp = mod[l][:, :bp].reshape(6, bp, 1, d)
        ms = mod[l][:, bp:bp + bs]
        li = l // 2
        if l % 2 == 0:
            w = w_in_even[li]
            w_main = jnp.concatenate([w[:, :c_ab], w[:, c_b:]], axis=1)
            w_ab = jnp.zeros((d, V7X_LANES), F32).at[:, :2 * nh].set(w[:, c_ab:c_b])
            hb, ab = _prenorm(x, norm_mix[l:l + 1], mp, ms, w_ab, dims)
            proj = _even_in(hb, w_main, tm)
            adt = jnp.zeros((V7X_SUBLANES, V7X_LANES), F32).at[0, :nh].set(a_log[li]).at[1, :nh].set(dt_bias[li])
            wqkv, wb = conv_qkv_w[li], conv_b_w[li]
            qp, kp, vp, gbp, ybp, utp = _evenprep(proj, ab, None, wqkv, wb, adt, 0, ntp, tm, 1, V7X_SUBLANES,
                                                  tps, True, nh)
            sq = _to_pos_major(state_conv_qkv[li])
            su = jnp.concatenate([jnp.zeros(((CONV_A - CONV_B) * bs, db), F32), _to_pos_major(state_conv_b[li])], axis=0)
            halos = (sq[:, :da], sq[:, da:2 * da], sq[:, 2 * da:], su, jnp.ones_like(su))
            qs, ks, vs, gbs, ybs, uts = _evenprep(proj, ab, halos, wqkv, wb, adt, ntp, 1, tm, bs, hr_qkv,
                                                  1, False, nh)
            o_p, s_p = _delta(qp, kp, vp, gbp, None, li, bp, seq // chunk, chunk, nh, math.gcd(bp, DELTA_SEQS_PROMPT))

            qkvg = _from_pos_major(jnp.concatenate([qs, ks, vs, gbs], axis=1), bs)
            qkvg = jnp.pad(qkvg, ((0, 0), (0, V7X_SUBLANES - dseq), (0, 0))).reshape(bs * V7X_SUBLANES, -1)
            o_s8, s_s = _delta(qkvg, None, None, None, state_delta, li,
                               bs, 1, V7X_SUBLANES, nh, math.gcd(bs, DELTA_SEQS_SAMPLE))
            o_s = _to_pos_major(o_s8.reshape(bs, V7X_SUBLANES, da)[:, :dseq])
            x = _even_out(o_p, o_s, proj, ybp, ybs, o_norm_g[li:li + 1], mp, ms, w_out_even, li, x, dims, nh)

            outs["dp"].append(s_p)
            outs["qp"].append(_seq_tails(proj, bp, seq, CONV_A - 1, qkv_w))
            outs["bp"].append(utp.reshape(bp, tps, V7X_SUBLANES, db)[:, tps - 1, V7X_SUBLANES - (CONV_B - 1):])
            outs["ds"].append(s_s)
            outs["qs"].append(_from_pos_major(proj[tp + (dseq - (CONV_A - 1)) * bs:, :qkv_w], bs))
            u_ext = jnp.concatenate([state_conv_b[li], _from_pos_major(uts, bs)], axis=1)
            outs["bs"].append(u_ext[:, u_ext.shape[1] - (CONV_B - 1):])
        else:
            (hb,) = _prenorm(x, norm_mix[l:l + 1], mp, ms, None, dims)
            u = _odd_in(hb, w_in_odd, b_in_odd.reshape(-1, 1, 2 * dc), li, tm)
            dwb = dw_b.reshape(-1, 1, dc)
            zp = _dwconv(u, None, dw_w, dwb, li, 0, ntp, tm, 1, 32, tps, True)
            zs = _dwconv(u, _to_pos_major(state_conv_c[li]), dw_w, dwb, li, ntp, 1, tm, bs, hr_c, 1, False)
            x = _odd_out(zp, zs, ln_g.reshape(-1, 1, dc), ln_b.reshape(-1, 1, dc), mp, ms, w_out_odd,
                         b_out_odd.reshape(-1, 1, d), li, x, dims)
            outs["cp"].append(_seq_tails(u, bp, seq, CONV_C - 1, dc))
            c_ext = jnp.concatenate([state_conv_c[li], _from_pos_major(u[tp:], bs)], axis=1)
            outs["cs"].append(c_ext[:, c_ext.shape[1] - (CONV_C - 1):])
        x = _moe(x, l, norm_ffn.reshape(depth, 1, d), mp, ms, w_router_p, b_router_p, w_gate_up, b_gate_up4,
                 w_down, b_down4, dims)

    y_p, y_s = _final_norm(x, norm_final.reshape(1, d), tm, ntp)
    y_prompt = y_p.reshape(bp, seq, d)
    y_sample = _from_pos_major(y_s, bs)
    return (y_prompt, y_sample,
            jnp.stack(outs["dp"]), jnp.stack(outs["qp"]), jnp.stack(outs["bp"]), jnp.stack(outs["cp"]),
            jnp.stack(outs["ds"]), jnp.stack(outs["qs"]), jnp.stack(outs["bs"]), jnp.stack(outs["cs"]))
```

```python
import functools
import math

import jax
import jax.numpy as jnp
from jax import lax
from jax.experimental import pallas as pl
from jax.experimental.pallas import tpu as pltpu

F32 = jnp.float32
BF16 = jnp.bfloat16
I32 = jnp.int32
U32 = jnp.uint32
EPS = 1e-6
HI = lax.Precision.HIGHEST

DK = 128
DV = 128
CONV_A = 4
CONV_B = 3
CONV_C = 31
CHUNK_A = 64
TOP_K = 4
SWIGLU_ALPHA = 1.702
SWIGLU_LIMIT = 7.0

V7X_LANES = 128
V7X_SUBLANES = 8
V7X_VMEM_LIMIT_BYTES = 56 * 1024 * 1024
MOE_ROWS = 256
EXPERT_COLS = 1024
EXPERT_DOWN_COLS = 2048
DELTA_SEQS_PROMPT = 2
DELTA_SEQS_SAMPLE = 4
ROW_UNROLL = 8


def _params(*sem):
    return pltpu.CompilerParams(dimension_semantics=sem, vmem_limit_bytes=V7X_VMEM_LIMIT_BYTES)


def _mm(a, b):
    return jnp.dot(a.astype(BF16), b.astype(BF16), preferred_element_type=F32)


def _mm_hi(a, b):
    return jnp.dot(a, b, precision=HI, preferred_element_type=F32)


_NN = (((1,), (0,)), ((), ()))
_NT = (((1,), (1,)), ((), ()))
_TN = (((0,), (0,)), ((), ()))


def _dot(a, b, dims):
    return lax.dot_general(a, b, dims, preferred_element_type=F32)


def _split2(a):
    hi = a.astype(BF16)
    return hi, (a - hi.astype(F32)).astype(BF16)


def _mm3s(a, b):
    return _dot(a[0], b[0], _NN) + (_dot(a[0], b[1], _NN) + _dot(a[1], b[0], _NN))


def _pack_bf16_pairs(x):
    half = x.shape[1] // 2
    bits = lax.bitcast_convert_type(x.astype(BF16).astype(F32), U32)
    return (bits[:, half:] & jnp.uint32(0xFFFF0000)) | (bits[:, :half] >> 16)


def _unpack_bf16_pairs(w):
    lo = lax.bitcast_convert_type(w << 16, F32)
    hi = lax.bitcast_convert_type(w & jnp.uint32(0xFFFF0000), F32)
    return jnp.concatenate([lo, hi], axis=1).astype(BF16)


def _silu(x):
    return x * jax.nn.sigmoid(x)


def _rms(x):
    return x * lax.rsqrt(jnp.mean(x * x, axis=-1, keepdims=True) + EPS)


def _tile_mod(mp_ref, ms_ref, is_sample, reps):
    s = ms_ref[0]
    s = jnp.concatenate([s] * reps, axis=0)
    return jnp.where(is_sample, s, mp_ref[0, 0])


def _mod_specs(k, tps, bp, bs, d, ngrid):
    if ngrid == 1:
        mp = pl.BlockSpec((1, 1, 1, d), lambda i: (k, jnp.minimum(i // tps, bp - 1), 0, 0))
        ms = pl.BlockSpec((1, bs, d), lambda i: (k, 0, 0))
    else:
        mp = pl.BlockSpec((1, 1, 1, d), lambda i, j: (k, jnp.minimum(i // tps, bp - 1), 0, 0))
        ms = pl.BlockSpec((1, bs, d), lambda i, j: (k, 0, 0))
    return mp, ms


def _mod_kernel(c_ref, w_ref, b_ref, o_ref):
    o_ref[0, 0] = _mm(_silu(c_ref[...]), w_ref[0]) + b_ref[0]


def _modulation(c_all, w_mod, b_mod):
    depth, d, d6 = w_mod.shape
    rc = c_all.shape[0]
    tn = min(512, d)
    nj = d // tn
    return pl.pallas_call(
        _mod_kernel,
        out_shape=jax.ShapeDtypeStruct((depth, 6, rc, d), F32),
        grid=(depth, 6, nj),
        in_specs=[
            pl.BlockSpec((rc, d), lambda l, k, j: (0, 0)),
            pl.BlockSpec((1, d, tn), lambda l, k, j: (l, 0, k * nj + j)),
            pl.BlockSpec((1, 1, tn), lambda l, k, j: (l, 0, k * nj + j)),
        ],
        out_specs=pl.BlockSpec((1, 1, rc, tn), lambda l, k, j: (l, k, 0, j)),
        compiler_params=_params("parallel", "parallel", "parallel"),
        name="modulation",
    )(c_all, w_mod, b_mod.reshape(depth, 1, d6))


def _prenorm_kernel(*refs, ntp, reps, has_ab):
    if has_ab:
        x_ref, g_ref, mpsh, mpsc, mssh, mssc, wab_ref, hb_ref, ab_ref = refs
    else:
        x_ref, g_ref, mpsh, mpsc, mssh, mssc, hb_ref = refs
    is_s = pl.program_id(0) >= ntp
    h = _rms(x_ref[...]) * g_ref[...] * (1 + _tile_mod(mpsc, mssc, is_s, reps)) + _tile_mod(mpsh, mssh, is_s, reps)
    hb = h.astype(BF16)
    hb_ref[...] = hb
    if has_ab:
        ab_ref[...] = jnp.dot(hb, wab_ref[...].astype(BF16), preferred_element_type=F32)


def _prenorm(x, g, mp, ms, w_ab, dims):
    t, d = x.shape
    tm, ntp, tps, bp, bs, reps = dims
    sh = _mod_specs(0, tps, bp, bs, d, 1)
    sc = _mod_specs(1, tps, bp, bs, d, 1)
    in_specs = [pl.BlockSpec((tm, d), lambda i: (i, 0)), pl.BlockSpec((1, d), lambda i: (0, 0)),
                sh[0], sc[0], sh[1], sc[1]]
    args = [x, g, mp, mp, ms, ms]
    out_shape = [jax.ShapeDtypeStruct((t, d), BF16)]
    out_specs = [pl.BlockSpec((tm, d), lambda i: (i, 0))]
    if w_ab is not None:
        in_specs.append(pl.BlockSpec((d, V7X_LANES), lambda i: (0, 0)))
        args.append(w_ab)
        out_shape.append(jax.ShapeDtypeStruct((t, V7X_LANES), F32))
        out_specs.append(pl.BlockSpec((tm, V7X_LANES), lambda i: (i, 0)))
    return pl.pallas_call(
        functools.partial(_prenorm_kernel, ntp=ntp, reps=reps, has_ab=w_ab is not None),
        out_shape=tuple(out_shape),
        grid=(t // tm,),
        in_specs=in_specs,
        out_specs=tuple(out_specs),
        compiler_params=_params("parallel"),
        name="prenorm",
    )(*args)


def _even_in_kernel(hb_ref, w_ref, proj_ref, wb_ref):
    @pl.when(pl.program_id(1) == 0)
    def _():
        wb_ref[...] = w_ref[...].astype(BF16)

    proj_ref[...] = jnp.dot(hb_ref[...], wb_ref[...], preferred_element_type=F32)


def _even_in(hb, w_main, tm):
    t, d = hb.shape
    nmain = w_main.shape[1]
    tn = nmain // 7
    return pl.pallas_call(
        _even_in_kernel,
        out_shape=jax.ShapeDtypeStruct((t, nmain), F32),
        grid=(7, t // tm),
        in_specs=[pl.BlockSpec((tm, d), lambda j, i: (i, 0)), pl.BlockSpec((d, tn), lambda j, i: (0, j))],
        out_specs=pl.BlockSpec((tm, tn), lambda j, i: (i, j)),
        scratch_shapes=[pltpu.VMEM((d, tn), BF16)],
        compiler_params=_params("arbitrary", "arbitrary"),
        name="even_in",
    )(hb, w_main)


def _causal_taps(ext_ref, r0, rb, lanes, w, width, stride, hr):
    acc = None
    if stride % V7X_SUBLANES == 0:
        for j in range(width):
            start = pl.multiple_of(r0 + (hr - (width - 1 - j) * stride), V7X_SUBLANES)
            term = ext_ref[pl.ds(start, rb), lanes] * w[j:j + 1, :]
            acc = term if acc is None else acc + term
        return acc
    look = -(-(width - 1) * stride // V7X_SUBLANES) * V7X_SUBLANES
    nrow = rb + look
    win = ext_ref[pl.ds(pl.multiple_of(r0 + (hr - look), V7X_SUBLANES), nrow), lanes]
    for res in range(V7X_SUBLANES):
        taps = [j for j in range(width) if (look - (width - 1 - j) * stride) % V7X_SUBLANES == res]
        if not taps:
            continue
        shifted = win if res == 0 else pltpu.roll(win, nrow - res, axis=0)
        for j in taps:
            off = look - (width - 1 - j) * stride - res
            term = shifted[off:off + rb, :] * w[j:j + 1, :]
            acc = term if acc is None else acc + term
    return acc


def _evenprep_kernel(q_ref, k_ref, v_ref, xb_ref, cp_ref, bpost_ref, ab_ref,
                     hq_ref, hk_ref, hv_ref, hxb_ref, hcp_ref,
                     wqkv_ref, wb_ref, adt_ref,
                     qo_ref, ko_ref, vo_ref, gb_ref, yb_ref, ut_ref,
                     extq, extk, extv, extu, *, stride, hr, tps, zero_start, nh):
    tm = q_ref.shape[0]
    da = q_ref.shape[1]
    keep = jnp.logical_not(jnp.logical_and(zero_start, pl.program_id(0) % tps == 0)).astype(F32)
    for ext, halo, cur in ((extq, hq_ref, q_ref), (extk, hk_ref, k_ref), (extv, hv_ref, v_ref)):
        ext[pl.ds(0, hr), :] = halo[...] * keep
        ext[pl.ds(hr, tm), :] = cur[...]
    extu[pl.ds(0, hr), :] = hcp_ref[...] * hxb_ref[...] * keep
    extu[pl.ds(hr, tm), :] = cp_ref[...] * xb_ref[...]
    tr = ut_ref.shape[0]
    ut_ref[...] = extu[pl.ds(hr + tm - tr, tr), :]

    ab = ab_ref[...]
    adt = adt_ref[...]
    z = ab + adt[1:2, :]
    softplus = jnp.maximum(z, 0.0) + jnp.log(1.0 + jnp.exp(-jnp.abs(z)))
    lane = lax.broadcasted_iota(I32, ab.shape, 1)
    gb_ref[...] = jnp.where(lane < nh, -jnp.exp(adt[0:1, :]) * softplus, jax.nn.sigmoid(ab))

    wqkv = wqkv_ref[...]
    wb = wb_ref[...]
    rb = min(64, tm)

    def chunk(ci, carry):
        r0 = pl.multiple_of(ci * rb, rb)
        rows = pl.ds(r0, rb)
        for h in range(nh):
            sl = slice(h * DK, (h + 1) * DK)
            taps = lambda ext, w0: _causal_taps(ext, r0, rb, sl, wqkv[:, w0 + h * DK:w0 + (h + 1) * DK],
                                                CONV_A, stride, hr)
            qh = _silu(taps(extq, 0))
            kh = _silu(taps(extk, da))
            qo_ref[rows, sl] = qh * lax.rsqrt(jnp.sum(qh * qh, axis=-1, keepdims=True) + EPS) * (DK ** -0.5)
            ko_ref[rows, sl] = kh * lax.rsqrt(jnp.sum(kh * kh, axis=-1, keepdims=True) + EPS)
            vo_ref[rows, sl] = _silu(taps(extv, 2 * da))
            yb = bpost_ref[rows, sl] * _causal_taps(extu, r0, rb, sl, wb[:, sl], CONV_B, stride, hr)
            yb_ref[rows, sl] = yb.astype(BF16)
        return carry

    lax.fori_loop(0, tm // rb, chunk, 0)


def _evenprep(proj, ab, halos, wqkv, wb, adt, row0, ntiles, tm, stride, hr, tps, zero_start, nh):
    da = nh * DK
    tr = max(V7X_SUBLANES, (CONV_B - 1) * stride)
    cur = lambda c: pl.BlockSpec((tm, da), lambda i: (row0 + i, c))
    if halos is None:
        hb = tm // hr
        hspec = lambda c: pl.BlockSpec((hr, da), lambda i: (jnp.maximum((row0 + i) * hb - 1, 0), c))
        hargs = [proj] * 5
        hspecs = [hspec(0), hspec(1), hspec(2), hspec(4), hspec(5)]
    else:
        hargs = list(halos)
        hspecs = [pl.BlockSpec((hr, da), lambda i: (0, 0)) for _ in range(5)]
    rows = ntiles * tm
    full = lambda a: pl.BlockSpec(a.shape, lambda i: (0, 0))
    return pl.pallas_call(
        functools.partial(_evenprep_kernel, stride=stride, hr=hr, tps=tps, zero_start=zero_start, nh=nh),
        out_shape=(jax.ShapeDtypeStruct((rows, da), F32), jax.ShapeDtypeStruct((rows, da), F32),
                   jax.ShapeDtypeStruct((rows, da), F32), jax.ShapeDtypeStruct((rows, V7X_LANES), F32),
                   jax.ShapeDtypeStruct((rows, da), BF16), jax.ShapeDtypeStruct((ntiles * tr, da), F32)),
        grid=(ntiles,),
        in_specs=[cur(0), cur(1), cur(2), cur(4), cur(5), cur(6),
                  pl.BlockSpec((tm, V7X_LANES), lambda i: (row0 + i, 0))] + hspecs + [full(wqkv), full(wb), full(adt)],
        out_specs=(pl.BlockSpec((tm, da), lambda i: (i, 0)), pl.BlockSpec((tm, da), lambda i: (i, 0)),
                   pl.BlockSpec((tm, da), lambda i: (i, 0)), pl.BlockSpec((tm, V7X_LANES), lambda i: (i, 0)),
                   pl.BlockSpec((tm, da), lambda i: (i, 0)), pl.BlockSpec((tr, da), lambda i: (i, 0))),
        scratch_shapes=[pltpu.VMEM((hr + tm, da), F32) for _ in range(4)],
        compiler_params=_params("parallel"),
        name="evenprep",
    )(proj, proj, proj, proj, proj, proj, ab, *hargs, wqkv, wb, adt)


def _delta_kernel(*refs, nh, has_s0):
    if has_s0:
        q_ref, k_ref, v_ref, gb_ref, s0_ref, o_ref, sout_ref, s_ref = refs
    else:
        q_ref, k_ref, v_ref, gb_ref, o_ref, sout_ref, s_ref = refs
    n = pl.program_id(1)
    nseq, c = q_ref.shape[0], q_ref.shape[1]

    @pl.when(n == 0)
    def _():
        s_ref[...] = s0_ref[0] if has_s0 else jnp.zeros(s_ref.shape, F32)

    row = lax.broadcasted_iota(I32, (c, c), 0)
    col = lax.broadcasted_iota(I32, (c, c), 1)
    causal = row >= col
    strict = row > col
    eye = (row == col).astype(F32)

    tri = causal.astype(BF16)
    gbs, gcums, gcum_ts = [], [], []
    for sq in range(nseq):
        gb = gb_ref[sq]
        g1 = gb.astype(BF16)
        r1 = gb - g1.astype(F32)
        g2 = r1.astype(BF16)
        g3 = (r1 - g2.astype(F32)).astype(BF16)
        gcum = _dot(tri, g1, _NN) + (_dot(tri, g2, _NN) + _dot(tri, g3, _NN))
        gbs.append(gb)
        gcums.append(gcum)
        gcum_ts.append(gcum.T)
    levels = int(math.log2(c))

    chains = [(sq, hh) for sq in range(nseq) for hh in range(nh)]
    heads = range(len(chains))
    lanes = [slice(hh * DK, (hh + 1) * DK) for _, hh in chains]
    qs = [q_ref[sq, :, lanes[h]] for h, (sq, _) in enumerate(chains)]
    ks = [k_ref[sq, :, lanes[h]] for h, (sq, _) in enumerate(chains)]
    g_col = [gcums[sq][:, hh:hh + 1] for sq, hh in chains]
    g_last = [gcums[sq][c - 1:c, hh:hh + 1] for sq, hh in chains]
    beta = [gbs[sq][:, nh + hh:nh + hh + 1] for sq, hh in chains]
    decay = [jnp.where(causal, jnp.exp(jnp.where(causal, g_col[h] - gcum_ts[sq][hh:hh + 1, :], 0.0)), 0.0)
             for h, (sq, hh) in enumerate(chains)]
    kb = [ks[h] * beta[h] for h in heads]
    khb = [ks[h].astype(BF16) for h in heads]
    low = [jnp.where(strict, _dot(kb[h].astype(BF16), khb[h], _NT) * decay[h], 0.0) for h in heads]
    inv = [eye - low[h] for h in heads]
    pw = [_split2(low[h]) for h in heads]
    for _ in range(levels - 1):
        pw = [_split2(_mm3s(pw[h], pw[h])) for h in heads]
        inv = [inv[h] + _mm3s(_split2(inv[h]), pw[h]) for h in heads]
    eg = [jnp.exp(g_col[h]) for h in heads]
    uw = [_mm3s(_split2(inv[h]),
                _split2(jnp.concatenate([v_ref[chains[h][0], :, lanes[h]] * beta[h], kb[h] * eg[h]], axis=1)))
          for h in heads]
    intra = [jnp.where(causal, _dot(qs[h].astype(BF16), khb[h], _NT) * decay[h], 0.0) for h in heads]
    s = [s_ref[sq, hh] for sq, hh in chains]
    sb = [s[h].astype(BF16) for h in heads]
    vnb = [(uw[h][:, :DV] - _dot(uw[h][:, DV:].astype(BF16), sb[h], _NN)).astype(BF16) for h in heads]
    for h, (sq, hh) in enumerate(chains):
        o_ref[sq, :, lanes[h]] = (_dot((qs[h] * eg[h]).astype(BF16), sb[h], _NN)
                                  + _dot(intra[h].astype(BF16), vnb[h], _NN))
    for h, (sq, hh) in enumerate(chains):
        k_dec = ks[h] * jnp.exp(g_last[h] - g_col[h])
        s_ref[sq, hh] = s[h] * jnp.exp(g_last[h]) + _dot(k_dec.astype(BF16), vnb[h], _TN)

    @pl.when(n == pl.num_programs(1) - 1)
    def _():
        sout_ref[...] = s_ref[...]


def _delta(q, k, v, gb, s0, li, nb, nchunks, c, nh, nseq):
    da = nh * DK
    assert nb % nseq == 0
    seqlen = nchunks * c
    view = lambda a: a.reshape(nb, seqlen, a.shape[1])
    rowspec = lambda w, col=0: pl.BlockSpec((nseq, c, w), lambda b, n: (b, n, col))
    sspec = pl.BlockSpec((nseq, nh, DK, DV), lambda b, n: (b, 0, 0, 0))
    if k is None:
        in_specs = [rowspec(da, 0), rowspec(da, 1), rowspec(da, 2), rowspec(V7X_LANES, 3 * da // V7X_LANES)]
        args = [view(q)] * 4
    else:
        in_specs = [rowspec(da), rowspec(da), rowspec(da), rowspec(V7X_LANES)]
        args = [view(q), view(k), view(v), view(gb)]
    if s0 is not None:
        in_specs.append(pl.BlockSpec((1, nseq, nh, DK, DV), lambda b, n: (li, b, 0, 0, 0)))
        args.append(s0)
    o, s_out = pl.pallas_call(
        functools.partial(_delta_kernel, nh=nh, has_s0=s0 is not None),
        out_shape=(jax.ShapeDtypeStruct((nb, seqlen, da), F32), jax.ShapeDtypeStruct((nb, nh, DK, DV), F32)),
        grid=(nb // nseq, nchunks),
        in_specs=in_specs,
        out_specs=(rowspec(da), sspec),
        scratch_shapes=[pltpu.VMEM((nseq, nh, DK, DV), F32)],
        compiler_params=_params("parallel", "arbitrary"),
        name="delta_rule",
    )(*args)
    return o.reshape(nb * seqlen, da), s_out


def _even_out_kernel(op_ref, os_ref, gout_ref, ybp_ref, ybs_ref, og_ref, mpg, msg, w_ref, x_ref, xo_ref, yin_ref,
                     *, ntp, reps, nh):
    i = pl.program_id(0)
    da = nh * DV

    @pl.when(pl.program_id(1) == 0)
    def _():
        is_s = i >= ntp
        for h in range(nh):
            sl = slice(h * DV, (h + 1) * DV)
            o = jnp.where(is_s, os_ref[:, sl], op_ref[:, sl])
            yin_ref[:, sl] = (_rms(o) * og_ref[...] * _silu(gout_ref[:, sl])).astype(BF16)
        yin_ref[:, da:] = jnp.where(is_s, ybs_ref[...], ybp_ref[...])

    y = jnp.dot(yin_ref[...], w_ref[0].astype(BF16), preferred_element_type=F32)
    xo_ref[...] = x_ref[...] + _tile_mod(mpg, msg, i >= ntp, reps) * y


def _even_out(o_p, o_s, proj, yb_p, yb_s, og, mp, ms, w_out, li, x, dims, nh):
    t, d = x.shape
    tm, ntp, tps, bp, bs, reps = dims
    da = nh * DV
    tn = min(1024, d)
    nj = d // tn
    mpg = pl.BlockSpec((1, 1, 1, tn), lambda i, j: (2, jnp.minimum(i // tps, bp - 1), 0, j))
    msg = pl.BlockSpec((1, bs, tn), lambda i, j: (2, 0, j))
    prompt_rows = lambda w: pl.BlockSpec((tm, w), lambda i, j: (jnp.minimum(i, ntp - 1), 0))
    sample_rows = lambda w: pl.BlockSpec((tm, w), lambda i, j: (0, 0))
    return pl.pallas_call(
        functools.partial(_even_out_kernel, ntp=ntp, reps=reps, nh=nh),
        out_shape=jax.ShapeDtypeStruct((t, d), F32),
        grid=(t // tm, nj),
        in_specs=[
            prompt_rows(da), sample_rows(da),
            pl.BlockSpec((tm, da), lambda i, j: (i, 3)),
            prompt_rows(yb_p.shape[1]), sample_rows(yb_s.shape[1]),
            pl.BlockSpec((1, DV), lambda i, j: (0, 0)),
            mpg, msg,
            pl.BlockSpec((1, w_out.shape[1], tn), lambda i, j: (li, 0, j)),
            pl.BlockSpec((tm, tn), lambda i, j: (i, j)),
        ],
        out_specs=pl.BlockSpec((tm, tn), lambda i, j: (i, j)),
        scratch_shapes=[pltpu.VMEM((tm, w_out.shape[1]), BF16)],
        compiler_params=_params("parallel", "arbitrary"),
        name="even_out",
    )(o_p, o_s, proj, yb_p, yb_s, og, mp, ms, w_out, x)


def _odd_in_kernel(hb_ref, wa_ref, wb_ref, ba_ref, bb_ref, u_ref, wab_ref, wbb_ref):
    @pl.when(pl.program_id(1) == 0)
    def _():
        wab_ref[...] = wa_ref[0].astype(BF16)
        wbb_ref[...] = wb_ref[0].astype(BF16)

    hb = hb_ref[...]
    a = jnp.dot(hb, wab_ref[...], preferred_element_type=F32) + ba_ref[0]
    b = jnp.dot(hb, wbb_ref[...], preferred_element_type=F32) + bb_ref[0]
    u_ref[...] = a * jax.nn.sigmoid(b)


def _odd_in(hb, w_in, b_in, li, tm):
    t, d = hb.shape
    dc = w_in.shape[2] // 2
    tn = min(512, dc)
    nj = dc // tn
    return pl.pallas_call(
        _odd_in_kernel,
        out_shape=jax.ShapeDtypeStruct((t, dc), F32),
        grid=(nj, t // tm),
        in_specs=[
            pl.BlockSpec((tm, d), lambda j, i: (i, 0)),
            pl.BlockSpec((1, d, tn), lambda j, i: (li, 0, j)),
            pl.BlockSpec((1, d, tn), lambda j, i: (li, 0, nj + j)),
            pl.BlockSpec((1, 1, tn), lambda j, i: (li, 0, j)),
            pl.BlockSpec((1, 1, tn), lambda j, i: (li, 0, nj + j)),
        ],
        out_specs=pl.BlockSpec((tm, tn), lambda j, i: (i, j)),
        scratch_shapes=[pltpu.VMEM((d, tn), BF16), pltpu.VMEM((d, tn), BF16)],
        compiler_params=_params("arbitrary", "arbitrary"),
        name="odd_in",
    )(hb, w_in, w_in, b_in, b_in)


def _dwconv_kernel(u_ref, halo_ref, w_ref, b_ref, z_ref, ext_ref, *, stride, hr, tps, zero_start):
    tm = u_ref.shape[0]
    keep = jnp.logical_not(jnp.logical_and(zero_start, pl.program_id(0) % tps == 0)).astype(F32)
    ext_ref[pl.ds(0, hr), :] = halo_ref[...] * keep
    ext_ref[pl.ds(hr, tm), :] = u_ref[...]
    w = w_ref[0]
    b = b_ref[0]
    rb = min(32, tm)

    def chunk(ci, carry):
        r0 = pl.multiple_of(ci * rb, rb)
        z_ref[pl.ds(r0, rb), :] = _causal_taps(ext_ref, r0, rb, slice(None), w, CONV_C, stride, hr) + b
        return carry

    lax.fori_loop(0, tm // rb, chunk, 0)


def _dwconv(u, halo, dw_w, dw_b, li, row0, ntiles, tm, stride, hr, tps, zero_start):
    dc = u.shape[1]
    cb = min(256, dc)
    if halo is None:
        hb = tm // hr
        harg = u
        hspec = pl.BlockSpec((hr, cb), lambda i, c: (jnp.maximum((row0 + i) * hb - 1, 0), c))
    else:
        harg = halo
        hspec = pl.BlockSpec((hr, cb), lambda i, c: (0, c))
    return pl.pallas_call(
        functools.partial(_dwconv_kernel, stride=stride, hr=hr, tps=tps, zero_start=zero_start),
        out_shape=jax.ShapeDtypeStruct((ntiles * tm, dc), F32),
        grid=(ntiles, dc // cb),
        in_specs=[
            pl.BlockSpec((tm, cb), lambda i, c: (row0 + i, c)),
            hspec,
            pl.BlockSpec((1, CONV_C, cb), lambda i, c: (li, 0, c)),
            pl.BlockSpec((1, 1, cb), lambda i, c: (li, 0, c)),
        ],
        out_specs=pl.BlockSpec((tm, cb), lambda i, c: (i, c)),
        scratch_shapes=[pltpu.VMEM((hr + tm, cb), F32)],
        compiler_params=_params("parallel", "parallel"),
        name="dwconv",
    )(u, harg, dw_w, dw_b)


def _odd_out_kernel(zp_ref, zsm_ref, lg_ref, lb_ref, mpg, msg, w_ref, b_ref, x_ref, xo_ref, zs_ref, *, ntp, reps):
    i = pl.program_id(0)

    @pl.when(pl.program_id(1) == 0)
    def _():
        z = jnp.where(i >= ntp, zsm_ref[...], zp_ref[...])
        zc = z - jnp.mean(z, axis=-1, keepdims=True)
        y = zc * lax.rsqrt(jnp.mean(zc * zc, axis=-1, keepdims=True) + EPS)
        zs_ref[...] = _silu(y * lg_ref[0] + lb_ref[0]).astype(BF16)

    y = jnp.dot(zs_ref[...], w_ref[0].astype(BF16), preferred_element_type=F32) + b_ref[0]
    xo_ref[...] = x_ref[...] + _tile_mod(mpg, msg, i >= ntp, reps) * y


def _odd_out(z_p, z_s, ln_g, ln_b, mp, ms, w_out, b_out, li, x, dims):
    t, d = x.shape
    tm, ntp, tps, bp, bs, reps = dims
    dc = z_p.shape[1]
    tn = min(1024, d)
    nj = d // tn
    mpg = pl.BlockSpec((1, 1, 1, tn), lambda i, j: (2, jnp.minimum(i // tps, bp - 1), 0, j))
    msg = pl.BlockSpec((1, bs, tn), lambda i, j: (2, 0, j))
    return pl.pallas_call(
        functools.partial(_odd_out_kernel, ntp=ntp, reps=reps),
        out_shape=jax.ShapeDtypeStruct((t, d), F32),
        grid=(t // tm, nj),
        in_specs=[
            pl.BlockSpec((tm, dc), lambda i, j: (jnp.minimum(i, ntp - 1), 0)),
            pl.BlockSpec((tm, dc), lambda i, j: (0, 0)),
            pl.BlockSpec((1, 1, dc), lambda i, j: (li, 0, 0)),
            pl.BlockSpec((1, 1, dc), lambda i, j: (li, 0, 0)),
            mpg, msg,
            pl.BlockSpec((1, dc, tn), lambda i, j: (li, 0, j)),
            pl.BlockSpec((1, 1, tn), lambda i, j: (li, 0, j)),
            pl.BlockSpec((tm, tn), lambda i, j: (i, j)),
        ],
        out_specs=pl.BlockSpec((tm, tn), lambda i, j: (i, j)),
        scratch_shapes=[pltpu.VMEM((tm, dc), BF16)],
        compiler_params=_params("parallel", "arbitrary"),
        name="odd_out",
    )(z_p, z_s, ln_g, ln_b, mp, ms, w_out, b_out, x)


def _ffnpre_kernel(x_ref, g_ref, mpsh, mpsc, mssh, mssc, wr_ref, br_ref,
                   h_ref, idx_ref, gate_ref, rank_ref, cnt_ref, carry_ref, *, ntp, reps):
    i = pl.program_id(0)
    tm = x_ref.shape[0]
    is_s = i >= ntp
    h = _rms(x_ref[...]) * g_ref[0] * (1 + _tile_mod(mpsc, mssc, is_s, reps)) + _tile_mod(mpsh, mssh, is_s, reps)
    h_ref[...] = _pack_bf16_pairs(h)
    logits = _mm(h, wr_ref[0]) + br_ref[0]

    @pl.when(i == 0)
    def _():
        carry_ref[...] = jnp.zeros(carry_ref.shape, F32)

    lane = lax.broadcasted_iota(I32, logits.shape, 1)
    work = logits
    sels, tops, picks = [], [], []
    for _ in range(TOP_K):
        m = jnp.max(work, axis=-1, keepdims=True)
        pick = jnp.min(jnp.where(work == m, lane, V7X_LANES), axis=-1, keepdims=True)
        sel = lane == pick
        work = jnp.where(sel, -jnp.inf, work)
        sels.append(sel)
        tops.append(m)
        picks.append(pick)
    exps = [jnp.exp(m - tops[0]) for m in tops]
    denom = exps[0]
    for e in exps[1:]:
        denom = denom + e

    onehot = sels[0]
    for s in sels[1:]:
        onehot = jnp.logical_or(onehot, s)
    onehot = onehot.astype(F32)
    r = lax.broadcasted_iota(I32, (tm, tm), 0)
    c = lax.broadcasted_iota(I32, (tm, tm), 1)
    before = _mm((r > c).astype(F32), onehot) + carry_ref[...]
    carry_ref[...] = carry_ref[...] + jnp.sum(onehot, axis=0, keepdims=True)

    idx_out = jnp.zeros(logits.shape, I32)
    gate_out = jnp.zeros(logits.shape, F32)
    rank_out = jnp.zeros(logits.shape, I32)
    for kk in range(TOP_K):
        rank = jnp.sum(jnp.where(sels[kk], before, 0.0), axis=-1, keepdims=True).astype(I32)
        idx_out = jnp.where(lane == kk, picks[kk], idx_out)
        gate_out = jnp.where(lane == kk, exps[kk] / denom, gate_out)
        rank_out = jnp.where(lane == kk, rank, rank_out)
    idx_ref[...] = idx_out
    gate_ref[...] = gate_out
    rank_ref[...] = rank_out
    cnt_ref[...] = carry_ref[...].astype(I32)


def _ffnpre(x, g, mp, ms, w_router, b_router, li, dims):
    t, d = x.shape
    tm, ntp, tps, bp, bs, reps = dims
    sh = _mod_specs(3, tps, bp, bs, d, 1)
    sc = _mod_specs(4, tps, bp, bs, d, 1)
    lanes = jax.ShapeDtypeStruct((t, V7X_LANES), I32)
    rowspec = pl.BlockSpec((tm, V7X_LANES), lambda i: (i, 0))
    return pl.pallas_call(
        functools.partial(_ffnpre_kernel, ntp=ntp, reps=reps),
        out_shape=(jax.ShapeDtypeStruct((t, d // 2), U32), lanes, jax.ShapeDtypeStruct((t, V7X_LANES), F32), lanes,
                   jax.ShapeDtypeStruct((1, V7X_LANES), I32)),
        grid=(t // tm,),
        in_specs=[
            pl.BlockSpec((tm, d), lambda i: (i, 0)),
            pl.BlockSpec((1, 1, d), lambda i: (li, 0, 0)),
            sh[0], sc[0], sh[1], sc[1],
            pl.BlockSpec((1, d, V7X_LANES), lambda i: (li, 0, 0)),
            pl.BlockSpec((1, 1, V7X_LANES), lambda i: (li, 0, 0)),
        ],
        out_specs=(pl.BlockSpec((tm, d // 2), lambda i: (i, 0)), rowspec, rowspec, rowspec,
                   pl.BlockSpec((1, V7X_LANES), lambda i: (0, 0))),
        scratch_shapes=[pltpu.VMEM((1, V7X_LANES), F32)],
        compiler_params=_params("arbitrary"),
        name="ffn_pre_router",
    )(x, g, mp, mp, ms, ms, w_router, b_router)


def _dispatch_kernel(rt_ref, nv_ref, h_ref, xs_ref, sem):
    i = pl.program_id(0)
    bm = xs_ref.shape[0]

    @pl.when(i < nv_ref[0])
    def _():
        def rcopy(r):
            return pltpu.make_async_copy(h_ref.at[pl.ds(rt_ref[i * bm + r], 1)], xs_ref.at[pl.ds(r, 1)], sem)

        def group(g, fn):
            for rr in range(4 * ROW_UNROLL):
                fn(rcopy(g * (4 * ROW_UNROLL) + rr), rr)

        ngroups = bm // (4 * ROW_UNROLL)

        def body(g, carry):
            group(g, lambda c, rr: c.start(priority=rr % 2))

            @pl.when(g > 0)
            def _():
                group(g - 1, lambda c, rr: c.wait())

            return carry

        lax.fori_loop(0, ngroups, body, 0)
        group(ngroups - 1, lambda c, rr: c.wait())


def _dispatch(row_tok, nvalid, h, bm):
    rows = row_tok.shape[0]
    d = h.shape[1]
    assert bm % (4 * ROW_UNROLL) == 0
    return pl.pallas_call(
        _dispatch_kernel,
        out_shape=jax.ShapeDtypeStruct((rows, d), h.dtype),
        grid_spec=pltpu.PrefetchScalarGridSpec(
            num_scalar_prefetch=2,
            grid=(rows // bm,),
            in_specs=[pl.BlockSpec(memory_space=pl.ANY)],
            out_specs=pl.BlockSpec((bm, d), lambda i, rt, nv: (jnp.minimum(i, nv[0] - 1), 0)),
            scratch_shapes=[pltpu.SemaphoreType.DMA(())],
        ),
        compiler_params=_params("arbitrary"),
        name="moe_dispatch",
    )(row_tok, nvalid, h)


def _weight_stream(be_ref, grp_ref, ge_ref, meta_ref, copies, install):
    j, i = pl.program_id(0), pl.program_id(1)
    nj = pl.num_programs(0)
    nv, ng = meta_ref[0], meta_ref[1]
    valid = i < nv
    first = jnp.logical_and(valid, jnp.logical_or(i == 0, be_ref[i] != be_ref[jnp.maximum(i - 1, 0)]))

    @pl.when(first)
    def _():
        grp = grp_ref[i]

        @pl.when(jnp.logical_and(j == 0, grp == 0))
        def _():
            for c in copies(j, be_ref[i]):
                c.start()

        for c in copies(j, be_ref[i]):
            c.wait()
        install()
        wrap = grp + 1 == ng
        nxt_j = jnp.where(wrap, j + 1, j)
        nxt_e = ge_ref[jnp.where(wrap, 0, grp + 1)]

        @pl.when(nxt_j < nj)
        def _():
            for c in copies(nxt_j, nxt_e):
                c.start()

    return valid


def _expert_up_kernel(be_ref, grp_ref, ge_ref, meta_ref, x_ref, w_ref, bg_ref, bu_ref, act_ref,
                      sg_ref, su_ref, wgb_ref, wub_ref, sem, *, li, f):
    tn = sg_ref.shape[1]

    def copies(j, e):
        col = pl.multiple_of(j * tn, tn)
        return (pltpu.make_async_copy(w_ref.at[li, e, :, pl.ds(col, tn)], sg_ref, sem.at[0]),
                pltpu.make_async_copy(w_ref.at[li, e, :, pl.ds(f + col, tn)], su_ref, sem.at[1]))

    def install():
        wgb_ref[...] = sg_ref[...].astype(BF16)
        wub_ref[...] = su_ref[...].astype(BF16)

    valid = _weight_stream(be_ref, grp_ref, ge_ref, meta_ref, copies, install)

    @pl.when(valid)
    def _():
        xb = _unpack_bf16_pairs(x_ref[...])
        g = jnp.dot(xb, wgb_ref[...], preferred_element_type=F32) + bg_ref[0, 0]
        u = jnp.dot(xb, wub_ref[...], preferred_element_type=F32) + bu_ref[0, 0]
        gate = jnp.minimum(g, SWIGLU_LIMIT)
        up = jnp.clip(u, -SWIGLU_LIMIT, SWIGLU_LIMIT)
        act_ref[...] = ((up + 1) * gate * jax.nn.sigmoid(SWIGLU_ALPHA * gate)).astype(BF16)


def _expert_up(route, xs, w_gate_up, b_gate_up, li, bm):
    rows, dh = xs.shape
    d = 2 * dh
    nblk = rows // bm
    f = w_gate_up.shape[3] // 2
    tn = min(EXPERT_COLS, f)
    nj = f // tn
    blk = lambda i, meta: jnp.minimum(i, meta[0] - 1)
    return pl.pallas_call(
        functools.partial(_expert_up_kernel, li=li, f=f),
        out_shape=jax.ShapeDtypeStruct((rows, f), BF16),
        grid_spec=pltpu.PrefetchScalarGridSpec(
            num_scalar_prefetch=4,
            grid=(nj, nblk),
            in_specs=[
                pl.BlockSpec((bm, dh), lambda j, i, be, grp, ge, meta: (blk(i, meta), 0)),
                pl.BlockSpec(memory_space=pl.ANY),
                pl.BlockSpec((1, 1, 1, tn), lambda j, i, be, grp, ge, meta: (li, be[blk(i, meta)], 0, j)),
                pl.BlockSpec((1, 1, 1, tn), lambda j, i, be, grp, ge, meta: (li, be[blk(i, meta)], 0, nj + j)),
            ],
            out_specs=pl.BlockSpec((bm, tn), lambda j, i, be, grp, ge, meta: (blk(i, meta), j)),
            scratch_shapes=[pltpu.VMEM((d, tn), F32), pltpu.VMEM((d, tn), F32),
                            pltpu.VMEM((d, tn), BF16), pltpu.VMEM((d, tn), BF16),
                            pltpu.SemaphoreType.DMA((2,))],
        ),
        compiler_params=_params("arbitrary", "arbitrary"),
        name="expert_gate_up",
    )(*route, xs, w_gate_up, b_gate_up, b_gate_up)


def _expert_down_kernel(be_ref, grp_ref, ge_ref, meta_ref, a_ref, w_ref, b_ref, y_ref, st_ref, wb_ref, sem, *, li):
    tn = st_ref.shape[1]

    def copies(j, e):
        col = pl.multiple_of(j * tn, tn)
        return (pltpu.make_async_copy(w_ref.at[li, e, :, pl.ds(col, tn)], st_ref, sem.at[0]),)

    def install():
        wb_ref[...] = st_ref[...].astype(BF16)

    valid = _weight_stream(be_ref, grp_ref, ge_ref, meta_ref, copies, install)

    @pl.when(valid)
    def _():
        y_ref[...] = jnp.dot(a_ref[...], wb_ref[...], preferred_element_type=F32) + b_ref[0, 0]


def _expert_down(route, act, w_down, b_down, li, bm):
    rows, f = act.shape
    nblk = rows // bm
    d = w_down.shape[3]
    tn = min(EXPERT_DOWN_COLS, d)
    nj = d // tn
    blk = lambda i, meta: jnp.minimum(i, meta[0] - 1)
    return pl.pallas_call(
        functools.partial(_expert_down_kernel, li=li),
        out_shape=jax.ShapeDtypeStruct((rows, d), F32),
        grid_spec=pltpu.PrefetchScalarGridSpec(
            num_scalar_prefetch=4,
            grid=(nj, nblk),
            in_specs=[
                pl.BlockSpec((bm, f), lambda j, i, be, grp, ge, meta: (blk(i, meta), 0)),
                pl.BlockSpec(memory_space=pl.ANY),
                pl.BlockSpec((1, 1, 1, tn), lambda j, i, be, grp, ge, meta: (li, be[blk(i, meta)], 0, j)),
            ],
            out_specs=pl.BlockSpec((bm, tn), lambda j, i, be, grp, ge, meta: (blk(i, meta), j)),
            scratch_shapes=[pltpu.VMEM((f, tn), F32), pltpu.VMEM((f, tn), BF16), pltpu.SemaphoreType.DMA((1,))],
        ),
        compiler_params=_params("arbitrary", "arbitrary"),
        name="expert_down",
    )(*route, act, w_down, b_down)


def _combine_kernel(slot_ref, gate_ref, mpg, msg, x_ref, ys_ref, xo_ref, buf_ref, sem, *, ntp, reps):
    i = pl.program_id(0)
    tm = x_ref.shape[0]

    def rcopy(t, kk):
        return pltpu.make_async_copy(ys_ref.at[pl.ds(slot_ref[(i * tm + t) * TOP_K + kk], 1)],
                                     buf_ref.at[kk, pl.ds(t, 1)], sem)

    def issue(g, carry):
        for tt in range(ROW_UNROLL):
            for kk in range(TOP_K):
                rcopy(g * ROW_UNROLL + tt, kk).start(priority=kk % 2)
        return carry

    lax.fori_loop(0, tm // ROW_UNROLL, issue, 0)

    def drain(g, carry):
        for tt in range(ROW_UNROLL):
            for kk in range(TOP_K):
                rcopy(g * ROW_UNROLL + tt, kk).wait()
        return carry

    lax.fori_loop(0, tm // ROW_UNROLL, drain, 0)

    gates = gate_ref[...]
    y = buf_ref[0] * gates[:, 0:1]
    for kk in range(1, TOP_K):
        y = y + buf_ref[kk] * gates[:, kk:kk + 1]
    xo_ref[...] = x_ref[...] + _tile_mod(mpg, msg, i >= ntp, reps) * y


def _combine(slots, gates, mp, ms, x, ys, dims):
    t, d = x.shape
    tm, ntp, tps, bp, bs, reps = dims
    mpg = pl.BlockSpec((1, 1, 1, d), lambda i, sl: (5, jnp.minimum(i // tps, bp - 1), 0, 0))
    msg = pl.BlockSpec((1, bs, d), lambda i, sl: (5, 0, 0))
    return pl.pallas_call(
        functools.partial(_combine_kernel, ntp=ntp, reps=reps),
        out_shape=jax.ShapeDtypeStruct((t, d), F32),
        grid_spec=pltpu.PrefetchScalarGridSpec(
            num_scalar_prefetch=1,
            grid=(t // tm,),
            in_specs=[
                pl.BlockSpec((tm, V7X_LANES), lambda i, sl: (i, 0)),
                mpg, msg,
                pl.BlockSpec((tm, d), lambda i, sl: (i, 0)),
                pl.BlockSpec(memory_space=pl.ANY),
            ],
            out_specs=pl.BlockSpec((tm, d), lambda i, sl: (i, 0)),
            scratch_shapes=[pltpu.VMEM((TOP_K, tm, d), F32), pltpu.SemaphoreType.DMA(())],
        ),
        compiler_params=_params("arbitrary"),
        name="moe_combine",
    )(slots, gates, mp, ms, x, ys)


def _moe(x, li, g, mp, ms, w_router_p, b_router_p, w_gate_up, b_gate_up4, w_down, b_down4, dims):
    t, d = x.shape
    nexp = w_gate_up.shape[1]
    bm = MOE_ROWS
    h, idx, gates, rank, cnt = _ffnpre(x, g, mp, ms, w_router_p, b_router_p, li, dims)
    counts = cnt[0, :nexp]
    padded = (counts + bm - 1) // bm * bm
    pad_end = jnp.cumsum(padded)
    pad_start = pad_end - padded
    idx4 = idx[:, :TOP_K]
    start4 = jnp.sum(jnp.where(idx4[..., None] == jnp.arange(nexp, dtype=I32), pad_start, 0), axis=-1)
    slots = (start4 + rank[:, :TOP_K]).reshape(t * TOP_K).astype(I32)
    nblk = -(-(t * TOP_K + nexp * (bm - 1)) // bm)
    rows = nblk * bm
    blk_start = jnp.arange(nblk, dtype=I32) * bm
    blk_expert = jnp.minimum(jnp.sum(pad_end[None, :] <= blk_start[:, None], axis=1), nexp - 1).astype(I32)
    has = counts > 0
    run_of_expert = jnp.cumsum(has.astype(I32)) - 1
    blk_run = jnp.sum(jnp.where(blk_expert[:, None] == jnp.arange(nexp, dtype=I32), run_of_expert, 0), axis=1)
    run_expert = jnp.argsort(jnp.logical_not(has), stable=True).astype(I32)
    meta = jnp.stack([pad_end[-1] // bm, jnp.sum(has.astype(I32))]).astype(I32)
    route = (blk_expert, blk_run.astype(I32), run_expert, meta)
    row_tok = jnp.zeros((rows,), I32).at[slots].set(jnp.arange(t * TOP_K, dtype=I32) // TOP_K,
                                                    unique_indices=True)
    xs = _dispatch(row_tok, meta[:1], h, bm)
    act = _expert_up(route, xs, w_gate_up, b_gate_up4, li, bm)
    ys = _expert_down(route, act, w_down, b_down4, li, bm)
    return _combine(slots, gates, mp, ms, x, ys, dims)


def _final_kernel(x_ref, g_ref, yp_ref, ys_ref, *, ntp):
    y = _rms(x_ref[...]) * g_ref[...]
    i = pl.program_id(0)

    @pl.when(i < ntp)
    def _():
        yp_ref[...] = y

    @pl.when(i >= ntp)
    def _():
        ys_ref[...] = y


def _final_norm(x, g, tm, ntp):
    t, d = x.shape
    return pl.pallas_call(
        functools.partial(_final_kernel, ntp=ntp),
        out_shape=(jax.ShapeDtypeStruct((ntp * tm, d), F32), jax.ShapeDtypeStruct((t - ntp * tm, d), F32)),
        grid=(t // tm,),
        in_specs=[pl.BlockSpec((tm, d), lambda i: (i, 0)), pl.BlockSpec((1, d), lambda i: (0, 0))],
        out_specs=(pl.BlockSpec((tm, d), lambda i: (jnp.minimum(i, ntp - 1), 0)),
                   pl.BlockSpec((tm, d), lambda i: (0, 0))),
        compiler_params=_params("arbitrary"),
        name="final_norm",
    )(x, g)


def _to_pos_major(a):
    b, l, c = a.shape
    return jnp.swapaxes(a, 0, 1).reshape(l * b, c)


def _seq_tails(a, nseq, seqlen, n, width):
    return jnp.stack([a[(b + 1) * seqlen - n:(b + 1) * seqlen, :width] for b in range(nseq)])


def _from_pos_major(a, b):
    lb, c = a.shape
    return jnp.swapaxes(a.reshape(lb // b, b, c), 0, 1)


def kernel(x_prompt, x_sample, state_delta, state_conv_qkv, state_conv_b, state_conv_c, c_prompt, c_sample, norm_mix, norm_ffn, w_mod, b_mod, w_in_even, conv_qkv_w, a_log, dt_bias, o_norm_g, conv_b_w, w_out_even, w_in_odd, b_in_odd, dw_w, dw_b, ln_g, ln_b, w_out_odd, b_out_odd, w_router, b_router, w_gate_up, b_gate_up, w_down, b_down, norm_final):
    bp, seq, d = x_prompt.shape
    bs, dseq, _ = x_sample.shape
    depth = w_mod.shape[0]
    nh = a_log.shape[1]
    da = nh * DV
    db = conv_b_w.shape[2]
    dc = dw_w.shape[2]
    nexp = w_router.shape[2]
    tp, ts = bp * seq, bs * dseq
    tm = ts
    assert seq % tm == 0 and bs % V7X_SUBLANES == 0 and da == db and dseq >= CONV_A - 1
    assert nexp <= V7X_LANES and 2 * nh <= V7X_LANES
    tps = seq // tm
    ntp = tp // tm
    dims = (tm, ntp, tps, bp, bs, dseq)
    chunk = math.gcd(seq, CHUNK_A)

    x = jnp.concatenate([x_prompt.reshape(tp, d), _to_pos_major(x_sample)], axis=0)
    rc = -(-(bp + bs) // V7X_SUBLANES) * V7X_SUBLANES
    c_all = jnp.zeros((rc, d), F32).at[:bp].set(c_prompt).at[bp:bp + bs].set(c_sample)
    mod = _modulation(c_all, w_mod, b_mod)

    w_router_p = jnp.zeros((depth, d, V7X_LANES), F32).at[:, :, :nexp].set(w_router)
    b_router_p = jnp.full((depth, 1, V7X_LANES), -jnp.inf, F32).at[:, 0, :nexp].set(b_router)
    b_gate_up4 = b_gate_up.reshape(depth, nexp, 1, b_gate_up.shape[2])
    b_down4 = b_down.reshape(depth, nexp, 1, d)

    qkv_w = 3 * nh * DK
    c_ab = qkv_w + da
    c_b = c_ab + 2 * nh
    hr_qkv = (CONV_A - 1) * bs
    hr_c = (CONV_C - 1) * bs
    outs = {k: [] for k in ("dp", "qp", "bp", "cp", "ds", "qs", "bs", "cs")}

    for l in range(depth):
        mp = mod[l][:, :bp].reshape(6, bp, 1, d)
        ms = mod[l][:, bp:bp + bs]
        li = l // 2
        if l % 2 == 0:
            w = w_in_even[li]
            w_main = jnp.concatenate([w[:, :c_ab], w[:, c_b:]], axis=1)
            w_ab = jnp.zeros((d, V7X_LANES), F32).at[:, :2 * nh].set(w[:, c_ab:c_b])
            hb, ab = _prenorm(x, norm_mix[l:l + 1], mp, ms, w_ab, dims)
            proj = _even_in(hb, w_main, tm)
            adt = jnp.zeros((V7X_SUBLANES, V7X_LANES), F32).at[0, :nh].set(a_log[li]).at[1, :nh].set(dt_bias[li])
            wqkv, wb = conv_qkv_w[li], conv_b_w[li]
            qp, kp, vp, gbp, ybp, utp = _evenprep(proj, ab, None, wqkv, wb, adt, 0, ntp, tm, 1, V7X_SUBLANES,
                                                  tps, True, nh)
            sq = _to_pos_major(state_conv_qkv[li])
            su = jnp.concatenate([jnp.zeros(((CONV_A - CONV_B) * bs, db), F32), _to_pos_major(state_conv_b[li])], axis=0)
            halos = (sq[:, :da], sq[:, da:2 * da], sq[:, 2 * da:], su, jnp.ones_like(su))
            qs, ks, vs, gbs, ybs, uts = _evenprep(proj, ab, halos, wqkv, wb, adt, ntp, 1, tm, bs, hr_qkv,
                                                  1, False, nh)
            o_p, s_p = _delta(qp, kp, vp, gbp, None, li, bp, seq // chunk, chunk, nh, math.gcd(bp, DELTA_SEQS_PROMPT))

            qkvg = _from_pos_major(jnp.concatenate([qs, ks, vs, gbs], axis=1), bs)
            qkvg = jnp.pad(qkvg, ((0, 0), (0, V7X_SUBLANES - dseq), (0, 0))).reshape(bs * V7X_SUBLANES, -1)
            o_s8, s_s = _delta(qkvg, None, None, None, state_delta, li,
                               bs, 1, V7X_SUBLANES, nh, math.gcd(bs, DELTA_SEQS_SAMPLE))
            o_s = _to_pos_major(o_s8.reshape(bs, V7X_SUBLANES, da)[:, :dseq])
            x = _even_out(o_p, o_s, proj, ybp, ybs, o_norm_g[li:li + 1], mp, ms, w_out_even, li, x, dims, nh)

            outs["dp"].append(s_p)
            outs["qp"].append(_seq_tails(proj, bp, seq, CONV_A - 1, qkv_w))
            outs["bp"].append(utp.reshape(bp, tps, V7X_SUBLANES, db)[:, tps - 1, V7X_SUBLANES - (CONV_B - 1):])
            outs["ds"].append(s_s)
            outs["qs"].append(_from_pos_major(proj[tp + (dseq - (CONV_A - 1)) * bs:, :qkv_w], bs))
            u_ext = jnp.concatenate([state_conv_b[li], _from_pos_major(uts, bs)], axis=1)
            outs["bs"].append(u_ext[:, u_ext.shape[1] - (CONV_B - 1):])
        else:
            (hb,) = _prenorm(x, norm_mix[l:l + 1], mp, ms, None, dims)
            u = _odd_in(hb, w_in_odd, b_in_odd.reshape(-1, 1, 2 * dc), li, tm)
            dwb = dw_b.reshape(-1, 1, dc)
            zp = _dwconv(u, None, dw_w, dwb, li, 0, ntp, tm, 1, 32, tps, True)
            zs = _dwconv(u, _to_pos_major(state_conv_c[li]), dw_w, dwb, li, ntp, 1, tm, bs, hr_c, 1, False)
            x = _odd_out(zp, zs, ln_g.reshape(-1, 1, dc), ln_b.reshape(-1, 1, dc), mp, ms, w_out_odd,
                         b_out_odd.reshape(-1, 1, d), li, x, dims)
            outs["cp"].append(_seq_tails(u, bp, seq, CONV_C - 1, dc))
            c_ext = jnp.concatenate([state_conv_c[li], _from_pos_major(u[tp:], bs)], axis=1)
            outs["cs"].append(c_ext[:, c_ext.shape[1] - (CONV_C - 1):])
        x = _moe(x, l, norm_ffn.reshape(depth, 1, d), mp, ms, w_router_p, b_router_p, w_gate_up, b_gate_up4,
                 w_down, b_down4, dims)

    y_p, y_s = _final_norm(x, norm_final.reshape(1, d), tm, ntp)
    y_prompt = y_p.reshape(bp, seq, d)
    y_sample = _from_pos_major(y_s, bs)
    return (y_prompt, y_sample,
            jnp.stack(outs["dp"]), jnp.stack(outs["qp"]), jnp.stack(outs["bp"]), jnp.stack(outs["cp"]),
            jnp.stack(outs["ds"]), jnp.stack(outs["qs"]), jnp.stack(outs["bs"]), jnp.stack(outs["cs"]))
```

```python
import functools
import math

import jax
import jax.numpy as jnp
from jax import lax
from jax.experimental import pallas as pl
from jax.experimental.pallas import tpu as pltpu

F32 = jnp.float32
BF16 = jnp.bfloat16
I32 = jnp.int32
U32 = jnp.uint32
EPS = 1e-6
HI = lax.Precision.HIGHEST

DK = 128
DV = 128
CONV_A = 4
CONV_B = 3
CONV_C = 31
CHUNK_A = 64
TOP_K = 4
SWIGLU_ALPHA = 1.702
SWIGLU_LIMIT = 7.0

V7X_LANES = 128
V7X_SUBLANES = 8
V7X_VMEM_LIMIT_BYTES = 56 * 1024 * 1024
MOE_ROWS = 256
EXPERT_COLS = 1024
EXPERT_DOWN_COLS = 2048
DELTA_SEQS_PROMPT = 2
DELTA_SEQS_SAMPLE = 4
ROW_UNROLL = 8


def _params(*sem):
    return pltpu.CompilerParams(dimension_semantics=sem, vmem_limit_bytes=V7X_VMEM_LIMIT_BYTES)


def _mm(a, b):
    return jnp.dot(a.astype(BF16), b.astype(BF16), preferred_element_type=F32)


def _mm_hi(a, b):
    return jnp.dot(a, b, precision=HI, preferred_element_type=F32)


_NN = (((1,), (0,)), ((), ()))
_NT = (((1,), (1,)), ((), ()))
_TN = (((0,), (0,)), ((), ()))


def _dot(a, b, dims):
    return lax.dot_general(a, b, dims, preferred_element_type=F32)


def _split2(a):
    hi = a.astype(BF16)
    return hi, (a - hi.astype(F32)).astype(BF16)


def _mm3s(a, b):
    return _dot(a[0], b[0], _NN) + (_dot(a[0], b[1], _NN) + _dot(a[1], b[0], _NN))


def _pack_bf16_pairs(x):
    half = x.shape[1] // 2
    bits = lax.bitcast_convert_type(x.astype(BF16).astype(F32), U32)
    return (bits[:, half:] & jnp.uint32(0xFFFF0000)) | (bits[:, :half] >> 16)


def _unpack_bf16_pairs(w):
    lo = lax.bitcast_convert_type(w << 16, F32)
    hi = lax.bitcast_convert_type(w & jnp.uint32(0xFFFF0000), F32)
    return jnp.concatenate([lo, hi], axis=1).astype(BF16)


def _silu(x):
    return x * jax.nn.sigmoid(x)


def _rms(x):
    return x * lax.rsqrt(jnp.mean(x * x, axis=-1, keepdims=True) + EPS)


def _tile_mod(mp_ref, ms_ref, is_sample, reps):
    s = ms_ref[0]
    s = jnp.concatenate([s] * reps, axis=0)
    return jnp.where(is_sample, s, mp_ref[0, 0])


def _mod_specs(k, tps, bp, bs, d, ngrid):
    if ngrid == 1:
        mp = pl.BlockSpec((1, 1, 1, d), lambda i: (k, jnp.minimum(i // tps, bp - 1), 0, 0))
        ms = pl.BlockSpec((1, bs, d), lambda i: (k, 0, 0))
    else:
        mp = pl.BlockSpec((1, 1, 1, d), lambda i, j: (k, jnp.minimum(i // tps, bp - 1), 0, 0))
        ms = pl.BlockSpec((1, bs, d), lambda i, j: (k, 0, 0))
    return mp, ms


def _mod_kernel(c_ref, w_ref, b_ref, o_ref):
    o_ref[0, 0] = _mm(_silu(c_ref[...]), w_ref[0]) + b_ref[0]


def _modulation(c_all, w_mod, b_mod):
    depth, d, d6 = w_mod.shape
    rc = c_all.shape[0]
    tn = min(512, d)
    nj = d // tn
    return pl.pallas_call(
        _mod_kernel,
        out_shape=jax.ShapeDtypeStruct((depth, 6, rc, d), F32),
        grid=(depth, 6, nj),
        in_specs=[
            pl.BlockSpec((rc, d), lambda l, k, j: (0, 0)),
            pl.BlockSpec((1, d, tn), lambda l, k, j: (l, 0, k * nj + j)),
            pl.BlockSpec((1, 1, tn), lambda l, k, j: (l, 0, k * nj + j)),
        ],
        out_specs=pl.BlockSpec((1, 1, rc, tn), lambda l, k, j: (l, k, 0, j)),
        compiler_params=_params("parallel", "parallel", "parallel"),
        name="modulation",
    )(c_all, w_mod, b_mod.reshape(depth, 1, d6))


def _prenorm_kernel(*refs, ntp, reps, has_ab):
    if has_ab:
        x_ref, g_ref, mpsh, mpsc, mssh, mssc, wab_ref, hb_ref, ab_ref = refs
    else:
        x_ref, g_ref, mpsh, mpsc, mssh, mssc, hb_ref = refs
    is_s = pl.program_id(0) >= ntp
    h = _rms(x_ref[...]) * g_ref[...] * (1 + _tile_mod(mpsc, mssc, is_s, reps)) + _tile_mod(mpsh, mssh, is_s, reps)
    hb = h.astype(BF16)
    hb_ref[...] = hb
    if has_ab:
        ab_ref[...] = jnp.dot(hb, wab_ref[...].astype(BF16), preferred_element_type=F32)


def _prenorm(x, g, mp, ms, w_ab, dims):
    t, d = x.shape
    tm, ntp, tps, bp, bs, reps = dims
    sh = _mod_specs(0, tps, bp, bs, d, 1)
    sc = _mod_specs(1, tps, bp, bs, d, 1)
    in_specs = [pl.BlockSpec((tm, d), lambda i: (i, 0)), pl.BlockSpec((1, d), lambda i: (0, 0)),
                sh[0], sc[0], sh[1], sc[1]]
    args = [x, g, mp, mp, ms, ms]
    out_shape = [jax.ShapeDtypeStruct((t, d), BF16)]
    out_specs = [pl.BlockSpec((tm, d), lambda i: (i, 0))]
    if w_ab is not None:
        in_specs.append(pl.BlockSpec((d, V7X_LANES), lambda i: (0, 0)))
        args.append(w_ab)
        out_shape.append(jax.ShapeDtypeStruct((t, V7X_LANES), F32))
        out_specs.append(pl.BlockSpec((tm, V7X_LANES), lambda i: (i, 0)))
    return pl.pallas_call(
        functools.partial(_prenorm_kernel, ntp=ntp, reps=reps, has_ab=w_ab is not None),
        out_shape=tuple(out_shape),
        grid=(t // tm,),
        in_specs=in_specs,
        out_specs=tuple(out_specs),
        compiler_params=_params("parallel"),
        name="prenorm",
    )(*args)


def _even_in_kernel(hb_ref, w_ref, proj_ref, wb_ref):
    @pl.when(pl.program_id(1) == 0)
    def _():
        wb_ref[...] = w_ref[...].astype(BF16)

    proj_ref[...] = jnp.dot(hb_ref[...], wb_ref[...], preferred_element_type=F32)


def _even_in(hb, w_main, tm):
    t, d = hb.shape
    nmain = w_main.shape[1]
    tn = nmain // 7
    return pl.pallas_call(
        _even_in_kernel,
        out_shape=jax.ShapeDtypeStruct((t, nmain), F32),
        grid=(7, t // tm),
        in_specs=[pl.BlockSpec((tm, d), lambda j, i: (i, 0)), pl.BlockSpec((d, tn), lambda j, i: (0, j))],
        out_specs=pl.BlockSpec((tm, tn), lambda j, i: (i, j)),
        scratch_shapes=[pltpu.VMEM((d, tn), BF16)],
        compiler_params=_params("arbitrary", "arbitrary"),
        name="even_in",
    )(hb, w_main)


def _causal_taps(ext_ref, r0, rb, lanes, w, width, stride, hr):
    acc = None
    if stride % V7X_SUBLANES == 0:
        for j in range(width):
            start = pl.multiple_of(r0 + (hr - (width - 1 - j) * stride), V7X_SUBLANES)
            term = ext_ref[pl.ds(start, rb), lanes] * w[j:j + 1, :]
            acc = term if acc is None else acc + term
        return acc
    look = -(-(width - 1) * stride // V7X_SUBLANES) * V7X_SUBLANES
    nrow = rb + look
    win = ext_ref[pl.ds(pl.multiple_of(r0 + (hr - look), V7X_SUBLANES), nrow), lanes]
    for res in range(V7X_SUBLANES):
        taps = [j for j in range(width) if (look - (width - 1 - j) * stride) % V7X_SUBLANES == res]
        if not taps:
            continue
        shifted = win if res == 0 else pltpu.roll(win, nrow - res, axis=0)
        for j in taps:
            off = look - (width - 1 - j) * stride - res
            term = shifted[off:off + rb, :] * w[j:j + 1, :]
            acc = term if acc is None else acc + term
    return acc


def _evenprep_kernel(q_ref, k_ref, v_ref, xb_ref, cp_ref, bpost_ref, ab_ref,
                     hq_ref, hk_ref, hv_ref, hxb_ref, hcp_ref,
                     wqkv_ref, wb_ref, adt_ref,
                     qo_ref, ko_ref, vo_ref, gb_ref, yb_ref, ut_ref,
                     extq, extk, extv, extu, *, stride, hr, tps, zero_start, nh):
    tm = q_ref.shape[0]
    da = q_ref.shape[1]
    keep = jnp.logical_not(jnp.logical_and(zero_start, pl.program_id(0) % tps == 0)).astype(F32)
    for ext, halo, cur in ((extq, hq_ref, q_ref), (extk, hk_ref, k_ref), (extv, hv_ref, v_ref)):
        ext[pl.ds(0, hr), :] = halo[...] * keep
        ext[pl.ds(hr, tm), :] = cur[...]
    extu[pl.ds(0, hr), :] = hcp_ref[...] * hxb_ref[...] * keep
    extu[pl.ds(hr, tm), :] = cp_ref[...] * xb_ref[...]
    tr = ut_ref.shape[0]
    ut_ref[...] = extu[pl.ds(hr + tm - tr, tr), :]

    ab = ab_ref[...]
    adt = adt_ref[...]
    z = ab + adt[1:2, :]
    softplus = jnp.maximum(z, 0.0) + jnp.log(1.0 + jnp.exp(-jnp.abs(z)))
    lane = lax.broadcasted_iota(I32, ab.shape, 1)
    gb_ref[...] = jnp.where(lane < nh, -jnp.exp(adt[0:1, :]) * softplus, jax.nn.sigmoid(ab))

    wqkv = wqkv_ref[...]
    wb = wb_ref[...]
    rb = min(64, tm)

    def chunk(ci, carry):
        r0 = pl.multiple_of(ci * rb, rb)
        rows = pl.ds(r0, rb)
        for h in range(nh):
            sl = slice(h * DK, (h + 1) * DK)
            taps = lambda ext, w0: _causal_taps(ext, r0, rb, sl, wqkv[:, w0 + h * DK:w0 + (h + 1) * DK],
                                                CONV_A, stride, hr)
            qh = _silu(taps(extq, 0))
            kh = _silu(taps(extk, da))
            qo_ref[rows, sl] = qh * lax.rsqrt(jnp.sum(qh * qh, axis=-1, keepdims=True) + EPS) * (DK ** -0.5)
            ko_ref[rows, sl] = kh * lax.rsqrt(jnp.sum(kh * kh, axis=-1, keepdims=True) + EPS)
            vo_ref[rows, sl] = _silu(taps(extv, 2 * da))
            yb = bpost_ref[rows, sl] * _causal_taps(extu, r0, rb, sl, wb[:, sl], CONV_B, stride, hr)
            yb_ref[rows, sl] = yb.astype(BF16)
        return carry

    lax.fori_loop(0, tm // rb, chunk, 0)


def _evenprep(proj, ab, halos, wqkv, wb, adt, row0, ntiles, tm, stride, hr, tps, zero_start, nh):
    da = nh * DK
    tr = max(V7X_SUBLANES, (CONV_B - 1) * stride)
    cur = lambda c: pl.BlockSpec((tm, da), lambda i: (row0 + i, c))
    if halos is None:
        hb = tm // hr
        hspec = lambda c: pl.BlockSpec((hr, da), lambda i: (jnp.maximum((row0 + i) * hb - 1, 0), c))
        hargs = [proj] * 5
        hspecs = [hspec(0), hspec(1), hspec(2), hspec(4), hspec(5)]
    else:
        hargs = list(halos)
        hspecs = [pl.BlockSpec((hr, da), lambda i: (0, 0)) for _ in range(5)]
    rows = ntiles * tm
    full = lambda a: pl.BlockSpec(a.shape, lambda i: (0, 0))
    return pl.pallas_call(
        functools.partial(_evenprep_kernel, stride=stride, hr=hr, tps=tps, zero_start=zero_start, nh=nh),
        out_shape=(jax.ShapeDtypeStruct((rows, da), F32), jax.ShapeDtypeStruct((rows, da), F32),
                   jax.ShapeDtypeStruct((rows, da), F32), jax.ShapeDtypeStruct((rows, V7X_LANES), F32),
                   jax.ShapeDtypeStruct((rows, da), BF16), jax.ShapeDtypeStruct((ntiles * tr, da), F32)),
        grid=(ntiles,),
        in_specs=[cur(0), cur(1), cur(2), cur(4), cur(5), cur(6),
                  pl.BlockSpec((tm, V7X_LANES), lambda i: (row0 + i, 0))] + hspecs + [full(wqkv), full(wb), full(adt)],
        out_specs=(pl.BlockSpec((tm, da), lambda i: (i, 0)), pl.BlockSpec((tm, da), lambda i: (i, 0)),
                   pl.BlockSpec((tm, da), lambda i: (i, 0)), pl.BlockSpec((tm, V7X_LANES), lambda i: (i, 0)),
                   pl.BlockSpec((tm, da), lambda i: (i, 0)), pl.BlockSpec((tr, da), lambda i: (i, 0))),
        scratch_shapes=[pltpu.VMEM((hr + tm, da), F32) for _ in range(4)],
        compiler_params=_params("parallel"),
        name="evenprep",
    )(proj, proj, proj, proj, proj, proj, ab, *hargs, wqkv, wb, adt)


def _delta_kernel(*refs, nh, has_s0):
    if has_s0:
        q_ref, k_ref, v_ref, gb_ref, s0_ref, o_ref, sout_ref, s_ref = refs
    else:
        q_ref, k_ref, v_ref, gb_ref, o_ref, sout_ref, s_ref = refs
    n = pl.program_id(1)
    nseq, c = q_ref.shape[0], q_ref.shape[1]

    @pl.when(n == 0)
    def _():
        s_ref[...] = s0_ref[0] if has_s0 else jnp.zeros(s_ref.shape, F32)

    row = lax.broadcasted_iota(I32, (c, c), 0)
    col = lax.broadcasted_iota(I32, (c, c), 1)
    causal = row >= col
    strict = row > col
    eye = (row == col).astype(F32)

    tri = causal.astype(BF16)
    gbs, gcums, gcum_ts = [], [], []
    for sq in range(nseq):
        gb = gb_ref[sq]
        g1 = gb.astype(BF16)
        r1 = gb - g1.astype(F32)
        g2 = r1.astype(BF16)
        g3 = (r1 - g2.astype(F32)).astype(BF16)
        gcum = _dot(tri, g1, _NN) + (_dot(tri, g2, _NN) + _dot(tri, g3, _NN))
        gbs.append(gb)
        gcums.append(gcum)
        gcum_ts.append(gcum.T)
    levels = int(math.log2(c))

    chains = [(sq, hh) for sq in range(nseq) for hh in range(nh)]
    heads = range(len(chains))
    lanes = [slice(hh * DK, (hh + 1) * DK) for _, hh in chains]
    qs = [q_ref[sq, :, lanes[h]] for h, (sq, _) in enumerate(chains)]
    ks = [k_ref[sq, :, lanes[h]] for h, (sq, _) in enumerate(chains)]
    g_col = [gcums[sq][:, hh:hh + 1] for sq, hh in chains]
    g_last = [gcums[sq][c - 1:c, hh:hh + 1] for sq, hh in chains]
    beta = [gbs[sq][:, nh + hh:nh + hh + 1] for sq, hh in chains]
    decay = [jnp.where(causal, jnp.exp(jnp.where(causal, g_col[h] - gcum_ts[sq][hh:hh + 1, :], 0.0)), 0.0)
             for h, (sq, hh) in enumerate(chains)]
    kb = [ks[h] * beta[h] for h in heads]
    khb = [ks[h].astype(BF16) for h in heads]
    low = [jnp.where(strict, _dot(kb[h].astype(BF16), khb[h], _NT) * decay[h], 0.0) for h in heads]
    inv = [eye - low[h] for h in heads]
    pw = [_split2(low[h]) for h in heads]
    for _ in range(levels - 1):
        pw = [_split2(_mm3s(pw[h], pw[h])) for h in heads]
        inv = [inv[h] + _mm3s(_split2(inv[h]), pw[h]) for h in heads]
    eg = [jnp.exp(g_col[h]) for h in heads]
    uw = [_mm3s(_split2(inv[h]),
                _split2(jnp.concatenate([v_ref[chains[h][0], :, lanes[h]] * beta[h], kb[h] * eg[h]], axis=1)))
          for h in heads]
    intra = [jnp.where(causal, _dot(qs[h].astype(BF16), khb[h], _NT) * decay[h], 0.0) for h in heads]
    s = [s_ref[sq, hh] for sq, hh in chains]
    sb = [s[h].astype(BF16) for h in heads]
    vnb = [(uw[h][:, :DV] - _dot(uw[h][:, DV:].astype(BF16), sb[h], _NN)).astype(BF16) for h in heads]
    for h, (sq, hh) in enumerate(chains):
        o_ref[sq, :, lanes[h]] = (_dot((qs[h] * eg[h]).astype(BF16), sb[h], _NN)
                                  + _dot(intra[h].astype(BF16), vnb[h], _NN))
    for h, (sq, hh) in enumerate(chains):
        k_dec = ks[h] * jnp.exp(g_last[h] - g_col[h])
        s_ref[sq, hh] = s[h] * jnp.exp(g_last[h]) + _dot(k_dec.astype(BF16), vnb[h], _TN)

    @pl.when(n == pl.num_programs(1) - 1)
    def _():
        sout_ref[...] = s_ref[...]


def _delta(q, k, v, gb, s0, li, nb, nchunks, c, nh, nseq):
    da = nh * DK
    assert nb % nseq == 0
    seqlen = nchunks * c
    view = lambda a: a.reshape(nb, seqlen, a.shape[1])
    rowspec = lambda w, col=0: pl.BlockSpec((nseq, c, w), lambda b, n: (b, n, col))
    sspec = pl.BlockSpec((nseq, nh, DK, DV), lambda b, n: (b, 0, 0, 0))
    if k is None:
        in_specs = [rowspec(da, 0), rowspec(da, 1), rowspec(da, 2), rowspec(V7X_LANES, 3 * da // V7X_LANES)]
        args = [view(q)] * 4
    else:
        in_specs = [rowspec(da), rowspec(da), rowspec(da), rowspec(V7X_LANES)]
        args = [view(q), view(k), view(v), view(gb)]
    if s0 is not None:
        in_specs.append(pl.BlockSpec((1, nseq, nh, DK, DV), lambda b, n: (li, b, 0, 0, 0)))
        args.append(s0)
    o, s_out = pl.pallas_call(
        functools.partial(_delta_kernel, nh=nh, has_s0=s0 is not None),
        out_shape=(jax.ShapeDtypeStruct((nb, seqlen, da), F32), jax.ShapeDtypeStruct((nb, nh, DK, DV), F32)),
        grid=(nb // nseq, nchunks),
        in_specs=in_specs,
        out_specs=(rowspec(da), sspec),
        scratch_shapes=[pltpu.VMEM((nseq, nh, DK, DV), F32)],
        compiler_params=_params("parallel", "arbitrary"),
        name="delta_rule",
    )(*args)
    return o.reshape(nb * seqlen, da), s_out


def _even_out_kernel(op_ref, os_ref, gout_ref, ybp_ref, ybs_ref, og_ref, mpg, msg, w_ref, x_ref, xo_ref, yin_ref,
                     *, ntp, reps, nh):
    i = pl.program_id(0)
    da = nh * DV

    @pl.when(pl.program_id(1) == 0)
    def _():
        is_s = i >= ntp
        for h in range(nh):
            sl = slice(h * DV, (h + 1) * DV)
            o = jnp.where(is_s, os_ref[:, sl], op_ref[:, sl])
            yin_ref[:, sl] = (_rms(o) * og_ref[...] * _silu(gout_ref[:, sl])).astype(BF16)
        yin_ref[:, da:] = jnp.where(is_s, ybs_ref[...], ybp_ref[...])

    y = jnp.dot(yin_ref[...], w_ref[0].astype(BF16), preferred_element_type=F32)
    xo_ref[...] = x_ref[...] + _tile_mod(mpg, msg, i >= ntp, reps) * y


def _even_out(o_p, o_s, proj, yb_p, yb_s, og, mp, ms, w_out, li, x, dims, nh):
    t, d = x.shape
    tm, ntp, tps, bp, bs, reps = dims
    da = nh * DV
    tn = min(1024, d)
    nj = d // tn
    mpg = pl.BlockSpec((1, 1, 1, tn), lambda i, j: (2, jnp.minimum(i // tps, bp - 1), 0, j))
    msg = pl.BlockSpec((1, bs, tn), lambda i, j: (2, 0, j))
    prompt_rows = lambda w: pl.BlockSpec((tm, w), lambda i, j: (jnp.minimum(i, ntp - 1), 0))
    sample_rows = lambda w: pl.BlockSpec((tm, w), lambda i, j: (0, 0))
    return pl.pallas_call(
        functools.partial(_even_out_kernel, ntp=ntp, reps=reps, nh=nh),
        out_shape=jax.ShapeDtypeStruct((t, d), F32),
        grid=(t // tm, nj),
        in_specs=[
            prompt_rows(da), sample_rows(da),
            pl.BlockSpec((tm, da), lambda i, j: (i, 3)),
            prompt_rows(yb_p.shape[1]), sample_rows(yb_s.shape[1]),
            pl.BlockSpec((1, DV), lambda i, j: (0, 0)),
            mpg, msg,
            pl.BlockSpec((1, w_out.shape[1], tn), lambda i, j: (li, 0, j)),
            pl.BlockSpec((tm, tn), lambda i, j: (i, j)),
        ],
        out_specs=pl.BlockSpec((tm, tn), lambda i, j: (i, j)),
        scratch_shapes=[pltpu.VMEM((tm, w_out.shape[1]), BF16)],
        compiler_params=_params("parallel", "arbitrary"),
        name="even_out",
    )(o_p, o_s, proj, yb_p, yb_s, og, mp, ms, w_out, x)


def _odd_in_kernel(hb_ref, wa_ref, wb_ref, ba_ref, bb_ref, u_ref, wab_ref, wbb_ref):
    @pl.when(pl.program_id(1) == 0)
    def _():
        wab_ref[...] = wa_ref[0].astype(BF16)
        wbb_ref[...] = wb_ref[0].astype(BF16)

    hb = hb_ref[...]
    a = jnp.dot(hb, wab_ref[...], preferred_element_type=F32) + ba_ref[0]
    b = jnp.dot(hb, wbb_ref[...], preferred_element_type=F32) + bb_ref[0]
    u_ref[...] = a * jax.nn.sigmoid(b)


def _odd_in(hb, w_in, b_in, li, tm):
    t, d = hb.shape
    dc = w_in.shape[2] // 2
    tn = min(512, dc)
    nj = dc // tn
    return pl.pallas_call(
        _odd_in_kernel,
        out_shape=jax.ShapeDtypeStruct((t, dc), F32),
        grid=(nj, t // tm),
        in_specs=[
            pl.BlockSpec((tm, d), lambda j, i: (i, 0)),
            pl.BlockSpec((1, d, tn), lambda j, i: (li, 0, j)),
            pl.BlockSpec((1, d, tn), lambda j, i: (li, 0, nj + j)),
            pl.BlockSpec((1, 1, tn), lambda j, i: (li, 0, j)),
            pl.BlockSpec((1, 1, tn), lambda j, i: (li, 0, nj + j)),
        ],
        out_specs=pl.BlockSpec((tm, tn), lambda j, i: (i, j)),
        scratch_shapes=[pltpu.VMEM((d, tn), BF16), pltpu.VMEM((d, tn), BF16)],
        compiler_params=_params("arbitrary", "arbitrary"),
        name="odd_in",
    )(hb, w_in, w_in, b_in, b_in)


def _dwconv_kernel(u_ref, halo_ref, w_ref, b_ref, z_ref, ext_ref, *, stride, hr, tps, zero_start):
    tm = u_ref.shape[0]
    keep = jnp.logical_not(jnp.logical_and(zero_start, pl.program_id(0) % tps == 0)).astype(F32)
    ext_ref[pl.ds(0, hr), :] = halo_ref[...] * keep
    ext_ref[pl.ds(hr, tm), :] = u_ref[...]
    w = w_ref[0]
    b = b_ref[0]
    rb = min(32, tm)

    def chunk(ci, carry):
        r0 = pl.multiple_of(ci * rb, rb)
        z_ref[pl.ds(r0, rb), :] = _causal_taps(ext_ref, r0, rb, slice(None), w, CONV_C, stride, hr) + b
        return carry

    lax.fori_loop(0, tm // rb, chunk, 0)


def _dwconv(u, halo, dw_w, dw_b, li, row0, ntiles, tm, stride, hr, tps, zero_start):
    dc = u.shape[1]
    cb = min(256, dc)
    if halo is None:
        hb = tm // hr
        harg = u
        hspec = pl.BlockSpec((hr, cb), lambda i, c: (jnp.maximum((row0 + i) * hb - 1, 0), c))
    else:
        harg = halo
        hspec = pl.BlockSpec((hr, cb), lambda i, c: (0, c))
    return pl.pallas_call(
        functools.partial(_dwconv_kernel, stride=stride, hr=hr, tps=tps, zero_start=zero_start),
        out_shape=jax.ShapeDtypeStruct((ntiles * tm, dc), F32),
        grid=(ntiles, dc // cb),
        in_specs=[
            pl.BlockSpec((tm, cb), lambda i, c: (row0 + i, c)),
            hspec,
            pl.BlockSpec((1, CONV_C, cb), lambda i, c: (li, 0, c)),
            pl.BlockSpec((1, 1, cb), lambda i, c: (li, 0, c)),
        ],
        out_specs=pl.BlockSpec((tm, cb), lambda i, c: (i, c)),
        scratch_shapes=[pltpu.VMEM((hr + tm, cb), F32)],
        compiler_params=_params("parallel", "parallel"),
        name="dwconv",
    )(u, harg, dw_w, dw_b)


def _odd_out_kernel(zp_ref, zsm_ref, lg_ref, lb_ref, mpg, msg, w_ref, b_ref, x_ref, xo_ref, zs_ref, *, ntp, reps):
    i = pl.program_id(0)

    @pl.when(pl.program_id(1) == 0)
    def _():
        z = jnp.where(i >= ntp, zsm_ref[...], zp_ref[...])
        zc = z - jnp.mean(z, axis=-1, keepdims=True)
        y = zc * lax.rsqrt(jnp.mean(zc * zc, axis=-1, keepdims=True) + EPS)
        zs_ref[...] = _silu(y * lg_ref[0] + lb_ref[0]).astype(BF16)

    y = jnp.dot(zs_ref[...], w_ref[0].astype(BF16), preferred_element_type=F32) + b_ref[0]
    xo_ref[...] = x_ref[...] + _tile_mod(mpg, msg, i >= ntp, reps) * y


def _odd_out(z_p, z_s, ln_g, ln_b, mp, ms, w_out, b_out, li, x, dims):
    t, d = x.shape
    tm, ntp, tps, bp, bs, reps = dims
    dc = z_p.shape[1]
    tn = min(1024, d)
    nj = d // tn
    mpg = pl.BlockSpec((1, 1, 1, tn), lambda i, j: (2, jnp.minimum(i // tps, bp - 1), 0, j))
    msg = pl.BlockSpec((1, bs, tn), lambda i, j: (2, 0, j))
    return pl.pallas_call(
        functools.partial(_odd_out_kernel, ntp=ntp, reps=reps),
        out_shape=jax.ShapeDtypeStruct((t, d), F32),
        grid=(t // tm, nj),
        in_specs=[
            pl.BlockSpec((tm, dc), lambda i, j: (jnp.minimum(i, ntp - 1), 0)),
            pl.BlockSpec((tm, dc), lambda i, j: (0, 0)),
            pl.BlockSpec((1, 1, dc), lambda i, j: (li, 0, 0)),
            pl.BlockSpec((1, 1, dc), lambda i, j: (li, 0, 0)),
            mpg, msg,
            pl.BlockSpec((1, dc, tn), lambda i, j: (li, 0, j)),
            pl.BlockSpec((1, 1, tn), lambda i, j: (li, 0, j)),
            pl.BlockSpec((tm, tn), lambda i, j: (i, j)),
        ],
        out_specs=pl.BlockSpec((tm, tn), lambda i, j: (i, j)),
        scratch_shapes=[pltpu.VMEM((tm, dc), BF16)],
        compiler_params=_params("parallel", "arbitrary"),
        name="odd_out",
    )(z_p, z_s, ln_g, ln_b, mp, ms, w_out, b_out, x)


def _ffnpre_kernel(x_ref, g_ref, mpsh, mpsc, mssh, mssc, wr_ref, br_ref,
                   h_ref, idx_ref, gate_ref, rank_ref, cnt_ref, carry_ref, *, ntp, reps):
    i = pl.program_id(0)
    tm = x_ref.shape[0]
    is_s = i >= ntp
    h = _rms(x_ref[...]) * g_ref[0] * (1 + _tile_mod(mpsc, mssc, is_s, reps)) + _tile_mod(mpsh, mssh, is_s, reps)
    h_ref[...] = _pack_bf16_pairs(h)
    logits = _mm(h, wr_ref[0]) + br_ref[0]

    @pl.when(i == 0)
    def _():
        carry_ref[...] = jnp.zeros(carry_ref.shape, F32)

    lane = lax.broadcasted_iota(I32, logits.shape, 1)
    work = logits
    sels, tops, picks = [], [], []
    for _ in range(TOP_K):
        m = jnp.max(work, axis=-1, keepdims=True)
        pick = jnp.min(jnp.where(work == m, lane, V7X_LANES), axis=-1, keepdims=True)
        sel = lane == pick
        work = jnp.where(sel, -jnp.inf, work)
        sels.append(sel)
        tops.append(m)
        picks.append(pick)
    exps = [jnp.exp(m - tops[0]) for m in tops]
    denom = exps[0]
    for e in exps[1:]:
        denom = denom + e

    onehot = sels[0]
    for s in sels[1:]:
        onehot = jnp.logical_or(onehot, s)
    onehot = onehot.astype(F32)
    r = lax.broadcasted_iota(I32, (tm, tm), 0)
    c = lax.broadcasted_iota(I32, (tm, tm), 1)
    before = _mm((r > c).astype(F32), onehot) + carry_ref[...]
    carry_ref[...] = carry_ref[...] + jnp.sum(onehot, axis=0, keepdims=True)

    idx_out = jnp.zeros(logits.shape, I32)
    gate_out = jnp.zeros(logits.shape, F32)
    rank_out = jnp.zeros(logits.shape, I32)
    for kk in range(TOP_K):
        rank = jnp.sum(jnp.where(sels[kk], before, 0.0), axis=-1, keepdims=True).astype(I32)
        idx_out = jnp.where(lane == kk, picks[kk], idx_out)
        gate_out = jnp.where(lane == kk, exps[kk] / denom, gate_out)
        rank_out = jnp.where(lane == kk, rank, rank_out)
    idx_ref[...] = idx_out
    gate_ref[...] = gate_out
    rank_ref[...] = rank_out
    cnt_ref[...] = carry_ref[...].astype(I32)


def _ffnpre(x, g, mp, ms, w_router, b_router, li, dims):
    t, d = x.shape
    tm, ntp, tps, bp, bs, reps = dims
    sh = _mod_specs(3, tps, bp, bs, d, 1)
    sc = _mod_specs(4, tps, bp, bs, d, 1)
    lanes = jax.ShapeDtypeStruct((t, V7X_LANES), I32)
    rowspec = pl.BlockSpec((tm, V7X_LANES), lambda i: (i, 0))
    return pl.pallas_call(
        functools.partial(_ffnpre_kernel, ntp=ntp, reps=reps),
        out_shape=(jax.ShapeDtypeStruct((t, d // 2), U32), lanes, jax.ShapeDtypeStruct((t, V7X_LANES), F32), lanes,
                   jax.ShapeDtypeStruct((1, V7X_LANES), I32)),
        grid=(t // tm,),
        in_specs=[
            pl.BlockSpec((tm, d), lambda i: (i, 0)),
            pl.BlockSpec((1, 1, d), lambda i: (li, 0, 0)),
            sh[0], sc[0], sh[1], sc[1],
            pl.BlockSpec((1, d, V7X_LANES), lambda i: (li, 0, 0)),
            pl.BlockSpec((1, 1, V7X_LANES), lambda i: (li, 0, 0)),
        ],
        out_specs=(pl.BlockSpec((tm, d // 2), lambda i: (i, 0)), rowspec, rowspec, rowspec,
                   pl.BlockSpec((1, V7X_LANES), lambda i: (0, 0))),
        scratch_shapes=[pltpu.VMEM((1, V7X_LANES), F32)],
        compiler_params=_params("arbitrary"),
        name="ffn_pre_router",
    )(x, g, mp, mp, ms, ms, w_router, b_router)


def _dispatch_kernel(slot_ref, zrow_ref, h_ref, xs_ref, zero_ref, sem, *, nexp, bm):
    i = pl.program_id(0)
    tm = h_ref.shape[0]

    @pl.when(i == 0)
    def _():
        zero_ref[...] = jnp.zeros(zero_ref.shape, zero_ref.dtype)

        def zcopy(r):
            return pltpu.make_async_copy(zero_ref, xs_ref.at[pl.ds(r, 1)], sem)

        def zfill(e, carry):
            base = zrow_ref[e]

            def issue(j, cc):
                for jj in range(ROW_UNROLL):
                    zcopy(base + j * ROW_UNROLL + jj).start()
                return cc

            lax.fori_loop(0, bm // ROW_UNROLL, issue, 0)

            def drain(j, cc):
                for jj in range(ROW_UNROLL):
                    zcopy(base + j * ROW_UNROLL + jj).wait()
                return cc

            lax.fori_loop(0, bm // ROW_UNROLL, drain, 0)
            return carry

        lax.fori_loop(0, nexp, zfill, 0)

    def rcopy(t, kk):
        return pltpu.make_async_copy(h_ref.at[pl.ds(t, 1)],
                                     xs_ref.at[pl.ds(slot_ref[(i * tm + t) * TOP_K + kk], 1)], sem)

    def issue(g, carry):
        for tt in range(ROW_UNROLL):
            for kk in range(TOP_K):
                rcopy(g * ROW_UNROLL + tt, kk).start(priority=kk % 2)
        return carry

    lax.fori_loop(0, tm // ROW_UNROLL, issue, 0)

    def drain(g, carry):
        for tt in range(ROW_UNROLL):
            for kk in range(TOP_K):
                rcopy(g * ROW_UNROLL + tt, kk).wait()
        return carry

    lax.fori_loop(0, tm // ROW_UNROLL, drain, 0)


def _dispatch(slots, zrows, h, rows, bm, tm):
    t, d = h.shape
    nexp = zrows.shape[0]
    assert tm % ROW_UNROLL == 0 and bm % ROW_UNROLL == 0
    return pl.pallas_call(
        functools.partial(_dispatch_kernel, nexp=nexp, bm=bm),
        out_shape=jax.ShapeDtypeStruct((rows, d), h.dtype),
        grid_spec=pltpu.PrefetchScalarGridSpec(
            num_scalar_prefetch=2,
            grid=(t // tm,),
            in_specs=[pl.BlockSpec((tm, d), lambda i, sl, zr: (i, 0))],
            out_specs=pl.BlockSpec(memory_space=pl.ANY),
            scratch_shapes=[pltpu.VMEM((1, d), h.dtype), pltpu.SemaphoreType.DMA(())],
        ),
        compiler_params=_params("arbitrary"),
        name="moe_dispatch",
    )(slots, zrows, h)


def _weight_stream(be_ref, grp_ref, ge_ref, meta_ref, copies, install):
    j, i = pl.program_id(0), pl.program_id(1)
    nj = pl.num_programs(0)
    nv, ng = meta_ref[0], meta_ref[1]
    valid = i < nv
    first = jnp.logical_and(valid, jnp.logical_or(i == 0, be_ref[i] != be_ref[jnp.maximum(i - 1, 0)]))

    @pl.when(first)
    def _():
        grp = grp_ref[i]

        @pl.when(jnp.logical_and(j == 0, grp == 0))
        def _():
            for c in copies(j, be_ref[i]):
                c.start()

        for c in copies(j, be_ref[i]):
            c.wait()
        install()
        wrap = grp + 1 == ng
        nxt_j = jnp.where(wrap, j + 1, j)
        nxt_e = ge_ref[jnp.where(wrap, 0, grp + 1)]

        @pl.when(nxt_j < nj)
        def _():
            for c in copies(nxt_j, nxt_e):
                c.start()

    return valid


def _expert_up_kernel(be_ref, grp_ref, ge_ref, meta_ref, x_ref, w_ref, bg_ref, bu_ref, act_ref,
                      sg_ref, su_ref, wgb_ref, wub_ref, sem, *, li, f):
    tn = sg_ref.shape[1]

    def copies(j, e):
        col = pl.multiple_of(j * tn, tn)
        return (pltpu.make_async_copy(w_ref.at[li, e, :, pl.ds(col, tn)], sg_ref, sem.at[0]),
                pltpu.make_async_copy(w_ref.at[li, e, :, pl.ds(f + col, tn)], su_ref, sem.at[1]))

    def install():
        wgb_ref[...] = sg_ref[...].astype(BF16)
        wub_ref[...] = su_ref[...].astype(BF16)

    valid = _weight_stream(be_ref, grp_ref, ge_ref, meta_ref, copies, install)

    @pl.when(valid)
    def _():
        xb = _unpack_bf16_pairs(x_ref[...])
        g = jnp.dot(xb, wgb_ref[...], preferred_element_type=F32) + bg_ref[0, 0]
        u = jnp.dot(xb, wub_ref[...], preferred_element_type=F32) + bu_ref[0, 0]
        gate = jnp.minimum(g, SWIGLU_LIMIT)
        up = jnp.clip(u, -SWIGLU_LIMIT, SWIGLU_LIMIT)
        act_ref[...] = ((up + 1) * gate * jax.nn.sigmoid(SWIGLU_ALPHA * gate)).astype(BF16)


def _expert_up(route, xs, w_gate_up, b_gate_up, li, bm):
    rows, dh = xs.shape
    d = 2 * dh
    nblk = rows // bm
    f = w_gate_up.shape[3] // 2
    tn = min(EXPERT_COLS, f)
    nj = f // tn
    blk = lambda i, meta: jnp.minimum(i, meta[0] - 1)
    return pl.pallas_call(
        functools.partial(_expert_up_kernel, li=li, f=f),
        out_shape=jax.ShapeDtypeStruct((rows, f), BF16),
        grid_spec=pltpu.PrefetchScalarGridSpec(
            num_scalar_prefetch=4,
            grid=(nj, nblk),
            in_specs=[
                pl.BlockSpec((bm, dh), lambda j, i, be, grp, ge, meta: (blk(i, meta), 0)),
                pl.BlockSpec(memory_space=pl.ANY),
                pl.BlockSpec((1, 1, 1, tn), lambda j, i, be, grp, ge, meta: (li, be[blk(i, meta)], 0, j)),
                pl.BlockSpec((1, 1, 1, tn), lambda j, i, be, grp, ge, meta: (li, be[blk(i, meta)], 0, nj + j)),
            ],
            out_specs=pl.BlockSpec((bm, tn), lambda j, i, be, grp, ge, meta: (blk(i, meta), j)),
            scratch_shapes=[pltpu.VMEM((d, tn), F32), pltpu.VMEM((d, tn), F32),
                            pltpu.VMEM((d, tn), BF16), pltpu.VMEM((d, tn), BF16),
                            pltpu.SemaphoreType.DMA((2,))],
        ),
        compiler_params=_params("arbitrary", "arbitrary"),
        name="expert_gate_up",
    )(*route, xs, w_gate_up, b_gate_up, b_gate_up)


def _expert_down_kernel(be_ref, grp_ref, ge_ref, meta_ref, a_ref, w_ref, b_ref, y_ref, st_ref, wb_ref, sem, *, li):
    tn = st_ref.shape[1]

    def copies(j, e):
        col = pl.multiple_of(j * tn, tn)
        return (pltpu.make_async_copy(w_ref.at[li, e, :, pl.ds(col, tn)], st_ref, sem.at[0]),)

    def install():
        wb_ref[...] = st_ref[...].astype(BF16)

    valid = _weight_stream(be_ref, grp_ref, ge_ref, meta_ref, copies, install)

    @pl.when(valid)
    def _():
        y_ref[...] = jnp.dot(a_ref[...], wb_ref[...], preferred_element_type=F32) + b_ref[0, 0]


def _expert_down(route, act, w_down, b_down, li, bm):
    rows, f = act.shape
    nblk = rows // bm
    d = w_down.shape[3]
    tn = min(EXPERT_DOWN_COLS, d)
    nj = d // tn
    blk = lambda i, meta: jnp.minimum(i, meta[0] - 1)
    return pl.pallas_call(
        functools.partial(_expert_down_kernel, li=li),
        out_shape=jax.ShapeDtypeStruct((rows, d), F32),
        grid_spec=pltpu.PrefetchScalarGridSpec(
            num_scalar_prefetch=4,
            grid=(nj, nblk),
            in_specs=[
                pl.BlockSpec((bm, f), lambda j, i, be, grp, ge, meta: (blk(i, meta), 0)),
                pl.BlockSpec(memory_space=pl.ANY),
                pl.BlockSpec((1, 1, 1, tn), lambda j, i, be, grp, ge, meta: (li, be[blk(i, meta)], 0, j)),
            ],
            out_specs=pl.BlockSpec((bm, tn), lambda j, i, be, grp, ge, meta: (blk(i, meta), j)),
            scratch_shapes=[pltpu.VMEM((f, tn), F32), pltpu.VMEM((f, tn), BF16), pltpu.SemaphoreType.DMA((1,))],
        ),
        compiler_params=_params("arbitrary", "arbitrary"),
        name="expert_down",
    )(*route, act, w_down, b_down)


def _combine_kernel(slot_ref, gate_ref, mpg, msg, x_ref, ys_ref, xo_ref, buf_ref, sem, *, ntp, reps):
    i = pl.program_id(0)
    tm = x_ref.shape[0]

    def rcopy(t, kk):
        return pltpu.make_async_copy(ys_ref.at[pl.ds(slot_ref[(i * tm + t) * TOP_K + kk], 1)],
                                     buf_ref.at[kk, pl.ds(t, 1)], sem)

    def issue(g, carry):
        for tt in range(ROW_UNROLL):
            for kk in range(TOP_K):
                rcopy(g * ROW_UNROLL + tt, kk).start(priority=kk % 2)
        return carry

    lax.fori_loop(0, tm // ROW_UNROLL, issue, 0)

    def drain(g, carry):
        for tt in range(ROW_UNROLL):
            for kk in range(TOP_K):
                rcopy(g * ROW_UNROLL + tt, kk).wait()
        return carry

    lax.fori_loop(0, tm // ROW_UNROLL, drain, 0)

    gates = gate_ref[...]
    y = buf_ref[0] * gates[:, 0:1]
    for kk in range(1, TOP_K):
        y = y + buf_ref[kk] * gates[:, kk:kk + 1]
    xo_ref[...] = x_ref[...] + _tile_mod(mpg, msg, i >= ntp, reps) * y


def _combine(slots, gates, mp, ms, x, ys, dims):
    t, d = x.shape
    tm, ntp, tps, bp, bs, reps = dims
    mpg = pl.BlockSpec((1, 1, 1, d), lambda i, sl: (5, jnp.minimum(i // tps, bp - 1), 0, 0))
    msg = pl.BlockSpec((1, bs, d), lambda i, sl: (5, 0, 0))
    return pl.pallas_call(
        functools.partial(_combine_kernel, ntp=ntp, reps=reps),
        out_shape=jax.ShapeDtypeStruct((t, d), F32),
        grid_spec=pltpu.PrefetchScalarGridSpec(
            num_scalar_prefetch=1,
            grid=(t // tm,),
            in_specs=[
                pl.BlockSpec((tm, V7X_LANES), lambda i, sl: (i, 0)),
                mpg, msg,
                pl.BlockSpec((tm, d), lambda i, sl: (i, 0)),
                pl.BlockSpec(memory_space=pl.ANY),
            ],
            out_specs=pl.BlockSpec((tm, d), lambda i, sl: (i, 0)),
            scratch_shapes=[pltpu.VMEM((TOP_K, tm, d), F32), pltpu.SemaphoreType.DMA(())],
        ),
        compiler_params=_params("arbitrary"),
        name="moe_combine",
    )(slots, gates, mp, ms, x, ys)


def _moe(x, li, g, mp, ms, w_router_p, b_router_p, w_gate_up, b_gate_up4, w_down, b_down4, dims):
    t, d = x.shape
    nexp = w_gate_up.shape[1]
    bm = MOE_ROWS
    h, idx, gates, rank, cnt = _ffnpre(x, g, mp, ms, w_router_p, b_router_p, li, dims)
    counts = cnt[0, :nexp]
    padded = (counts + bm - 1) // bm * bm
    pad_end = jnp.cumsum(padded)
    pad_start = pad_end - padded
    idx4 = idx[:, :TOP_K]
    start4 = jnp.sum(jnp.where(idx4[..., None] == jnp.arange(nexp, dtype=I32), pad_start, 0), axis=-1)
    slots = (start4 + rank[:, :TOP_K]).reshape(t * TOP_K).astype(I32)
    nblk = -(-(t * TOP_K + nexp * (bm - 1)) // bm) + 1
    rows = nblk * bm
    blk_start = jnp.arange(nblk, dtype=I32) * bm
    blk_expert = jnp.minimum(jnp.sum(pad_end[None, :] <= blk_start[:, None], axis=1), nexp - 1).astype(I32)
    has = counts > 0
    run_of_expert = jnp.cumsum(has.astype(I32)) - 1
    blk_run = jnp.sum(jnp.where(blk_expert[:, None] == jnp.arange(nexp, dtype=I32), run_of_expert, 0), axis=1)
    run_expert = jnp.argsort(jnp.logical_not(has), stable=True).astype(I32)
    meta = jnp.stack([pad_end[-1] // bm, jnp.sum(has.astype(I32))]).astype(I32)
    route = (blk_expert, blk_run.astype(I32), run_expert, meta)
    zrows = (pad_start + counts).astype(I32)
    xs = _dispatch(slots, zrows, h, rows, bm, dims[0])
    act = _expert_up(route, xs, w_gate_up, b_gate_up4, li, bm)
    ys = _expert_down(route, act, w_down, b_down4, li, bm)
    return _combine(slots, gates, mp, ms, x, ys, dims)


def _final_kernel(x_ref, g_ref, yp_ref, ys_ref, *, ntp):
    y = _rms(x_ref[...]) * g_ref[...]
    i = pl.program_id(0)

    @pl.when(i < ntp)
    def _():
        yp_ref[...] = y

    @pl.when(i >= ntp)
    def _():
        ys_ref[...] = y


def _final_norm(x, g, tm, ntp):
    t, d = x.shape
    return pl.pallas_call(
        functools.partial(_final_kernel, ntp=ntp),
        out_shape=(jax.ShapeDtypeStruct((ntp * tm, d), F32), jax.ShapeDtypeStruct((t - ntp * tm, d), F32)),
        grid=(t // tm,),
        in_specs=[pl.BlockSpec((tm, d), lambda i: (i, 0)), pl.BlockSpec((1, d), lambda i: (0, 0))],
        out_specs=(pl.BlockSpec((tm, d), lambda i: (jnp.minimum(i, ntp - 1), 0)),
                   pl.BlockSpec((tm, d), lambda i: (0, 0))),
        compiler_params=_params("arbitrary"),
        name="final_norm",
    )(x, g)


def _to_pos_major(a):
    b, l, c = a.shape
    return jnp.swapaxes(a, 0, 1).reshape(l * b, c)


def _seq_tails(a, nseq, seqlen, n, width):
    return jnp.stack([a[(b + 1) * seqlen - n:(b + 1) * seqlen, :width] for b in range(nseq)])


def _from_pos_major(a, b):
    lb, c = a.shape
    return jnp.swapaxes(a.reshape(lb // b, b, c), 0, 1)


def kernel(x_prompt, x_sample, state_delta, state_conv_qkv, state_conv_b, state_conv_c, c_prompt, c_sample, norm_mix, norm_ffn, w_mod, b_mod, w_in_even, conv_qkv_w, a_log, dt_bias, o_norm_g, conv_b_w, w_out_even, w_in_odd, b_in_odd, dw_w, dw_b, ln_g, ln_b, w_out_odd, b_out_odd, w_router, b_router, w_gate_up, b_gate_up, w_down, b_down, norm_final):
    bp, seq, d = x_prompt.shape
    bs, dseq, _ = x_sample.shape
    depth = w_mod.shape[0]
    nh = a_log.shape[1]
    da = nh * DV
    db = conv_b_w.shape[2]
    dc = dw_w.shape[2]
    nexp = w_router.shape[2]
    tp, ts = bp * seq, bs * dseq
    tm = ts
    assert seq % tm == 0 and bs % V7X_SUBLANES == 0 and da == db and dseq >= CONV_A - 1
    assert nexp <= V7X_LANES and 2 * nh <= V7X_LANES
    tps = seq // tm
    ntp = tp // tm
    dims = (tm, ntp, tps, bp, bs, dseq)
    chunk = math.gcd(seq, CHUNK_A)

    x = jnp.concatenate([x_prompt.reshape(tp, d), _to_pos_major(x_sample)], axis=0)
    rc = -(-(bp + bs) // V7X_SUBLANES) * V7X_SUBLANES
    c_all = jnp.zeros((rc, d), F32).at[:bp].set(c_prompt).at[bp:bp + bs].set(c_sample)
    mod = _modulation(c_all, w_mod, b_mod)

    w_router_p = jnp.zeros((depth, d, V7X_LANES), F32).at[:, :, :nexp].set(w_router)
    b_router_p = jnp.full((depth, 1, V7X_LANES), -jnp.inf, F32).at[:, 0, :nexp].set(b_router)
    b_gate_up4 = b_gate_up.reshape(depth, nexp, 1, b_gate_up.shape[2])
    b_down4 = b_down.reshape(depth, nexp, 1, d)

    qkv_w = 3 * nh * DK
    c_ab = qkv_w + da
    c_b = c_ab + 2 * nh
    hr_qkv = (CONV_A - 1) * bs
    hr_c = (CONV_C - 1) * bs
    outs = {k: [] for k in ("dp", "qp", "bp", "cp", "ds", "qs", "bs", "cs")}

    for l in range(depth):
        mp = mod[l][:, :bp].reshape(6, bp, 1, d)
        ms = mod[l][:, bp:bp + bs]
        li = l // 2
        if l % 2 == 0:
            w = w_in_even[li]
            w_main = jnp.concatenate([w[:, :c_ab], w[:, c_b:]], axis=1)
            w_ab = jnp.zeros((d, V7X_LANES), F32).at[:, :2 * nh].set(w[:, c_ab:c_b])
            hb, ab = _prenorm(x, norm_mix[l:l + 1], mp, ms, w_ab, dims)
            proj = _even_in(hb, w_main, tm)
            adt = jnp.zeros((V7X_SUBLANES, V7X_LANES), F32).at[0, :nh].set(a_log[li]).at[1, :nh].set(dt_bias[li])
            wqkv, wb = conv_qkv_w[li], conv_b_w[li]
            qp, kp, vp, gbp, ybp, utp = _evenprep(proj, ab, None, wqkv, wb, adt, 0, ntp, tm, 1, V7X_SUBLANES,
                                                  tps, True, nh)
            sq = _to_pos_major(state_conv_qkv[li])
            su = jnp.concatenate([jnp.zeros(((CONV_A - CONV_B) * bs, db), F32), _to_pos_major(state_conv_b[li])], axis=0)
            halos = (sq[:, :da], sq[:, da:2 * da], sq[:, 2 * da:], su, jnp.ones_like(su))
            qs, ks, vs, gbs, ybs, uts = _evenprep(proj, ab, halos, wqkv, wb, adt, ntp, 1, tm, bs, hr_qkv,
                                                  1, False, nh)
            o_p, s_p = _delta(qp, kp, vp, gbp, None, li, bp, seq // chunk, chunk, nh, math.gcd(bp, DELTA_SEQS_PROMPT))

            qkvg = _from_pos_major(jnp.concatenate([qs, ks, vs, gbs], axis=1), bs)
            qkvg = jnp.pad(qkvg, ((0, 0), (0, V7X_SUBLANES - dseq), (0, 0))).reshape(bs * V7X_SUBLANES, -1)
            o_s8, s_s = _delta(qkvg, None, None, None, state_delta, li,
                               bs, 1, V7X_SUBLANES, nh, math.gcd(bs, DELTA_SEQS_SAMPLE))
            o_s = _to_pos_major(o_s8.reshape(bs, V7X_SUBLANES, da)[:, :dseq])
            x = _even_out(o_p, o_s, proj, ybp, ybs, o_norm_g[li:li + 1], mp, ms, w_out_even, li, x, dims, nh)

            outs["dp"].append(s_p)
            outs["qp"].append(_seq_tails(proj, bp, seq, CONV_A - 1, qkv_w))
            outs["bp"].append(utp.reshape(bp, tps, V7X_SUBLANES, db)[:, tps - 1, V7X_SUBLANES - (CONV_B - 1):])
            outs["ds"].append(s_s)
            outs["qs"].append(_from_pos_major(proj[tp + (dseq - (CONV_A - 1)) * bs:, :qkv_w], bs))
            u_ext = jnp.concatenate([state_conv_b[li], _from_pos_major(uts, bs)], axis=1)
            outs["bs"].append(u_ext[:, u_ext.shape[1] - (CONV_B - 1):])
        else:
            (hb,) = _prenorm(x, norm_mix[l:l + 1], mp, ms, None, dims)
            u = _odd_in(hb, w_in_odd, b_in_odd.reshape(-1, 1, 2 * dc), li, tm)
            dwb = dw_b.reshape(-1, 1, dc)
            zp = _dwconv(u, None, dw_w, dwb, li, 0, ntp, tm, 1, 32, tps, True)
            zs = _dwconv(u, _to_pos_major(state_conv_c[li]), dw_w, dwb, li, ntp, 1, tm, bs, hr_c, 1, False)
            x = _odd_out(zp, zs, ln_g.reshape(-1, 1, dc), ln_b.reshape(-1, 1, dc), mp, ms, w_out_odd,
                         b_out_odd.reshape(-1, 1, d), li, x, dims)
            outs["cp"].append(_seq_tails(u, bp, seq, CONV_C - 1, dc))
            c_ext = jnp.concatenate([state_conv_c[li], _from_pos_major(u[tp:], bs)], axis=1)
            outs["cs"].append(c_ext[:, c_ext.shape[1] - (CONV_C - 1):])
        x = _moe(x, l, norm_ffn.reshape(depth, 1, d), mp, ms, w_router_p, b_router_p, w_gate_up, b_gate_up4,
                 w_down, b_down4, dims)

    y_p, y_s = _final_norm(x, norm_final.reshape(1, d), tm, ntp)
    y_prompt = y_p.reshape(bp, seq, d)
    y_sample = _from_pos_major(y_s, bs)
    return (y_prompt, y_sample,
            jnp.stack(outs["dp"]), jnp.stack(outs["qp"]), jnp.stack(outs["bp"]), jnp.stack(outs["cp"]),
            jnp.stack(outs["ds"]), jnp.stack(outs["qs"]), jnp.stack(outs["bs"]), jnp.stack(outs["cs"]))
```

```python
import functools
import math

import jax
import jax.numpy as jnp
from jax import lax
from jax.experimental import pallas as pl
from jax.experimental.pallas import tpu as pltpu

F32 = jnp.float32
BF16 = jnp.bfloat16
I32 = jnp.int32
U32 = jnp.uint32
EPS = 1e-6
HI = lax.Precision.HIGHEST

DK = 128
DV = 128
CONV_A = 4
CONV_B = 3
CONV_C = 31
CHUNK_A = 64
TOP_K = 4
SWIGLU_ALPHA = 1.702
SWIGLU_LIMIT = 7.0

V7X_LANES = 128
V7X_SUBLANES = 8
V7X_VMEM_LIMIT_BYTES = 56 * 1024 * 1024
MOE_ROWS = 256
EXPERT_COLS = 1024
EXPERT_DOWN_COLS = 2048
DELTA_SEQS_PROMPT = 2
DELTA_SEQS_SAMPLE = 4
ROW_UNROLL = 8


def _params(*sem):
    return pltpu.CompilerParams(dimension_semantics=sem, vmem_limit_bytes=V7X_VMEM_LIMIT_BYTES)


def _mm(a, b):
    return jnp.dot(a.astype(BF16), b.astype(BF16), preferred_element_type=F32)


def _mm_hi(a, b):
    return jnp.dot(a, b, precision=HI, preferred_element_type=F32)


_NN = (((1,), (0,)), ((), ()))
_NT = (((1,), (1,)), ((), ()))
_TN = (((0,), (0,)), ((), ()))


def _dot(a, b, dims):
    return lax.dot_general(a, b, dims, preferred_element_type=F32)


def _split2(a):
    hi = a.astype(BF16)
    return hi, (a - hi.astype(F32)).astype(BF16)


def _mm3s(a, b):
    return _dot(a[0], b[0], _NN) + (_dot(a[0], b[1], _NN) + _dot(a[1], b[0], _NN))


def _pack_bf16_pairs(x):
    half = x.shape[1] // 2
    bits = lax.bitcast_convert_type(x.astype(BF16).astype(F32), U32)
    return (bits[:, half:] & jnp.uint32(0xFFFF0000)) | (bits[:, :half] >> 16)


def _unpack_bf16_pairs(w):
    lo = lax.bitcast_convert_type(w << 16, F32)
    hi = lax.bitcast_convert_type(w & jnp.uint32(0xFFFF0000), F32)
    return jnp.concatenate([lo, hi], axis=1).astype(BF16)


def _silu(x):
    return x * jax.nn.sigmoid(x)


def _rms(x):
    return x * lax.rsqrt(jnp.mean(x * x, axis=-1, keepdims=True) + EPS)


def _tile_mod(mp_ref, ms_ref, is_sample, reps):
    s = ms_ref[0]
    s = jnp.concatenate([s] * reps, axis=0)
    return jnp.where(is_sample, s, mp_ref[0, 0])


def _mod_specs(k, tps, bp, bs, d, ngrid):
    if ngrid == 1:
        mp = pl.BlockSpec((1, 1, 1, d), lambda i: (k, jnp.minimum(i // tps, bp - 1), 0, 0))
        ms = pl.BlockSpec((1, bs, d), lambda i: (k, 0, 0))
    else:
        mp = pl.BlockSpec((1, 1, 1, d), lambda i, j: (k, jnp.minimum(i // tps, bp - 1), 0, 0))
        ms = pl.BlockSpec((1, bs, d), lambda i, j: (k, 0, 0))
    return mp, ms


def _mod_kernel(c_ref, w_ref, b_ref, o_ref):
    o_ref[0, 0] = _mm(_silu(c_ref[...]), w_ref[0]) + b_ref[0]


def _modulation(c_all, w_mod, b_mod):
    depth, d, d6 = w_mod.shape
    rc = c_all.shape[0]
    tn = min(512, d)
    nj = d // tn
    return pl.pallas_call(
        _mod_kernel,
        out_shape=jax.ShapeDtypeStruct((depth, 6, rc, d), F32),
        grid=(depth, 6, nj),
        in_specs=[
            pl.BlockSpec((rc, d), lambda l, k, j: (0, 0)),
            pl.BlockSpec((1, d, tn), lambda l, k, j: (l, 0, k * nj + j)),
            pl.BlockSpec((1, 1, tn), lambda l, k, j: (l, 0, k * nj + j)),
        ],
        out_specs=pl.BlockSpec((1, 1, rc, tn), lambda l, k, j: (l, k, 0, j)),
        compiler_params=_params("parallel", "parallel", "parallel"),
        name="modulation",
    )(c_all, w_mod, b_mod.reshape(depth, 1, d6))


def _prenorm_kernel(*refs, ntp, reps, has_ab):
    if has_ab:
        x_ref, g_ref, mpsh, mpsc, mssh, mssc, wab_ref, hb_ref, ab_ref = refs
    else:
        x_ref, g_ref, mpsh, mpsc, mssh, mssc, hb_ref = refs
    is_s = pl.program_id(0) >= ntp
    h = _rms(x_ref[...]) * g_ref[...] * (1 + _tile_mod(mpsc, mssc, is_s, reps)) + _tile_mod(mpsh, mssh, is_s, reps)
    hb = h.astype(BF16)
    hb_ref[...] = hb
    if has_ab:
        ab_ref[...] = jnp.dot(hb, wab_ref[...].astype(BF16), preferred_element_type=F32)


def _prenorm(x, g, mp, ms, w_ab, dims):
    t, d = x.shape
    tm, ntp, tps, bp, bs, reps = dims
    sh = _mod_specs(0, tps, bp, bs, d, 1)
    sc = _mod_specs(1, tps, bp, bs, d, 1)
    in_specs = [pl.BlockSpec((tm, d), lambda i: (i, 0)), pl.BlockSpec((1, d), lambda i: (0, 0)),
                sh[0], sc[0], sh[1], sc[1]]
    args = [x, g, mp, mp, ms, ms]
    out_shape = [jax.ShapeDtypeStruct((t, d), BF16)]
    out_specs = [pl.BlockSpec((tm, d), lambda i: (i, 0))]
    if w_ab is not None:
        in_specs.append(pl.BlockSpec((d, V7X_LANES), lambda i: (0, 0)))
        args.append(w_ab)
        out_shape.append(jax.ShapeDtypeStruct((t, V7X_LANES), F32))
        out_specs.append(pl.BlockSpec((tm, V7X_LANES), lambda i: (i, 0)))
    return pl.pallas_call(
        functools.partial(_prenorm_kernel, ntp=ntp, reps=reps, has_ab=w_ab is not None),
        out_shape=tuple(out_shape),
        grid=(t // tm,),
        in_specs=in_specs,
        out_specs=tuple(out_specs),
        compiler_params=_params("parallel"),
        name="prenorm",
    )(*args)


def _even_in_kernel(hb_ref, w_ref, proj_ref, wb_ref):
    @pl.when(pl.program_id(1) == 0)
    def _():
        wb_ref[...] = w_ref[...].astype(BF16)

    proj_ref[...] = jnp.dot(hb_ref[...], wb_ref[...], preferred_element_type=F32)


def _even_in(hb, w_main, tm):
    t, d = hb.shape
    nmain = w_main.shape[1]
    tn = nmain // 7
    return pl.pallas_call(
        _even_in_kernel,
        out_shape=jax.ShapeDtypeStruct((t, nmain), F32),
        grid=(7, t // tm),
        in_specs=[pl.BlockSpec((tm, d), lambda j, i: (i, 0)), pl.BlockSpec((d, tn), lambda j, i: (0, j))],
        out_specs=pl.BlockSpec((tm, tn), lambda j, i: (i, j)),
        scratch_shapes=[pltpu.VMEM((d, tn), BF16)],
        compiler_params=_params("arbitrary", "arbitrary"),
        name="even_in",
    )(hb, w_main)


def _causal_taps(ext_ref, r0, rb, lanes, w, width, stride, hr):
    acc = None
    if stride % V7X_SUBLANES == 0:
        for j in range(width):
            start = pl.multiple_of(r0 + (hr - (width - 1 - j) * stride), V7X_SUBLANES)
            term = ext_ref[pl.ds(start, rb), lanes] * w[j:j + 1, :]
            acc = term if acc is None else acc + term
        return acc
    look = -(-(width - 1) * stride // V7X_SUBLANES) * V7X_SUBLANES
    nrow = rb + look
    win = ext_ref[pl.ds(pl.multiple_of(r0 + (hr - look), V7X_SUBLANES), nrow), lanes]
    for res in range(V7X_SUBLANES):
        taps = [j for j in range(width) if (look - (width - 1 - j) * stride) % V7X_SUBLANES == res]
        if not taps:
            continue
        shifted = win if res == 0 else pltpu.roll(win, nrow - res, axis=0)
        for j in taps:
            off = look - (width - 1 - j) * stride - res
            term = shifted[off:off + rb, :] * w[j:j + 1, :]
            acc = term if acc is None else acc + term
    return acc


def _evenprep_kernel(q_ref, k_ref, v_ref, xb_ref, cp_ref, bpost_ref, ab_ref,
                     hq_ref, hk_ref, hv_ref, hxb_ref, hcp_ref,
                     wqkv_ref, wb_ref, adt_ref,
                     qo_ref, ko_ref, vo_ref, gb_ref, yb_ref, ut_ref,
                     extq, extk, extv, extu, *, stride, hr, tps, zero_start, nh):
    tm = q_ref.shape[0]
    da = q_ref.shape[1]
    keep = jnp.logical_not(jnp.logical_and(zero_start, pl.program_id(0) % tps == 0)).astype(F32)
    for ext, halo, cur in ((extq, hq_ref, q_ref), (extk, hk_ref, k_ref), (extv, hv_ref, v_ref)):
        ext[pl.ds(0, hr), :] = halo[...] * keep
        ext[pl.ds(hr, tm), :] = cur[...]
    extu[pl.ds(0, hr), :] = hcp_ref[...] * hxb_ref[...] * keep
    extu[pl.ds(hr, tm), :] = cp_ref[...] * xb_ref[...]
    tr = ut_ref.shape[0]
    ut_ref[...] = extu[pl.ds(hr + tm - tr, tr), :]

    ab = ab_ref[...]
    adt = adt_ref[...]
    z = ab + adt[1:2, :]
    softplus = jnp.maximum(z, 0.0) + jnp.log(1.0 + jnp.exp(-jnp.abs(z)))
    lane = lax.broadcasted_iota(I32, ab.shape, 1)
    gb_ref[...] = jnp.where(lane < nh, -jnp.exp(adt[0:1, :]) * softplus, jax.nn.sigmoid(ab))

    wqkv = wqkv_ref[...]
    wb = wb_ref[...]
    rb = min(64, tm)

    def chunk(ci, carry):
        r0 = pl.multiple_of(ci * rb, rb)
        rows = pl.ds(r0, rb)
        for h in range(nh):
            sl = slice(h * DK, (h + 1) * DK)
            taps = lambda ext, w0: _causal_taps(ext, r0, rb, sl, wqkv[:, w0 + h * DK:w0 + (h + 1) * DK],
                                                CONV_A, stride, hr)
            qh = _silu(taps(extq, 0))
            kh = _silu(taps(extk, da))
            qo_ref[rows, sl] = qh * lax.rsqrt(jnp.sum(qh * qh, axis=-1, keepdims=True) + EPS) * (DK ** -0.5)
            ko_ref[rows, sl] = kh * lax.rsqrt(jnp.sum(kh * kh, axis=-1, keepdims=True) + EPS)
            vo_ref[rows, sl] = _silu(taps(extv, 2 * da))
            yb = bpost_ref[rows, sl] * _causal_taps(extu, r0, rb, sl, wb[:, sl], CONV_B, stride, hr)
            yb_ref[rows, sl] = yb.astype(BF16)
        return carry

    lax.fori_loop(0, tm // rb, chunk, 0)


def _evenprep(proj, ab, halos, wqkv, wb, adt, row0, ntiles, tm, stride, hr, tps, zero_start, nh):
    da = nh * DK
    tr = max(V7X_SUBLANES, (CONV_B - 1) * stride)
    cur = lambda c: pl.BlockSpec((tm, da), lambda i: (row0 + i, c))
    if halos is None:
        hb = tm // hr
        hspec = lambda c: pl.BlockSpec((hr, da), lambda i: (jnp.maximum((row0 + i) * hb - 1, 0), c))
        hargs = [proj] * 5
        hspecs = [hspec(0), hspec(1), hspec(2), hspec(4), hspec(5)]
    else:
        hargs = list(halos)
        hspecs = [pl.BlockSpec((hr, da), lambda i: (0, 0)) for _ in range(5)]
    rows = ntiles * tm
    full = lambda a: pl.BlockSpec(a.shape, lambda i: (0, 0))
    return pl.pallas_call(
        functools.partial(_evenprep_kernel, stride=stride, hr=hr, tps=tps, zero_start=zero_start, nh=nh),
        out_shape=(jax.ShapeDtypeStruct((rows, da), F32), jax.ShapeDtypeStruct((rows, da), F32),
                   jax.ShapeDtypeStruct((rows, da), F32), jax.ShapeDtypeStruct((rows, V7X_LANES), F32),
                   jax.ShapeDtypeStruct((rows, da), BF16), jax.ShapeDtypeStruct((ntiles * tr, da), F32)),
        grid=(ntiles,),
        in_specs=[cur(0), cur(1), cur(2), cur(4), cur(5), cur(6),
                  pl.BlockSpec((tm, V7X_LANES), lambda i: (row0 + i, 0))] + hspecs + [full(wqkv), full(wb), full(adt)],
        out_specs=(pl.BlockSpec((tm, da), lambda i: (i, 0)), pl.BlockSpec((tm, da), lambda i: (i, 0)),
                   pl.BlockSpec((tm, da), lambda i: (i, 0)), pl.BlockSpec((tm, V7X_LANES), lambda i: (i, 0)),
                   pl.BlockSpec((tm, da), lambda i: (i, 0)), pl.BlockSpec((tr, da), lambda i: (i, 0))),
        scratch_shapes=[pltpu.VMEM((hr + tm, da), F32) for _ in range(4)],
        compiler_params=_params("parallel"),
        name="evenprep",
    )(proj, proj, proj, proj, proj, proj, ab, *hargs, wqkv, wb, adt)


def _delta_kernel(*refs, nh, has_s0):
    if has_s0:
        q_ref, k_ref, v_ref, gb_ref, s0_ref, o_ref, sout_ref, s_ref = refs
    else:
        q_ref, k_ref, v_ref, gb_ref, o_ref, sout_ref, s_ref = refs
    n = pl.program_id(1)
    nseq, c = q_ref.shape[0], q_ref.shape[1]

    @pl.when(n == 0)
    def _():
        s_ref[...] = s0_ref[0] if has_s0 else jnp.zeros(s_ref.shape, F32)

    row = lax.broadcasted_iota(I32, (c, c), 0)
    col = lax.broadcasted_iota(I32, (c, c), 1)
    causal = row >= col
    strict = row > col
    eye = (row == col).astype(F32)

    tri = causal.astype(BF16)
    gbs, gcums, gcum_ts = [], [], []
    for sq in range(nseq):
        gb = gb_ref[sq]
        g1 = gb.astype(BF16)
        r1 = gb - g1.astype(F32)
        g2 = r1.astype(BF16)
        g3 = (r1 - g2.astype(F32)).astype(BF16)
        gcum = _dot(tri, g1, _NN) + (_dot(tri, g2, _NN) + _dot(tri, g3, _NN))
        gbs.append(gb)
        gcums.append(gcum)
        gcum_ts.append(gcum.T)
    levels = int(math.log2(c))

    chains = [(sq, hh) for sq in range(nseq) for hh in range(nh)]
    heads = range(len(chains))
    lanes = [slice(hh * DK, (hh + 1) * DK) for _, hh in chains]
    qs = [q_ref[sq, :, lanes[h]] for h, (sq, _) in enumerate(chains)]
    ks = [k_ref[sq, :, lanes[h]] for h, (sq, _) in enumerate(chains)]
    g_col = [gcums[sq][:, hh:hh + 1] for sq, hh in chains]
    g_last = [gcums[sq][c - 1:c, hh:hh + 1] for sq, hh in chains]
    beta = [gbs[sq][:, nh + hh:nh + hh + 1] for sq, hh in chains]
    decay = [jnp.where(causal, jnp.exp(jnp.where(causal, g_col[h] - gcum_ts[sq][hh:hh + 1, :], 0.0)), 0.0)
             for h, (sq, hh) in enumerate(chains)]
    kb = [ks[h] * beta[h] for h in heads]
    khb = [ks[h].astype(BF16) for h in heads]
    low = [jnp.where(strict, _dot(kb[h].astype(BF16), khb[h], _NT) * decay[h], 0.0) for h in heads]
    inv = [eye - low[h] for h in heads]
    pw = [_split2(low[h]) for h in heads]
    for _ in range(levels - 1):
        pw = [_split2(_mm3s(pw[h], pw[h])) for h in heads]
        inv = [inv[h] + _mm3s(_split2(inv[h]), pw[h]) for h in heads]
    eg = [jnp.exp(g_col[h]) for h in heads]
    uw = [_mm3s(_split2(inv[h]),
                _split2(jnp.concatenate([v_ref[chains[h][0], :, lanes[h]] * beta[h], kb[h] * eg[h]], axis=1)))
          for h in heads]
    intra = [jnp.where(causal, _dot(qs[h].astype(BF16), khb[h], _NT) * decay[h], 0.0) for h in heads]
    s = [s_ref[sq, hh] for sq, hh in chains]
    sb = [s[h].astype(BF16) for h in heads]
    vnb = [(uw[h][:, :DV] - _dot(uw[h][:, DV:].astype(BF16), sb[h], _NN)).astype(BF16) for h in heads]
    for h, (sq, hh) in enumerate(chains):
        o_ref[sq, :, lanes[h]] = (_dot((qs[h] * eg[h]).astype(BF16), sb[h], _NN)
                                  + _dot(intra[h].astype(BF16), vnb[h], _NN))
    for h, (sq, hh) in enumerate(chains):
        k_dec = ks[h] * jnp.exp(g_last[h] - g_col[h])
        s_ref[sq, hh] = s[h] * jnp.exp(g_last[h]) + _dot(k_dec.astype(BF16), vnb[h], _TN)

    @pl.when(n == pl.num_programs(1) - 1)
    def _():
        sout_ref[...] = s_ref[...]


def _delta(q, k, v, gb, s0, li, nb, nchunks, c, nh, nseq):
    da = nh * DK
    assert nb % nseq == 0
    seqlen = nchunks * c
    view = lambda a: a.reshape(nb, seqlen, a.shape[1])
    rowspec = lambda w, col=0: pl.BlockSpec((nseq, c, w), lambda b, n: (b, n, col))
    sspec = pl.BlockSpec((nseq, nh, DK, DV), lambda b, n: (b, 0, 0, 0))
    if k is None:
        in_specs = [rowspec(da, 0), rowspec(da, 1), rowspec(da, 2), rowspec(V7X_LANES, 3 * da // V7X_LANES)]
        args = [view(q)] * 4
    else:
        in_specs = [rowspec(da), rowspec(da), rowspec(da), rowspec(V7X_LANES)]
        args = [view(q), view(k), view(v), view(gb)]
    if s0 is not None:
        in_specs.append(pl.BlockSpec((1, nseq, nh, DK, DV), lambda b, n: (li, b, 0, 0, 0)))
        args.append(s0)
    o, s_out = pl.pallas_call(
        functools.partial(_delta_kernel, nh=nh, has_s0=s0 is not None),
        out_shape=(jax.ShapeDtypeStruct((nb, seqlen, da), F32), jax.ShapeDtypeStruct((nb, nh, DK, DV), F32)),
        grid=(nb // nseq, nchunks),
        in_specs=in_specs,
        out_specs=(rowspec(da), sspec),
        scratch_shapes=[pltpu.VMEM((nseq, nh, DK, DV), F32)],
        compiler_params=_params("parallel", "arbitrary"),
        name="delta_rule",
    )(*args)
    return o.reshape(nb * seqlen, da), s_out


def _even_out_kernel(op_ref, os_ref, gout_ref, ybp_ref, ybs_ref, og_ref, mpg, msg, w_ref, x_ref, xo_ref, yin_ref,
                     *, ntp, reps, nh):
    i = pl.program_id(0)
    da = nh * DV

    @pl.when(pl.program_id(1) == 0)
    def _():
        is_s = i >= ntp
        for h in range(nh):
            sl = slice(h * DV, (h + 1) * DV)
            o = jnp.where(is_s, os_ref[:, sl], op_ref[:, sl])
            yin_ref[:, sl] = (_rms(o) * og_ref[...] * _silu(gout_ref[:, sl])).astype(BF16)
        yin_ref[:, da:] = jnp.where(is_s, ybs_ref[...], ybp_ref[...])

    y = jnp.dot(yin_ref[...], w_ref[0].astype(BF16), preferred_element_type=F32)
    xo_ref[...] = x_ref[...] + _tile_mod(mpg, msg, i >= ntp, reps) * y


def _even_out(o_p, o_s, proj, yb_p, yb_s, og, mp, ms, w_out, li, x, dims, nh):
    t, d = x.shape
    tm, ntp, tps, bp, bs, reps = dims
    da = nh * DV
    tn = min(1024, d)
    nj = d // tn
    mpg = pl.BlockSpec((1, 1, 1, tn), lambda i, j: (2, jnp.minimum(i // tps, bp - 1), 0, j))
    msg = pl.BlockSpec((1, bs, tn), lambda i, j: (2, 0, j))
    prompt_rows = lambda w: pl.BlockSpec((tm, w), lambda i, j: (jnp.minimum(i, ntp - 1), 0))
    sample_rows = lambda w: pl.BlockSpec((tm, w), lambda i, j: (0, 0))
    return pl.pallas_call(
        functools.partial(_even_out_kernel, ntp=ntp, reps=reps, nh=nh),
        out_shape=jax.ShapeDtypeStruct((t, d), F32),
        grid=(t // tm, nj),
        in_specs=[
            prompt_rows(da), sample_rows(da),
            pl.BlockSpec((tm, da), lambda i, j: (i, 3)),
            prompt_rows(yb_p.shape[1]), sample_rows(yb_s.shape[1]),
            pl.BlockSpec((1, DV), lambda i, j: (0, 0)),
            mpg, msg,
            pl.BlockSpec((1, w_out.shape[1], tn), lambda i, j: (li, 0, j)),
            pl.BlockSpec((tm, tn), lambda i, j: (i, j)),
        ],
        out_specs=pl.BlockSpec((tm, tn), lambda i, j: (i, j)),
        scratch_shapes=[pltpu.VMEM((tm, w_out.shape[1]), BF16)],
        compiler_params=_params("parallel", "arbitrary"),
        name="even_out",
    )(o_p, o_s, proj, yb_p, yb_s, og, mp, ms, w_out, x)


def _odd_in_kernel(hb_ref, wa_ref, wb_ref, ba_ref, bb_ref, u_ref, wab_ref, wbb_ref):
    @pl.when(pl.program_id(1) == 0)
    def _():
        wab_ref[...] = wa_ref[0].astype(BF16)
        wbb_ref[...] = wb_ref[0].astype(BF16)

    hb = hb_ref[...]
    a = jnp.dot(hb, wab_ref[...], preferred_element_type=F32) + ba_ref[0]
    b = jnp.dot(hb, wbb_ref[...], preferred_element_type=F32) + bb_ref[0]
    u_ref[...] = a * jax.nn.sigmoid(b)


def _odd_in(hb, w_in, b_in, li, tm):
    t, d = hb.shape
    dc = w_in.shape[2] // 2
    tn = min(512, dc)
    nj = dc // tn
    return pl.pallas_call(
        _odd_in_kernel,
        out_shape=jax.ShapeDtypeStruct((t, dc), F32),
        grid=(nj, t // tm),
        in_specs=[
            pl.BlockSpec((tm, d), lambda j, i: (i, 0)),
            pl.BlockSpec((1, d, tn), lambda j, i: (li, 0, j)),
            pl.BlockSpec((1, d, tn), lambda j, i: (li, 0, nj + j)),
            pl.BlockSpec((1, 1, tn), lambda j, i: (li, 0, j)),
            pl.BlockSpec((1, 1, tn), lambda j, i: (li, 0, nj + j)),
        ],
        out_specs=pl.BlockSpec((tm, tn), lambda j, i: (i, j)),
        scratch_shapes=[pltpu.VMEM((d, tn), BF16), pltpu.VMEM((d, tn), BF16)],
        compiler_params=_params("arbitrary", "arbitrary"),
        name="odd_in",
    )(hb, w_in, w_in, b_in, b_in)


def _dwconv_kernel(u_ref, halo_ref, w_ref, b_ref, z_ref, ext_ref, *, stride, hr, tps, zero_start):
    tm = u_ref.shape[0]
    keep = jnp.logical_not(jnp.logical_and(zero_start, pl.program_id(0) % tps == 0)).astype(F32)
    ext_ref[pl.ds(0, hr), :] = halo_ref[...] * keep
    ext_ref[pl.ds(hr, tm), :] = u_ref[...]
    w = w_ref[0]
    b = b_ref[0]
    rb = min(32, tm)

    def chunk(ci, carry):
        r0 = pl.multiple_of(ci * rb, rb)
        z_ref[pl.ds(r0, rb), :] = _causal_taps(ext_ref, r0, rb, slice(None), w, CONV_C, stride, hr) + b
        return carry

    lax.fori_loop(0, tm // rb, chunk, 0)


def _dwconv(u, halo, dw_w, dw_b, li, row0, ntiles, tm, stride, hr, tps, zero_start):
    dc = u.shape[1]
    cb = min(256, dc)
    if halo is None:
        hb = tm // hr
        harg = u
        hspec = pl.BlockSpec((hr, cb), lambda i, c: (jnp.maximum((row0 + i) * hb - 1, 0), c))
    else:
        harg = halo
        hspec = pl.BlockSpec((hr, cb), lambda i, c: (0, c))
    return pl.pallas_call(
        functools.partial(_dwconv_kernel, stride=stride, hr=hr, tps=tps, zero_start=zero_start),
        out_shape=jax.ShapeDtypeStruct((ntiles * tm, dc), F32),
        grid=(ntiles, dc // cb),
        in_specs=[
            pl.BlockSpec((tm, cb), lambda i, c: (row0 + i, c)),
            hspec,
            pl.BlockSpec((1, CONV_C, cb), lambda i, c: (li, 0, c)),
            pl.BlockSpec((1, 1, cb), lambda i, c: (li, 0, c)),
        ],
        out_specs=pl.BlockSpec((tm, cb), lambda i, c: (i, c)),
        scratch_shapes=[pltpu.VMEM((hr + tm, cb), F32)],
        compiler_params=_params("parallel", "parallel"),
        name="dwconv",
    )(u, harg, dw_w, dw_b)


def _odd_out_kernel(zp_ref, zsm_ref, lg_ref, lb_ref, mpg, msg, w_ref, b_ref, x_ref, xo_ref, zs_ref, *, ntp, reps):
    i = pl.program_id(0)

    @pl.when(pl.program_id(1) == 0)
    def _():
        z = jnp.where(i >= ntp, zsm_ref[...], zp_ref[...])
        zc = z - jnp.mean(z, axis=-1, keepdims=True)
        y = zc * lax.rsqrt(jnp.mean(zc * zc, axis=-1, keepdims=True) + EPS)
        zs_ref[...] = _silu(y * lg_ref[0] + lb_ref[0]).astype(BF16)

    y = jnp.dot(zs_ref[...], w_ref[0].astype(BF16), preferred_element_type=F32) + b_ref[0]
    xo_ref[...] = x_ref[...] + _tile_mod(mpg, msg, i >= ntp, reps) * y


def _odd_out(z_p, z_s, ln_g, ln_b, mp, ms, w_out, b_out, li, x, dims):
    t, d = x.shape
    tm, ntp, tps, bp, bs, reps = dims
    dc = z_p.shape[1]
    tn = min(1024, d)
    nj = d // tn
    mpg = pl.BlockSpec((1, 1, 1, tn), lambda i, j: (2, jnp.minimum(i // tps, bp - 1), 0, j))
    msg = pl.BlockSpec((1, bs, tn), lambda i, j: (2, 0, j))
    return pl.pallas_call(
        functools.partial(_odd_out_kernel, ntp=ntp, reps=reps),
        out_shape=jax.ShapeDtypeStruct((t, d), F32),
        grid=(t // tm, nj),
        in_specs=[
            pl.BlockSpec((tm, dc), lambda i, j: (jnp.minimum(i, ntp - 1), 0)),
            pl.BlockSpec((tm, dc), lambda i, j: (0, 0)),
            pl.BlockSpec((1, 1, dc), lambda i, j: (li, 0, 0)),
            pl.BlockSpec((1, 1, dc), lambda i, j: (li, 0, 0)),
            mpg, msg,
            pl.BlockSpec((1, dc, tn), lambda i, j: (li, 0, j)),
            pl.BlockSpec((1, 1, tn), lambda i, j: (li, 0, j)),
            pl.BlockSpec((tm, tn), lambda i, j: (i, j)),
        ],
        out_specs=pl.BlockSpec((tm, tn), lambda i, j: (i, j)),
        scratch_shapes=[pltpu.VMEM((tm, dc), BF16)],
        compiler_params=_params("parallel", "arbitrary"),
        name="odd_out",
    )(z_p, z_s, ln_g, ln_b, mp, ms, w_out, b_out, x)


def _ffnpre_kernel(x_ref, g_ref, mpsh, mpsc, mssh, mssc, wr_ref, br_ref,
                   h_ref, idx_ref, gate_ref, rank_ref, cnt_ref, carry_ref, *, ntp, reps):
    i = pl.program_id(0)
    tm = x_ref.shape[0]
    is_s = i >= ntp
    h = _rms(x_ref[...]) * g_ref[0] * (1 + _tile_mod(mpsc, mssc, is_s, reps)) + _tile_mod(mpsh, mssh, is_s, reps)
    h_ref[...] = _pack_bf16_pairs(h)
    logits = _mm(h, wr_ref[0]) + br_ref[0]

    @pl.when(i == 0)
    def _():
        carry_ref[...] = jnp.zeros(carry_ref.shape, F32)

    lane = lax.broadcasted_iota(I32, logits.shape, 1)
    work = logits
    sels, tops, picks = [], [], []
    for _ in range(TOP_K):
        m = jnp.max(work, axis=-1, keepdims=True)
        pick = jnp.min(jnp.where(work == m, lane, V7X_LANES), axis=-1, keepdims=True)
        sel = lane == pick
        work = jnp.where(sel, -jnp.inf, work)
        sels.append(sel)
        tops.append(m)
        picks.append(pick)
    exps = [jnp.exp(m - tops[0]) for m in tops]
    denom = exps[0]
    for e in exps[1:]:
        denom = denom + e

    onehot = sels[0]
    for s in sels[1:]:
        onehot = jnp.logical_or(onehot, s)
    onehot = onehot.astype(F32)
    r = lax.broadcasted_iota(I32, (tm, tm), 0)
    c = lax.broadcasted_iota(I32, (tm, tm), 1)
    before = _mm((r > c).astype(F32), onehot) + carry_ref[...]
    carry_ref[...] = carry_ref[...] + jnp.sum(onehot, axis=0, keepdims=True)

    idx_out = jnp.zeros(logits.shape, I32)
    gate_out = jnp.zeros(logits.shape, F32)
    rank_out = jnp.zeros(logits.shape, I32)
    for kk in range(TOP_K):
        rank = jnp.sum(jnp.where(sels[kk], before, 0.0), axis=-1, keepdims=True).astype(I32)
        idx_out = jnp.where(lane == kk, picks[kk], idx_out)
        gate_out = jnp.where(lane == kk, exps[kk] / denom, gate_out)
        rank_out = jnp.where(lane == kk, rank, rank_out)
    idx_ref[...] = idx_out
    gate_ref[...] = gate_out
    rank_ref[...] = rank_out
    cnt_ref[...] = carry_ref[...].astype(I32)


def _ffnpre(x, g, mp, ms, w_router, b_router, li, dims):
    t, d = x.shape
    tm, ntp, tps, bp, bs, reps = dims
    sh = _mod_specs(3, tps, bp, bs, d, 1)
    sc = _mod_specs(4, tps, bp, bs, d, 1)
    lanes = jax.ShapeDtypeStruct((t, V7X_LANES), I32)
    rowspec = pl.BlockSpec((tm, V7X_LANES), lambda i: (i, 0))
    return pl.pallas_call(
        functools.partial(_ffnpre_kernel, ntp=ntp, reps=reps),
        out_shape=(jax.ShapeDtypeStruct((t, d // 2), U32), lanes, jax.ShapeDtypeStruct((t, V7X_LANES), F32), lanes,
                   jax.ShapeDtypeStruct((1, V7X_LANES), I32)),
        grid=(t // tm,),
        in_specs=[
            pl.BlockSpec((tm, d), lambda i: (i, 0)),
            pl.BlockSpec((1, 1, d), lambda i: (li, 0, 0)),
            sh[0], sc[0], sh[1], sc[1],
            pl.BlockSpec((1, d, V7X_LANES), lambda i: (li, 0, 0)),
            pl.BlockSpec((1, 1, V7X_LANES), lambda i: (li, 0, 0)),
        ],
        out_specs=(pl.BlockSpec((tm, d // 2), lambda i: (i, 0)), rowspec, rowspec, rowspec,
                   pl.BlockSpec((1, V7X_LANES), lambda i: (0, 0))),
        scratch_shapes=[pltpu.VMEM((1, V7X_LANES), F32)],
        compiler_params=_params("arbitrary"),
        name="ffn_pre_router",
    )(x, g, mp, mp, ms, ms, w_router, b_router)


def _dispatch_kernel(slot_ref, zrow_ref, h_ref, xs_ref, zero_ref, sem, *, nexp, bm):
    i = pl.program_id(0)
    tm = h_ref.shape[0]

    @pl.when(i == 0)
    def _():
        zero_ref[...] = jnp.zeros(zero_ref.shape, zero_ref.dtype)

        def zcopy(r):
            return pltpu.make_async_copy(zero_ref, xs_ref.at[pl.ds(r, 1)], sem)

        def zfill(e, carry):
            base = zrow_ref[e]

            def issue(j, cc):
                for jj in range(ROW_UNROLL):
                    zcopy(base + j * ROW_UNROLL + jj).start()
                return cc

            lax.fori_loop(0, bm // ROW_UNROLL, issue, 0)

            def drain(j, cc):
                for jj in range(ROW_UNROLL):
                    zcopy(base + j * ROW_UNROLL + jj).wait()
                return cc

            lax.fori_loop(0, bm // ROW_UNROLL, drain, 0)
            return carry

        lax.fori_loop(0, nexp, zfill, 0)

    def rcopy(t, kk):
        return pltpu.make_async_copy(h_ref.at[pl.ds(t, 1)],
                                     xs_ref.at[pl.ds(slot_ref[(i * tm + t) * TOP_K + kk], 1)], sem)

    def issue(g, carry):
        for tt in range(ROW_UNROLL):
            for kk in range(TOP_K):
                rcopy(g * ROW_UNROLL + tt, kk).start(priority=kk % 2)
        return carry

    lax.fori_loop(0, tm // ROW_UNROLL, issue, 0)

    def drain(g, carry):
        for tt in range(ROW_UNROLL):
            for kk in range(TOP_K):
                rcopy(g * ROW_UNROLL + tt, kk).wait()
        return carry

    lax.fori_loop(0, tm // ROW_UNROLL, drain, 0)


def _dispatch(slots, zrows, h, rows, bm, tm):
    t, d = h.shape
    nexp = zrows.shape[0]
    assert tm % ROW_UNROLL == 0 and bm % ROW_UNROLL == 0
    return pl.pallas_call(
        functools.partial(_dispatch_kernel, nexp=nexp, bm=bm),
        out_shape=jax.ShapeDtypeStruct((rows, d), h.dtype),
        grid_spec=pltpu.PrefetchScalarGridSpec(
            num_scalar_prefetch=2,
            grid=(t // tm,),
            in_specs=[pl.BlockSpec((tm, d), lambda i, sl, zr: (i, 0))],
            out_specs=pl.BlockSpec(memory_space=pl.ANY),
            scratch_shapes=[pltpu.VMEM((1, d), h.dtype), pltpu.SemaphoreType.DMA(())],
        ),
        compiler_params=_params("arbitrary"),
        name="moe_dispatch",
    )(slots, zrows, h)


def _weight_stream(be_ref, grp_ref, ge_ref, meta_ref, copies, install):
    j, i = pl.program_id(0), pl.program_id(1)
    nj = pl.num_programs(0)
    nv, ng = meta_ref[0], meta_ref[1]
    valid = i < nv
    first = jnp.logical_and(valid, jnp.logical_or(i == 0, be_ref[i] != be_ref[jnp.maximum(i - 1, 0)]))

    @pl.when(first)
    def _():
        grp = grp_ref[i]

        @pl.when(jnp.logical_and(j == 0, grp == 0))
        def _():
            for c in copies(j, be_ref[i]):
                c.start()

        for c in copies(j, be_ref[i]):
            c.wait()
        install()
        wrap = grp + 1 == ng
        nxt_j = jnp.where(wrap, j + 1, j)
        nxt_e = ge_ref[jnp.where(wrap, 0, grp + 1)]

        @pl.when(nxt_j < nj)
        def _():
            for c in copies(nxt_j, nxt_e):
                c.start()

    return valid


def _expert_up_kernel(be_ref, grp_ref, ge_ref, meta_ref, x_ref, w_ref, bg_ref, bu_ref, act_ref,
                      sg_ref, su_ref, wgb_ref, wub_ref, sem, *, li, f):
    tn = sg_ref.shape[1]

    def copies(j, e):
        col = pl.multiple_of(j * tn, tn)
        return (pltpu.make_async_copy(w_ref.at[li, e, :, pl.ds(col, tn)], sg_ref, sem.at[0]),
                pltpu.make_async_copy(w_ref.at[li, e, :, pl.ds(f + col, tn)], su_ref, sem.at[1]))

    def install():
        wgb_ref[...] = sg_ref[...].astype(BF16)
        wub_ref[...] = su_ref[...].astype(BF16)

    valid = _weight_stream(be_ref, grp_ref, ge_ref, meta_ref, copies, install)

    @pl.when(valid)
    def _():
        xb = _unpack_bf16_pairs(x_ref[...])
        g = jnp.dot(xb, wgb_ref[...], preferred_element_type=F32) + bg_ref[0, 0]
        u = jnp.dot(xb, wub_ref[...], preferred_element_type=F32) + bu_ref[0, 0]
        gate = jnp.minimum(g, SWIGLU_LIMIT)
        up = jnp.clip(u, -SWIGLU_LIMIT, SWIGLU_LIMIT)
        act_ref[...] = ((up + 1) * gate * jax.nn.sigmoid(SWIGLU_ALPHA * gate)).astype(BF16)


def _expert_up(route, xs, w_gate_up, b_gate_up, li, bm):
    rows, dh = xs.shape
    d = 2 * dh
    nblk = rows // bm
    f = w_gate_up.shape[3] // 2
    tn = min(EXPERT_COLS, f)
    nj = f // tn
    blk = lambda i, meta: jnp.minimum(i, meta[0] - 1)
    return pl.pallas_call(
        functools.partial(_expert_up_kernel, li=li, f=f),
        out_shape=jax.ShapeDtypeStruct((rows, f), BF16),
        grid_spec=pltpu.PrefetchScalarGridSpec(
            num_scalar_prefetch=4,
            grid=(nj, nblk),
            in_specs=[
                pl.BlockSpec((bm, dh), lambda j, i, be, grp, ge, meta: (blk(i, meta), 0)),
                pl.BlockSpec(memory_space=pl.ANY),
                pl.BlockSpec((1, 1, 1, tn), lambda j, i, be, grp, ge, meta: (li, be[blk(i, meta)], 0, j)),
                pl.BlockSpec((1, 1, 1, tn), lambda j, i, be, grp, ge, meta: (li, be[blk(i, meta)], 0, nj + j)),
            ],
            out_specs=pl.BlockSpec((bm, tn), lambda j, i, be, grp, ge, meta: (blk(i, meta), j)),
            scratch_shapes=[pltpu.VMEM((d, tn), F32), pltpu.VMEM((d, tn), F32),
                            pltpu.VMEM((d, tn), BF16), pltpu.VMEM((d, tn), BF16),
                            pltpu.SemaphoreType.DMA((2,))],
        ),
        compiler_params=_params("arbitrary", "arbitrary"),
        name="expert_gate_up",
    )(*route, xs, w_gate_up, b_gate_up, b_gate_up)


def _expert_down_kernel(be_ref, grp_ref, ge_ref, meta_ref, a_ref, w_ref, b_ref, y_ref, st_ref, wb_ref, sem, *, li):
    tn = st_ref.shape[1]

    def copies(j, e):
        col = pl.multiple_of(j * tn, tn)
        return (pltpu.make_async_copy(w_ref.at[li, e, :, pl.ds(col, tn)], st_ref, sem.at[0]),)

    def install():
        wb_ref[...] = st_ref[...].astype(BF16)

    valid = _weight_stream(be_ref, grp_ref, ge_ref, meta_ref, copies, install)

    @pl.when(valid)
    def _():
        y_ref[...] = jnp.dot(a_ref[...], wb_ref[...], preferred_element_type=F32) + b_ref[0, 0]


def _expert_down(route, act, w_down, b_down, li, bm):
    rows, f = act.shape
    nblk = rows // bm
    d = w_down.shape[3]
    tn = min(EXPERT_DOWN_COLS, d)
    nj = d // tn
    blk = lambda i, meta: jnp.minimum(i, meta[0] - 1)
    return pl.pallas_call(
        functools.partial(_expert_down_kernel, li=li),
        out_shape=jax.ShapeDtypeStruct((rows, d), F32),
        grid_spec=pltpu.PrefetchScalarGridSpec(
            num_scalar_prefetch=4,
            grid=(nj, nblk),
            in_specs=[
                pl.BlockSpec((bm, f), lambda j, i, be, grp, ge, meta: (blk(i, meta), 0)),
                pl.BlockSpec(memory_space=pl.ANY),
                pl.BlockSpec((1, 1, 1, tn), lambda j, i, be, grp, ge, meta: (li, be[blk(i, meta)], 0, j)),
            ],
            out_specs=pl.BlockSpec((bm, tn), lambda j, i, be, grp, ge, meta: (blk(i, meta), j)),
            scratch_shapes=[pltpu.VMEM((f, tn), F32), pltpu.VMEM((f, tn), BF16), pltpu.SemaphoreType.DMA((1,))],
        ),
        compiler_params=_params("arbitrary", "arbitrary"),
        name="expert_down",
    )(*route, act, w_down, b_down)


def _combine_kernel(slot_ref, gate_ref, mpg, msg, x_ref, ys_ref, xo_ref, buf_ref, sem, *, ntp, reps):
    i = pl.program_id(0)
    tm = x_ref.shape[0]
    half = i % 2

    def rcopy(tile, hf, t, kk):
        return pltpu.make_async_copy(ys_ref.at[pl.ds(slot_ref[(tile * tm + t) * TOP_K + kk], 1)],
                                     buf_ref.at[hf, kk, pl.ds(t, 1)], sem.at[hf])

    def start_tile(tile, hf):
        def issue(g, carry):
            for tt in range(ROW_UNROLL):
                for kk in range(TOP_K):
                    rcopy(tile, hf, g * ROW_UNROLL + tt, kk).start(priority=kk % 2)
            return carry

        lax.fori_loop(0, tm // ROW_UNROLL, issue, 0)

    @pl.when(i == 0)
    def _():
        start_tile(0, 0)

    @pl.when(i + 1 < pl.num_programs(0))
    def _():
        start_tile(i + 1, 1 - half)

    def drain(g, carry):
        for tt in range(ROW_UNROLL):
            for kk in range(TOP_K):
                rcopy(i, half, g * ROW_UNROLL + tt, kk).wait()
        return carry

    lax.fori_loop(0, tm // ROW_UNROLL, drain, 0)

    gates = gate_ref[...]
    y = buf_ref[half, 0] * gates[:, 0:1]
    for kk in range(1, TOP_K):
        y = y + buf_ref[half, kk] * gates[:, kk:kk + 1]
    xo_ref[...] = x_ref[...] + _tile_mod(mpg, msg, i >= ntp, reps) * y


def _combine(slots, gates, mp, ms, x, ys, dims):
    t, d = x.shape
    tm, ntp, tps, bp, bs, reps = dims
    mpg = pl.BlockSpec((1, 1, 1, d), lambda i, sl: (5, jnp.minimum(i // tps, bp - 1), 0, 0))
    msg = pl.BlockSpec((1, bs, d), lambda i, sl: (5, 0, 0))
    return pl.pallas_call(
        functools.partial(_combine_kernel, ntp=ntp, reps=reps),
        out_shape=jax.ShapeDtypeStruct((t, d), F32),
        grid_spec=pltpu.PrefetchScalarGridSpec(
            num_scalar_prefetch=1,
            grid=(t // tm,),
            in_specs=[
                pl.BlockSpec((tm, V7X_LANES), lambda i, sl: (i, 0)),
                mpg, msg,
                pl.BlockSpec((tm, d), lambda i, sl: (i, 0)),
                pl.BlockSpec(memory_space=pl.ANY),
            ],
            out_specs=pl.BlockSpec((tm, d), lambda i, sl: (i, 0)),
            scratch_shapes=[pltpu.VMEM((2, TOP_K, tm, d), F32), pltpu.SemaphoreType.DMA((2,))],
        ),
        compiler_params=_params("arbitrary"),
        name="moe_combine",
    )(slots, gates, mp, ms, x, ys)


def _moe(x, li, g, mp, ms, w_router_p, b_router_p, w_gate_up, b_gate_up4, w_down, b_down4, dims):
    t, d = x.shape
    nexp = w_gate_up.shape[1]
    bm = MOE_ROWS
    h, idx, gates, rank, cnt = _ffnpre(x, g, mp, ms, w_router_p, b_router_p, li, dims)
    counts = cnt[0, :nexp]
    padded = (counts + bm - 1) // bm * bm
    pad_end = jnp.cumsum(padded)
    pad_start = pad_end - padded
    idx4 = idx[:, :TOP_K]
    start4 = jnp.sum(jnp.where(idx4[..., None] == jnp.arange(nexp, dtype=I32), pad_start, 0), axis=-1)
    slots = (start4 + rank[:, :TOP_K]).reshape(t * TOP_K).astype(I32)
    nblk = -(-(t * TOP_K + nexp * (bm - 1)) // bm) + 1
    rows = nblk * bm
    blk_start = jnp.arange(nblk, dtype=I32) * bm
    blk_expert = jnp.minimum(jnp.sum(pad_end[None, :] <= blk_start[:, None], axis=1), nexp - 1).astype(I32)
    has = counts > 0
    run_of_expert = jnp.cumsum(has.astype(I32)) - 1
    blk_run = jnp.sum(jnp.where(blk_expert[:, None] == jnp.arange(nexp, dtype=I32), run_of_expert, 0), axis=1)
    run_expert = jnp.argsort(jnp.logical_not(has), stable=True).astype(I32)
    meta = jnp.stack([pad_end[-1] // bm, jnp.sum(has.astype(I32))]).astype(I32)
    route = (blk_expert, blk_run.astype(I32), run_expert, meta)
    zrows = (pad_start + counts).astype(I32)
    xs = _dispatch(slots, zrows, h, rows, bm, dims[0])
    act = _expert_up(route, xs, w_gate_up, b_gate_up4, li, bm)
    ys = _expert_down(route, act, w_down, b_down4, li, bm)
    return _combine(slots, gates, mp, ms, x, ys, dims)


def _final_kernel(x_ref, g_ref, yp_ref, ys_ref, *, ntp):
    y = _rms(x_ref[...]) * g_ref[...]
    i = pl.program_id(0)

    @pl.when(i < ntp)
    def _():
        yp_ref[...] = y

    @pl.when(i >= ntp)
    def _():
        ys_ref[...] = y


def _final_norm(x, g, tm, ntp):
    t, d = x.shape
    return pl.pallas_call(
        functools.partial(_final_kernel, ntp=ntp),
        out_shape=(jax.ShapeDtypeStruct((ntp * tm, d), F32), jax.ShapeDtypeStruct((t - ntp * tm, d), F32)),
        grid=(t // tm,),
        in_specs=[pl.BlockSpec((tm, d), lambda i: (i, 0)), pl.BlockSpec((1, d), lambda i: (0, 0))],
        out_specs=(pl.BlockSpec((tm, d), lambda i: (jnp.minimum(i, ntp - 1), 0)),
                   pl.BlockSpec((tm, d), lambda i: (0, 0))),
        compiler_params=_params("arbitrary"),
        name="final_norm",
    )(x, g)


def _to_pos_major(a):
    b, l, c = a.shape
    return jnp.swapaxes(a, 0, 1).reshape(l * b, c)


def _seq_tails(a, nseq, seqlen, n, width):
    return jnp.stack([a[(b + 1) * seqlen - n:(b + 1) * seqlen, :width] for b in range(nseq)])


def _from_pos_major(a, b):
    lb, c = a.shape
    return jnp.swapaxes(a.reshape(lb // b, b, c), 0, 1)


def kernel(x_prompt, x_sample, state_delta, state_conv_qkv, state_conv_b, state_conv_c, c_prompt, c_sample, norm_mix, norm_ffn, w_mod, b_mod, w_in_even, conv_qkv_w, a_log, dt_bias, o_norm_g, conv_b_w, w_out_even, w_in_odd, b_in_odd, dw_w, dw_b, ln_g, ln_b, w_out_odd, b_out_odd, w_router, b_router, w_gate_up, b_gate_up, w_down, b_down, norm_final):
    bp, seq, d = x_prompt.shape
    bs, dseq, _ = x_sample.shape
    depth = w_mod.shape[0]
    nh = a_log.shape[1]
    da = nh * DV
    db = conv_b_w.shape[2]
    dc = dw_w.shape[2]
    nexp = w_router.shape[2]
    tp, ts = bp * seq, bs * dseq
    tm = ts
    assert seq % tm == 0 and bs % V7X_SUBLANES == 0 and da == db and dseq >= CONV_A - 1
    assert nexp <= V7X_LANES and 2 * nh <= V7X_LANES
    tps = seq // tm
    ntp = tp // tm
    dims = (tm, ntp, tps, bp, bs, dseq)
    chunk = math.gcd(seq, CHUNK_A)

    x = jnp.concatenate([x_prompt.reshape(tp, d), _to_pos_major(x_sample)], axis=0)
    rc = -(-(bp + bs) // V7X_SUBLANES) * V7X_SUBLANES
    c_all = jnp.zeros((rc, d), F32).at[:bp].set(c_prompt).at[bp:bp + bs].set(c_sample)
    mod = _modulation(c_all, w_mod, b_mod)

    w_router_p = jnp.zeros((depth, d, V7X_LANES), F32).at[:, :, :nexp].set(w_router)
    b_router_p = jnp.full((depth, 1, V7X_LANES), -jnp.inf, F32).at[:, 0, :nexp].set(b_router)
    b_gate_up4 = b_gate_up.reshape(depth, nexp, 1, b_gate_up.shape[2])
    b_down4 = b_down.reshape(depth, nexp, 1, d)

    qkv_w = 3 * nh * DK
    c_ab = qkv_w + da
    c_b = c_ab + 2 * nh
    hr_qkv = (CONV_A - 1) * bs
    hr_c = (CONV_C - 1) * bs
    outs = {k: [] for k in ("dp", "qp", "bp", "cp", "ds", "qs", "bs", "cs")}

    for l in range(depth):
        mp = mod[l][:, :bp].reshape(6, bp, 1, d)
        ms = mod[l][:, bp:bp + bs]
        li = l // 2
        if l % 2 == 0:
            w = w_in_even[li]
            w_main = jnp.concatenate([w[:, :c_ab], w[:, c_b:]], axis=1)
            w_ab = jnp.zeros((d, V7X_LANES), F32).at[:, :2 * nh].set(w[:, c_ab:c_b])
            hb, ab = _prenorm(x, norm_mix[l:l + 1], mp, ms, w_ab, dims)
            proj = _even_in(hb, w_main, tm)
            adt = jnp.zeros((V7X_SUBLANES, V7X_LANES), F32).at[0, :nh].set(a_log[li]).at[1, :nh].set(dt_bias[li])
            wqkv, wb = conv_qkv_w[li], conv_b_w[li]
            qp, kp, vp, gbp, ybp, utp = _evenprep(proj, ab, None, wqkv, wb, adt, 0, ntp, tm, 1, V7X_SUBLANES,
                                                  tps, True, nh)
            sq = _to_pos_major(state_conv_qkv[li])
            su = jnp.concatenate([jnp.zeros(((CONV_A - CONV_B) * bs, db), F32), _to_pos_major(state_conv_b[li])], axis=0)
            halos = (sq[:, :da], sq[:, da:2 * da], sq[:, 2 * da:], su, jnp.ones_like(su))
            qs, ks, vs, gbs, ybs, uts = _evenprep(proj, ab, halos, wqkv, wb, adt, ntp, 1, tm, bs, hr_qkv,
                                                  1, False, nh)
            o_p, s_p = _delta(qp, kp, vp, gbp, None, li, bp, seq // chunk, chunk, nh, math.gcd(bp, DELTA_SEQS_PROMPT))

            qkvg = _from_pos_major(jnp.concatenate([qs, ks, vs, gbs], axis=1), bs)
            qkvg = jnp.pad(qkvg, ((0, 0), (0, V7X_SUBLANES - dseq), (0, 0))).reshape(bs * V7X_SUBLANES, -1)
            o_s8, s_s = _delta(qkvg, None, None, None, state_delta, li,
                               bs, 1, V7X_SUBLANES, nh, math.gcd(bs, DELTA_SEQS_SAMPLE))
            o_s = _to_pos_major(o_s8.reshape(bs, V7X_SUBLANES, da)[:, :dseq])
            x = _even_out(o_p, o_s, proj, ybp, ybs, o_norm_g[li:li + 1], mp, ms, w_out_even, li, x, dims, nh)

            outs["dp"].append(s_p)
            outs["qp"].append(_seq_tails(proj, bp, seq, CONV_A - 1, qkv_w))
            outs["bp"].append(utp.reshape(bp, tps, V7X_SUBLANES, db)[:, tps - 1, V7X_SUBLANES - (CONV_B - 1):])
            outs["ds"].append(s_s)
            outs["qs"].append(_from_pos_major(proj[tp + (dseq - (CONV_A - 1)) * bs:, :qkv_w], bs))
            u_ext = jnp.concatenate([state_conv_b[li], _from_pos_major(uts, bs)], axis=1)
            outs["bs"].append(u_ext[:, u_ext.shape[1] - (CONV_B - 1):])
        else:
            (hb,) = _prenorm(x, norm_mix[l:l + 1], mp, ms, None, dims)
            u = _odd_in(hb, w_in_odd, b_in_odd.reshape(-1, 1, 2 * dc), li, tm)
            dwb = dw_b.reshape(-1, 1, dc)
            zp = _dwconv(u, None, dw_w, dwb, li, 0, ntp, tm, 1, 32, tps, True)
            zs = _dwconv(u, _to_pos_major(state_conv_c[li]), dw_w, dwb, li, ntp, 1, tm, bs, hr_c, 1, False)
            x = _odd_out(zp, zs, ln_g.reshape(-1, 1, dc), ln_b.reshape(-1, 1, dc), mp, ms, w_out_odd,
                         b_out_odd.reshape(-1, 1, d), li, x, dims)
            outs["cp"].append(_seq_tails(u, bp, seq, CONV_C - 1, dc))
            c_ext = jnp.concatenate([state_conv_c[li], _from_pos_major(u[tp:], bs)], axis=1)
            outs["cs"].append(c_ext[:, c_ext.shape[1] - (CONV_C - 1):])
        x = _moe(x, l, norm_ffn.reshape(depth, 1, d), mp, ms, w_router_p, b_router_p, w_gate_up, b_gate_up4,
                 w_down, b_down4, dims)

    y_p, y_s = _final_norm(x, norm_final.reshape(1, d), tm, ntp)
    y_prompt = y_p.reshape(bp, seq, d)
    y_sample = _from_pos_major(y_s, bs)
    return (y_prompt, y_sample,
            jnp.stack(outs["dp"]), jnp.stack(outs["qp"]), jnp.stack(outs["bp"]), jnp.stack(outs["cp"]),
            jnp.stack(outs["ds"]), jnp.stack(outs["qs"]), jnp.stack(outs["bs"]), jnp.stack(outs["cs"]))
```
